```python
import math
import jax, jax.numpy as jnp
from jax import lax
import numpy as np


D_MODEL = 1024
BATCH = 8
SEQ = 8192
DEPTH = 1

MIX_WIDTH = D_MODEL
NSA_WIDTH = MIX_WIDTH // 2
SSD_WIDTH = MIX_WIDTH - NSA_WIDTH

NSA_HEAD_DIM = 64
NSA_HEADS = NSA_WIDTH // NSA_HEAD_DIM
NSA_KV_GROUPS = 2
NSA_HPG = NSA_HEADS // NSA_KV_GROUPS
NSA_KV_WIDTH = NSA_KV_GROUPS * NSA_HEAD_DIM
N_BRANCH = 3
CMP_BLOCK = 32
CMP_STRIDE = 16
CMP_HIDDEN = 256
SEL_BLOCK = 64
SEL_TOPN = 16
WINDOW = 512
Q_BLOCK = 128
ROPE_THETA = 500000.0
ROPE_DIM = NSA_HEAD_DIM // 4

SSD_HEAD_DIM = 64
SSD_HEADS = SSD_WIDTH // SSD_HEAD_DIM
SSD_GROUPS = 2
SSD_RPG = SSD_HEADS // SSD_GROUPS
SSD_STATE = 128
SSD_CONV = 4
SSD_CHUNK = 128
SSD_CONV_CH = SSD_WIDTH + 2 * SSD_GROUPS * SSD_STATE

PEER_HEADS = 8
PEER_NKEYS = 128
PEER_EXPERTS = PEER_NKEYS * PEER_NKEYS
PEER_QDIM = 256
PEER_TOPK = 16
PEER_TOKEN_BLOCK = 256

IN_SPLIT_SIZES = (NSA_WIDTH,) + (NSA_KV_WIDTH,) * 6 + (NSA_HEADS * N_BRANCH, SSD_WIDTH, SSD_CONV_CH, SSD_HEADS)
IN_COLS = sum(IN_SPLIT_SIZES)

NORM_EPS = 1e-6
NEG_INF = -1e30
FORCE_BONUS = 1e4

kernel_name = 'hymba_nsa_ssd_peer_adaln_block'


def rmsnorm(x, w):
    xf = x.astype(jnp.float32)
    y = xf * lax.rsqrt(jnp.mean(xf * xf, axis=-1, keepdims=True) + NORM_EPS)
    return (y * w.astype(jnp.float32)).astype(x.dtype)


def modulate(x, w, shift, scale):
    return rmsnorm(x, w) * (1 + scale[:, None, :]) + shift[:, None, :]


def masked_softmax(s, mask):
    s = jnp.where(mask, s.astype(jnp.float32), NEG_INF)
    p = jax.nn.softmax(s, axis=-1)
    return jnp.where(mask, p, 0.0)


def partial_rope(x, pos):
    half = ROPE_DIM // 2
    inv_freq = ROPE_THETA ** (-jnp.arange(half, dtype=jnp.float32) / half)
    ang = pos.astype(jnp.float32)[:, None] * inv_freq[None, :]
    cos, sin = jnp.cos(ang), jnp.sin(ang)
    xr = x[..., :ROPE_DIM].astype(jnp.float32)
    x1, x2 = xr[..., :half], xr[..., half:]
    rot = jnp.concatenate([x1 * cos - x2 * sin, x2 * cos + x1 * sin], axis=-1).astype(x.dtype)
    return jnp.concatenate([rot, x[..., ROPE_DIM:]], axis=-1)


def nsa_mixer(q, k_cmp, v_cmp, k_sel, v_sel, k_win, v_win, gate_logits,
              pe_k, pe_v, w1_k, w2_k, w1_v, w2_v):
    bsz, s, _ = q.shape
    dh = NSA_HEAD_DIM
    G, R = NSA_KV_GROUPS, NSA_HPG
    pos = jnp.arange(s)
    scale = NSA_HEAD_DIM ** -0.5

    qh = q.reshape(bsz, s, G, R, dh).transpose(0, 2, 3, 1, 4)
    qr = partial_rope(qh, pos)

    def heads_kv(t):
        return t.reshape(bsz, s, G, dh).transpose(0, 2, 1, 3)

    k_cmp, v_cmp, k_sel, v_sel, k_win, v_win = (heads_kv(t) for t in (k_cmp, v_cmp, k_sel, v_sel, k_win, v_win))
    k_sel = partial_rope(k_sel, pos)
    k_win = partial_rope(k_win, pos)
    gates = jax.nn.sigmoid(gate_logits.astype(jnp.float32)).astype(q.dtype)
    gates = gates.reshape(bsz, s, G, R, N_BRANCH).transpose(0, 2, 3, 1, 4)

    n_cmp = (s - CMP_BLOCK) // CMP_STRIDE + 1
    cmp_start = jnp.arange(n_cmp) * CMP_STRIDE
    cmp_idx = cmp_start[:, None] + jnp.arange(CMP_BLOCK)[None, :]

    def compress(kv, pe, w1, w2):
        blk = (kv[:, :, cmp_idx] + pe).reshape(bsz, G, n_cmp, CMP_BLOCK * dh)
        return jax.nn.gelu(blk @ w1) @ w2

    kc = compress(k_cmp, pe_k, w1_k, w2_k)
    vc = compress(v_cmp, pe_v, w1_v, w2_v)
    cmp_end = cmp_start + CMP_BLOCK - 1

    n_sel = s // SEL_BLOCK
    n_top = min(SEL_TOPN, n_sel)
    sel_start = jnp.arange(n_sel) * SEL_BLOCK
    overlap = jnp.maximum(
        jnp.minimum(cmp_start[:, None] + CMP_BLOCK, sel_start[None, :] + SEL_BLOCK)
        - jnp.maximum(cmp_start[:, None], sel_start[None, :]), 0).astype(jnp.float32) / CMP_BLOCK
    ks_blk = k_sel.reshape(bsz, G, n_sel, SEL_BLOCK, dh)
    vs_blk = v_sel.reshape(bsz, G, n_sel, SEL_BLOCK, dh)
    gather_blocks = jax.vmap(jax.vmap(lambda blocks, idx: blocks[idx]))

    kw_pad = jnp.pad(k_win, ((0, 0), (0, 0), (WINDOW, 0), (0, 0)))
    vw_pad = jnp.pad(v_win, ((0, 0), (0, 0), (WINDOW, 0), (0, 0)))
    sel_off = jnp.arange(SEL_BLOCK)
    jj = jnp.arange(n_sel)

    def query_block(qb):
        q0 = qb * Q_BLOCK
        t = q0 + jnp.arange(Q_BLOCK)
        qc = lax.dynamic_slice_in_dim(qh, q0, Q_BLOCK, axis=3)
        qrc = lax.dynamic_slice_in_dim(qr, q0, Q_BLOCK, axis=3)
        g = lax.dynamic_slice_in_dim(gates, q0, Q_BLOCK, axis=3)

        s_c = jnp.einsum('bgrtd,bgnd->bgrtn', qc, kc) * scale
        p_c = masked_softmax(s_c, cmp_end[None, :] <= t[:, None])
        o_c = jnp.einsum('bgrtn,bgnd->bgrtd', p_c.astype(vc.dtype), vc)

        imp = jnp.einsum('bgrtn,nj->bgtj', p_c, overlap)
        blk_t = t // SEL_BLOCK
        forced = (jj[None, :] == 0) | (jj[None, :] == blk_t[:, None]) | (jj[None, :] == blk_t[:, None] - 1)
        imp = jnp.where(sel_start[None, :] <= t[:, None], imp + jnp.where(forced, FORCE_BONUS, 0.0), NEG_INF)
        _, sel = lax.top_k(imp, n_top)

        kg = gather_blocks(ks_blk, sel).reshape(bsz, G, Q_BLOCK, n_top * SEL_BLOCK, dh)
        vg = gather_blocks(vs_blk, sel).reshape(bsz, G, Q_BLOCK, n_top * SEL_BLOCK, dh)
        kpos = (sel[..., None] * SEL_BLOCK + sel_off).reshape(bsz, G, Q_BLOCK, n_top * SEL_BLOCK)
        s_s = jnp.einsum('bgrtd,bgtmd->bgrtm', qrc, kg) * scale
        p_s = masked_softmax(s_s, (kpos <= t[None, None, :, None])[:, :, None])
        o_s = jnp.einsum('bgrtm,bgtmd->bgrtd', p_s.astype(vg.dtype), vg)

        kw = lax.dynamic_slice_in_dim(kw_pad, q0, Q_BLOCK + WINDOW, axis=2)
        vw = lax.dynamic_slice_in_dim(vw_pad, q0, Q_BLOCK + WINDOW, axis=2)
        kpos_w = q0 - WINDOW + jnp.arange(Q_BLOCK + WINDOW)
        m_w = (kpos_w[None, :] >= 0) & (kpos_w[None, :] <= t[:, None]) & (t[:, None] - kpos_w[None, :] < WINDOW)
        s_w = jnp.einsum('bgrtd,bgkd->bgrtk', qrc, kw) * scale
        p_w = masked_softmax(s_w, m_w)
        o_w = jnp.einsum('bgrtk,bgkd->bgrtd', p_w.astype(vw.dtype), vw)

        return g[..., 0:1] * o_c + g[..., 1:2] * o_s + g[..., 2:3] * o_w

    out = lax.map(query_block, jnp.arange(s // Q_BLOCK))
    out = out.transpose(1, 2, 3, 0, 4, 5).reshape(bsz, G, R, s, dh)
    return out.transpose(0, 3, 1, 2, 4).reshape(bsz, s, NSA_WIDTH)


def causal_dwconv(x, w, b):
    k = w.shape[0]
    y = lax.conv_general_dilated(x, w[:, None, :].astype(x.dtype), window_strides=(1,), padding=[(k - 1, 0)],
                                 dimension_numbers=('NWC', 'WIO', 'NWC'), feature_group_count=x.shape[-1])
    return y + b


def ssd_scan(xs, dt, a, bm, cm):
    bsz, s = xs.shape[:2]
    L = SSD_CHUNK
    nc = s // L
    G, R, P, N = SSD_GROUPS, SSD_RPG, SSD_HEAD_DIM, SSD_STATE
    X = (xs * dt[..., None]).reshape(bsz, nc, L, G, R, P)
    A = (dt * a).reshape(bsz, nc, L, G, R).transpose(0, 3, 4, 1, 2)
    Bc = bm.reshape(bsz, nc, L, G, N)
    Cc = cm.reshape(bsz, nc, L, G, N)
    A_cs = jnp.cumsum(A, axis=-1)
    causal = jnp.tril(jnp.ones((L, L), dtype=bool))
    Lm = jnp.exp(jnp.where(causal, A_cs[..., :, None] - A_cs[..., None, :], -jnp.inf))
    CB = jnp.einsum('bclgn,bcsgn->bgcls', Cc, Bc)
    Y_diag = jnp.einsum('bgrcls,bcsgrp->bclgrp', CB[:, :, None] * Lm, X)
    decay_states = jnp.exp(A_cs[..., -1:] - A_cs)
    states = jnp.einsum('bclgn,bgrcl,bclgrp->bcgrpn', Bc, decay_states, X)
    chunk_decay = jnp.exp(A_cs[..., -1])

    def step(h, inp):
        st, d = inp
        return h * d[..., None, None] + st, h

    h0 = jnp.zeros((bsz, G, R, P, N), jnp.float32)
    _, prev = lax.scan(step, h0, (states.transpose(1, 0, 2, 3, 4, 5), chunk_decay.transpose(3, 0, 1, 2)))
    prev = prev.transpose(1, 0, 2, 3, 4, 5)
    Y_off = jnp.einsum('bclgn,bcgrpn,bgrcl->bclgrp', Cc, prev, jnp.exp(A_cs))
    return (Y_diag + Y_off).reshape(bsz, s, G, R, P)


def ssd_mixer(z, xbc, dt_raw, conv_w, conv_b, dt_bias, a_log, d_skip, norm_w):
    bsz, s, _ = z.shape
    f32 = jnp.float32
    xbc = jax.nn.silu(causal_dwconv(xbc, conv_w, conv_b)).astype(f32)
    xs, bm, cm = jnp.split(xbc, [SSD_WIDTH, SSD_WIDTH + SSD_GROUPS * SSD_STATE], axis=-1)
    xs = xs.reshape(bsz, s, SSD_GROUPS, SSD_RPG, SSD_HEAD_DIM)
    bm = bm.reshape(bsz, s, SSD_GROUPS, SSD_STATE)
    cm = cm.reshape(bsz, s, SSD_GROUPS, SSD_STATE)
    dt = jax.nn.softplus(dt_raw.astype(f32) + dt_bias.astype(f32)).reshape(bsz, s, SSD_GROUPS, SSD_RPG)
    a = -jnp.exp(a_log.astype(f32)).reshape(SSD_GROUPS, SSD_RPG)
    y = ssd_scan(xs, dt, a, bm, cm) + xs * d_skip.astype(f32).reshape(SSD_GROUPS, SSD_RPG)[:, :, None]
    y = y.reshape(bsz, s, SSD_WIDTH) * jax.nn.silu(z.astype(f32))
    return rmsnorm(y, norm_w).astype(z.dtype)


def peer_ffn(h, w_q, q_norm_w, sub_keys, expert_down, expert_up):
    bsz, s, d = h.shape
    n_tok = bsz * s
    tb = math.gcd(n_tok, PEER_TOKEN_BLOCK)
    half = PEER_QDIM // 2

    def token_block(xb):
        q = rmsnorm((xb @ w_q).reshape(tb, PEER_HEADS, PEER_QDIM), q_norm_w)
        q = q.reshape(tb, PEER_HEADS, 2, half)
        sc = jnp.einsum('thkd,hknd->thkn', q, sub_keys)
        s1, i1 = lax.top_k(sc[:, :, 0], PEER_TOPK)
        s2, i2 = lax.top_k(sc[:, :, 1], PEER_TOPK)
        cand_s = (s1[..., :, None] + s2[..., None, :]).reshape(tb, PEER_HEADS, PEER_TOPK * PEER_TOPK)
        cand_i = (i1[..., :, None] * PEER_NKEYS + i2[..., None, :]).reshape(tb, PEER_HEADS, PEER_TOPK * PEER_TOPK)
        top_s, top_pos = lax.top_k(cand_s, PEER_TOPK)
        eidx = jnp.take_along_axis(cand_i, top_pos, axis=-1)
        gate = jax.nn.softmax(top_s.astype(jnp.float32), axis=-1)
        u = expert_down[eidx]
        v = expert_up[eidx]
        act = jax.nn.gelu(jnp.einsum('td,thkd->thk', xb, u))
        return jnp.einsum('thk,thkd->td', (gate * act).astype(v.dtype), v)

    y = lax.map(token_block, h.reshape(n_tok // tb, tb, d))
    return y.reshape(bsz, s, d)


def setup_inputs(seed: int = 0) -> dict:
    key = jax.random.key(seed)
    ks = jax.random.split(key, 32)
    f32 = jnp.float32
    L = DEPTH

    def nrm(k, shape, scale):
        return jax.random.normal(k, shape, f32) * scale

    dt0 = jnp.exp(jax.random.uniform(ks[17], (L, SSD_HEADS), f32, math.log(1e-3), math.log(1e-1)))
    return {
        'x': nrm(ks[0], (BATCH, SEQ, D_MODEL), 1.0),
        'c': nrm(ks[1], (BATCH, D_MODEL), 1.0),
        'w_ada': nrm(ks[2], (L, D_MODEL, 6 * D_MODEL), 0.5 * D_MODEL ** -0.5),
        'b_ada': nrm(ks[3], (L, 6 * D_MODEL), 0.02),
        'norm1_w': 1.0 + nrm(ks[4], (L, D_MODEL), 0.02),
        'w_in': nrm(ks[5], (L, D_MODEL, IN_COLS), D_MODEL ** -0.5),
        'cmp_pe_k': nrm(ks[6], (L, CMP_BLOCK, NSA_HEAD_DIM), 0.1),
        'cmp_pe_v': nrm(ks[7], (L, CMP_BLOCK, NSA_HEAD_DIM), 0.1),
        'cmp_w1_k': nrm(ks[8], (L, CMP_BLOCK * NSA_HEAD_DIM, CMP_HIDDEN), (CMP_BLOCK * NSA_HEAD_DIM) ** -0.5),
        'cmp_w2_k': nrm(ks[9], (L, CMP_HIDDEN, NSA_HEAD_DIM), CMP_HIDDEN ** -0.5),
        'cmp_w1_v': nrm(ks[10], (L, CMP_BLOCK * NSA_HEAD_DIM, CMP_HIDDEN), (CMP_BLOCK * NSA_HEAD_DIM) ** -0.5),
        'cmp_w2_v': nrm(ks[11], (L, CMP_HIDDEN, NSA_HEAD_DIM), CMP_HIDDEN ** -0.5),
        'nsa_norm_w': 1.0 + nrm(ks[12], (L, NSA_WIDTH), 0.02),
        'conv_w': nrm(ks[13], (L, SSD_CONV, SSD_CONV_CH), SSD_CONV ** -0.5),
        'conv_b': nrm(ks[14], (L, SSD_CONV_CH), 0.02),
        'dt_bias': dt0 + jnp.log(-jnp.expm1(-dt0)),
        'a_log': jnp.log(jax.random.uniform(ks[15], (L, SSD_HEADS), f32, 1.0, 16.0)),
        'd_skip': 1.0 + nrm(ks[16], (L, SSD_HEADS), 0.02),
        'ssd_norm_w': 1.0 + nrm(ks[18], (L, SSD_WIDTH), 0.02),
        'w_out': nrm(ks[19], (L, MIX_WIDTH, D_MODEL), MIX_WIDTH ** -0.5),
        'norm2_w': 1.0 + nrm(ks[20], (L, D_MODEL), 0.02),
        'peer_wq': nrm(ks[21], (L, D_MODEL, PEER_HEADS * PEER_QDIM), D_MODEL ** -0.5),
        'peer_qnorm_w': 1.0 + nrm(ks[22], (L, PEER_QDIM), 0.02),
        'peer_sub_keys': nrm(ks[23], (L, PEER_HEADS, 2, PEER_NKEYS, PEER_QDIM // 2), (PEER_QDIM // 2) ** -0.5),
        'peer_down': nrm(ks[24], (L, PEER_EXPERTS, D_MODEL), D_MODEL ** -0.5),
        'peer_up': nrm(ks[25], (L, PEER_EXPERTS, D_MODEL), PEER_HEADS ** -0.5),
        'final_norm_w': 1.0 + nrm(ks[26], (D_MODEL,), 0.02),
    }


def reference(x, c, w_ada, b_ada, norm1_w, w_in, cmp_pe_k, cmp_pe_v, cmp_w1_k, cmp_w2_k, cmp_w1_v, cmp_w2_v,
              nsa_norm_w, conv_w, conv_b, dt_bias, a_log, d_skip, ssd_norm_w, w_out, norm2_w,
              peer_wq, peer_qnorm_w, peer_sub_keys, peer_down, peer_up, final_norm_w):
    split_at = np.cumsum(IN_SPLIT_SIZES)[:-1].tolist()
    for layer in range(DEPTH):
        mod = c @ w_ada[layer] + b_ada[layer]
        sh1, sc1, g1, sh2, sc2, g2 = jnp.split(mod, 6, axis=-1)

        h = modulate(x, norm1_w[layer], sh1, sc1)
        proj = h @ w_in[layer]
        q, kc, vc, ksel, vsel, kwin, vwin, gl, z, xbc, dtr = jnp.split(proj, split_at, axis=-1)
        o_nsa = rmsnorm(nsa_mixer(q, kc, vc, ksel, vsel, kwin, vwin, gl,
                                  cmp_pe_k[layer], cmp_pe_v[layer], cmp_w1_k[layer], cmp_w2_k[layer],
                                  cmp_w1_v[layer], cmp_w2_v[layer]), nsa_norm_w[layer])
        o_ssd = ssd_mixer(z, xbc, dtr, conv_w[layer], conv_b[layer], dt_bias[layer], a_log[layer],
                          d_skip[layer], ssd_norm_w[layer])
        mix = jnp.concatenate([o_nsa, o_ssd], axis=-1) @ w_out[layer]
        x = x + g1[:, None, :] * mix

        h2 = modulate(x, norm2_w[layer], sh2, sc2)
        x = x + g2[:, None, :] * peer_ffn(h2, peer_wq[layer], peer_qnorm_w[layer], peer_sub_keys[layer],
                                          peer_down[layer], peer_up[layer])
    return rmsnorm(x, final_norm_w)
```

```python
import functools
import math

import numpy as np
import jax
import jax.numpy as jnp
from jax import lax
from jax.experimental import pallas as pl
from jax.experimental.pallas import tpu as pltpu

F32 = jnp.float32
BF16 = jnp.bfloat16
HIGHEST = lax.Precision.HIGHEST

D_MODEL = 1024
NSA_WIDTH = 512
SSD_WIDTH = 512
HEAD_DIM = 64
NSA_HEADS = 8
NSA_GROUPS = 2
NSA_HPG = 4
N_BRANCH = 3
CMP_BLOCK = 32
CMP_STRIDE = 16
CMP_HIDDEN = 256
SEL_BLOCK = 64
SEL_TOPN = 16
WINDOW = 512
ROPE_THETA = 500000.0
ROPE_DIM = 16
SSD_HEADS = 8
SSD_GROUPS = 2
SSD_STATE = 128
SSD_CONV = 4
SSD_CHUNK = 128
PEER_HEADS = 8
PEER_NKEYS = 128
PEER_QDIM = 256
PEER_TOPK = 16
NORM_EPS = 1e-6
NEG_INF = -1e30
FORCE_BONUS = 1e4

LANES = 128
VMEM_LIMIT = 56 * 1024 * 1024

C_Q, C_QSW, C_KSEL, C_KSELSW, C_KWIN, C_KWINSW = 0, 512, 1024, 1152, 1280, 1408
C_VSEL, C_VWIN, C_KCMP, C_VCMP, C_GATE, C_Z, C_XBC, C_DT, C_END = 1536, 1664, 1792, 1920, 2048, 2304, 2816, 3840, 3968


def _gelu_tanh(x):
    c = math.sqrt(2.0 / math.pi)
    return 0.5 * x * (1.0 + jnp.tanh(c * (x + 0.044715 * (x * x * x))))


def _nt(a, b):
    return lax.dot_general(a, b, (((1,), (1,)), ((), ())), preferred_element_type=F32)


def _cparams(sem):
    return pltpu.CompilerParams(dimension_semantics=sem, vmem_limit_bytes=VMEM_LIMIT)


def _mod_kernel(c_ref, w_ref, b_ref, o_ref):
    o_ref[...] = jnp.dot(c_ref[...], w_ref[...], preferred_element_type=F32, precision=HIGHEST) + b_ref[...]


def _mod_call(c, w_ada, b_ada):
    bsz = c.shape[0]
    n = w_ada.shape[1]
    return pl.pallas_call(
        _mod_kernel,
        grid=(n // D_MODEL,),
        in_specs=[pl.BlockSpec((bsz, D_MODEL), lambda j: (0, 0)),
                  pl.BlockSpec((D_MODEL, D_MODEL), lambda j: (0, j)),
                  pl.BlockSpec((1, D_MODEL), lambda j: (0, j))],
        out_specs=pl.BlockSpec((bsz, D_MODEL), lambda j: (0, j)),
        out_shape=jax.ShapeDtypeStruct((bsz, n), F32),
        compiler_params=_cparams(("arbitrary",)),
        name="mod",
    )(c, w_ada, b_ada.reshape(1, n))


def _inproj_kernel(x_ref, mod_ref, nw_ref, w_ref, cos_ref, sin_ref,
                   qp_ref, qr_ref, ksel_ref, kwin_ref, vsel_ref, vwin_ref,
                   kcmp_ref, vcmp_ref, gate_ref, z_ref, xbc_ref, dt_ref):
    x = x_ref[0]
    ms = jnp.mean(x * x, axis=-1, keepdims=True)
    y = x * lax.rsqrt(ms + NORM_EPS) * nw_ref[...]
    h = (y * (1.0 + mod_ref[0, 1:2, :]) + mod_ref[0, 0:1, :]).astype(BF16)

    def proj(lo, hi):
        return jnp.dot(h, w_ref[:, lo:hi], preferred_element_type=F32)

    cos = cos_ref[...]
    sin = sin_ref[...]
    scale = HEAD_DIM ** -0.5
    q = proj(C_Q, C_Q + 512)
    qp_ref[0] = (q * scale).astype(BF16)
    qsw = proj(C_QSW, C_QSW + 512)
    cos4 = jnp.concatenate([cos] * 4, axis=1)
    sin4 = jnp.concatenate([sin] * 4, axis=1)
    qr_ref[0] = ((q * cos4 + qsw * sin4) * scale).astype(BF16)

    ks = proj(C_KSEL, C_KSEL + 128) * cos + proj(C_KSELSW, C_KSELSW + 128) * sin
    kw = proj(C_KWIN, C_KWIN + 128) * cos + proj(C_KWINSW, C_KWINSW + 128) * sin
    vs = proj(C_VSEL, C_VSEL + 128)
    vw = proj(C_VWIN, C_VWIN + 128)
    for ref, val in ((ksel_ref, ks), (kwin_ref, kw), (vsel_ref, vs), (vwin_ref, vw)):
        ref[0, 0] = val[:, :HEAD_DIM].astype(BF16)
        ref[0, 1] = val[:, HEAD_DIM:].astype(BF16)
    kcmp_ref[0] = proj(C_KCMP, C_KCMP + 128)
    vcmp_ref[0] = proj(C_VCMP, C_VCMP + 128)
    gl = proj(C_GATE, C_GATE + 256)
    sg = 1.0 / (1.0 + jnp.exp(-gl))
    gate_ref[0, 0] = sg[:, :128]
    gate_ref[0, 1] = sg[:, 128:]
    z_ref[0] = proj(C_Z, C_Z + 512)
    xbc_ref[0] = proj(C_XBC, C_XBC + 1024)
    dt_ref[0] = proj(C_DT, C_DT + 128)


def _inproj_call(x, mod, norm_w, w_ext, cos_t, sin_t, ts):
    bsz, s, d = x.shape
    grid = (bsz, s // ts)
    tok = lambda w: pl.BlockSpec((1, ts, w), lambda b, i: (b, i, 0))
    hm = pl.BlockSpec((1, NSA_GROUPS, ts, HEAD_DIM), lambda b, i: (b, 0, i, 0))
    sd = jax.ShapeDtypeStruct
    out_shape = [sd((bsz, s, 512), BF16), sd((bsz, s, 512), BF16)] + \
                [sd((bsz, NSA_GROUPS, s, HEAD_DIM), BF16)] * 4 + \
                [sd((bsz, s, 128), F32), sd((bsz, s, 128), F32),
                 sd((bsz, NSA_GROUPS, s, 128), F32),
                 sd((bsz, s, 512), F32), sd((bsz, s, 1024), F32), sd((bsz, s, 128), F32)]
    out_specs = [tok(512), tok(512), hm, hm, hm, hm, tok(128), tok(128),
                 pl.BlockSpec((1, NSA_GROUPS, ts, 128), lambda b, i: (b, 0, i, 0)),
                 tok(512), tok(1024), tok(128)]
    return pl.pallas_call(
        _inproj_kernel,
        grid=grid,
        in_specs=[tok(d),
                  pl.BlockSpec((1, 6, d), lambda b, i: (b, 0, 0)),
                  pl.BlockSpec((1, d), lambda b, i: (0, 0)),
                  pl.BlockSpec((d, C_END), lambda b, i: (0, 0)),
                  pl.BlockSpec((ts, 128), lambda b, i: (i, 0)),
                  pl.BlockSpec((ts, 128), lambda b, i: (i, 0))],
        out_specs=out_specs,
        out_shape=out_shape,
        compiler_params=_cparams(("parallel", "parallel")),
        name="inproj",
    )(x, mod, norm_w, w_ext, cos_t, sin_t)


def _compress_kernel(kx_ref, vx_ref, pek_ref, pev_ref, w1k_ref, w2k_ref, w1v_ref, w2v_ref,
                     kc_ref, vc_ref, hb_ref):
    nch = kx_ref.shape[2]

    def one(x_ref, pe_ref, w1_ref, w2_ref, o_ref):
        x = x_ref[0, 0]
        top = jnp.dot((x + pe_ref[0:1, :]).astype(BF16), w1_ref[0], preferred_element_type=F32)
        bot = jnp.dot((x + pe_ref[1:2, :]).astype(BF16), w1_ref[1], preferred_element_type=F32)
        hb_ref[0:nch, :] = bot
        hb_ref[nch:nch + 8, :] = jnp.zeros((8, CMP_HIDDEN), F32)
        pre = top + hb_ref[pl.ds(1, nch), :]
        out = jnp.dot(_gelu_tanh(pre).astype(BF16), w2_ref[...], preferred_element_type=F32)
        row = lax.broadcasted_iota(jnp.int32, out.shape, 0)
        o_ref[0, 0] = jnp.where(row < nch - 1, out, 0.0).astype(BF16)

    one(kx_ref, pek_ref, w1k_ref, w2k_ref, kc_ref)
    one(vx_ref, pev_ref, w1v_ref, w2v_ref, vc_ref)


def _compress_call(kx, vx, pek, pev, w1k, w2k, w1v, w2v):
    bsz, g, nch, cw = kx.shape
    xs = pl.BlockSpec((1, 1, nch, cw), lambda b, gi: (b, gi, 0, 0))
    full = lambda a: pl.BlockSpec(a.shape, lambda b, gi: (0,) * a.ndim)
    os_ = pl.BlockSpec((1, 1, nch, HEAD_DIM), lambda b, gi: (b, gi, 0, 0))
    sd = jax.ShapeDtypeStruct((bsz, g, nch, HEAD_DIM), BF16)
    return pl.pallas_call(
        _compress_kernel,
        grid=(bsz, g),
        in_specs=[xs, xs, full(pek), full(pev), full(w1k), full(w2k), full(w1v), full(w2v)],
        out_specs=[os_, os_],
        out_shape=[sd, sd],
        scratch_shapes=[pltpu.VMEM((nch + 8, CMP_HIDDEN), F32)],
        compiler_params=_cparams(("parallel", "parallel")),
        name="compress",
    )(kx, vx, pek, pev, w1k, w2k, w1v, w2v)


NSA_TQ = 128
NSA_TK = 256


def _nsa_kernel(qp_ref, qr_ref, kc_ref, vc_ref, ks_ref, vs_ref, kw_ref, vw_ref, gate_ref, ovt_ref, o_ref):
    tq = NSA_TQ
    rows = NSA_HPG * tq
    qt = pl.program_id(2)
    q0 = qt * tq
    ncmp = kc_ref.shape[2]
    nsel = ovt_ref.shape[0]

    qp = jnp.concatenate([qp_ref[0, :, r * HEAD_DIM:(r + 1) * HEAD_DIM] for r in range(NSA_HPG)], axis=0)
    qr = jnp.concatenate([qr_ref[0, :, r * HEAD_DIM:(r + 1) * HEAD_DIM] for r in range(NSA_HPG)], axis=0)

    t_row = q0 + lax.broadcasted_iota(jnp.int32, (tq, ncmp), 0)
    cend = lax.broadcasted_iota(jnp.int32, (tq, ncmp), 1) * CMP_STRIDE + (CMP_BLOCK - 1)
    cbias1 = jnp.where(cend <= t_row, 0.0, NEG_INF)
    cbias = jnp.concatenate([cbias1] * NSA_HPG, axis=0)
    s = _nt(qp, kc_ref[0, 0])
    s = jnp.where(cbias == 0.0, s, NEG_INF)
    m = jnp.max(s, axis=-1, keepdims=True)
    e = jnp.exp(s - m)
    p = jnp.where(cbias == 0.0, e / jnp.sum(e, axis=-1, keepdims=True), 0.0)
    o_c = jnp.dot(p.astype(BF16), vc_ref[0, 0], preferred_element_type=F32)

    psum = p[0:tq] + p[tq:2 * tq] + p[2 * tq:3 * tq] + p[3 * tq:4 * tq]
    p_hi = psum.astype(BF16)
    p_lo = (psum - p_hi.astype(F32)).astype(BF16)
    ovt = ovt_ref[...]
    imp = _nt(ovt, p_hi) + _nt(ovt, p_lo)
    jblk = lax.broadcasted_iota(jnp.int32, (nsel, tq), 0)
    tt = q0 + lax.broadcasted_iota(jnp.int32, (nsel, tq), 1)
    blk_t = jnp.right_shift(tt, 6)
    forced = (jblk == 0) | (jblk == blk_t) | (jblk == blk_t - 1)
    xs = jnp.where(jblk * SEL_BLOCK <= tt, imp + jnp.where(forced, FORCE_BONUS, 0.0), NEG_INF)
    sel_t = jnp.zeros((nsel, tq), F32)
    jblk_f = jblk.astype(F32)
    for _ in range(SEL_TOPN):
        mx = jnp.max(xs, axis=0, keepdims=True)
        idx = jnp.min(jnp.where(xs == mx, jblk_f, float(nsel)), axis=0, keepdims=True)
        hit = jblk_f == idx
        sel_t = jnp.where(hit, 1.0, sel_t)
        xs = jnp.where(hit, -jnp.inf, xs)
    sel = jnp.transpose(sel_t).astype(BF16)

    def online(carry, sc, v):
        m_i, l_i, acc = carry
        m_new = jnp.maximum(m_i, jnp.max(sc, axis=-1, keepdims=True))
        alpha = jnp.exp(m_i - m_new)
        pe = jnp.exp(sc - m_new)
        l_new = alpha * l_i + jnp.sum(pe, axis=-1, keepdims=True)
        acc_new = alpha * acc + jnp.dot(pe.astype(BF16), v, preferred_element_type=F32)
        return m_new, l_new, acc_new

    init = (jnp.full((rows, 1), NEG_INF, F32), jnp.zeros((rows, 1), F32), jnp.zeros((rows, HEAD_DIM), F32))

    tk = NSA_TK
    tq_s = q0 + lax.broadcasted_iota(jnp.int32, (tq, tk), 0)

    def sel_body(kt, carry):
        kbase = pl.multiple_of(kt * tk, tk)
        k = ks_ref[0, 0, pl.ds(kbase, tk), :]
        v = vs_ref[0, 0, pl.ds(kbase, tk), :]
        eb = lax.broadcasted_iota(jnp.int32, (nsel, tk), 0) == \
            jnp.right_shift(kbase + lax.broadcasted_iota(jnp.int32, (nsel, tk), 1), 6)
        mk = jnp.dot(sel, jnp.where(eb, 1.0, 0.0).astype(BF16), preferred_element_type=F32)
        kpos = kbase + lax.broadcasted_iota(jnp.int32, (tq, tk), 1)
        bias1 = jnp.where((mk > 0.5) & (kpos <= tq_s), 0.0, NEG_INF)
        sc = _nt(qr, k) + jnp.concatenate([bias1] * NSA_HPG, axis=0)
        return online(carry, sc, v)

    n_kt = (q0 + tq + tk - 1) // tk
    _, l_s, acc_s = lax.fori_loop(0, n_kt, sel_body, init)
    o_s = acc_s / l_s

    tq_w = q0 + lax.broadcasted_iota(jnp.int32, (tq, tq), 0)

    def win_body(kt, carry):
        kbase = pl.multiple_of(kt * tq, tq)
        k = kw_ref[0, 0, pl.ds(kbase, tq), :]
        v = vw_ref[0, 0, pl.ds(kbase, tq), :]
        kpos = kbase + lax.broadcasted_iota(jnp.int32, (tq, tq), 1)
        bias1 = jnp.where((kpos <= tq_w) & (tq_w - kpos < WINDOW), 0.0, NEG_INF)
        sc = _nt(qr, k) + jnp.concatenate([bias1] * NSA_HPG, axis=0)
        return online(carry, sc, v)

    _, l_w, acc_w = lax.fori_loop(jnp.maximum(qt - WINDOW // tq, 0), qt + 1, win_body, init)
    o_w = acc_w / l_w

    gates = gate_ref[0, 0]
    for r in range(NSA_HPG):
        sl = slice(r * tq, (r + 1) * tq)
        o_r = (gates[:, 3 * r:3 * r + 1] * o_c[sl] + gates[:, 3 * r + 1:3 * r + 2] * o_s[sl]
               + gates[:, 3 * r + 2:3 * r + 3] * o_w[sl])
        o_ref[0, :, r * HEAD_DIM:(r + 1) * HEAD_DIM] = o_r


def _nsa_call(qp, qr, kc, vc, ks, vs, kw, vw, gates, ovt):
    bsz, s, _ = qp.shape
    tq = NSA_TQ
    ncmp = kc.shape[2]
    gw = NSA_HPG * HEAD_DIM
    qspec = pl.BlockSpec((1, tq, gw), lambda b, g, i: (b, i, g))
    cspec = pl.BlockSpec((1, 1, ncmp, HEAD_DIM), lambda b, g, i: (b, g, 0, 0))
    kvspec = pl.BlockSpec((1, 1, s, HEAD_DIM), lambda b, g, i: (b, g, 0, 0))
    return pl.pallas_call(
        _nsa_kernel,
        grid=(bsz, NSA_GROUPS, s // tq),
        in_specs=[qspec, qspec, cspec, cspec, kvspec, kvspec, kvspec, kvspec,
                  pl.BlockSpec((1, 1, tq, 128), lambda b, g, i: (b, g, i, 0)),
                  pl.BlockSpec(ovt.shape, lambda b, g, i: (0, 0))],
        out_specs=pl.BlockSpec((1, tq, gw), lambda b, g, i: (b, i, g)),
        out_shape=jax.ShapeDtypeStruct((bsz, s, NSA_WIDTH), F32),
        compiler_params=_cparams(("parallel", "parallel", "arbitrary")),
        name="nsa",
    )(qp, qr, kc, vc, ks, vs, kw, vw, gates, ovt)


def _softplus(x):
    return jnp.maximum(x, 0.0) + jnp.log1p(jnp.exp(-jnp.abs(x)))


def _ssd_kernel(xbc_ref, z_ref, dt_ref, dtt_ref, cw_ref, cb_ref, dtb_ref, dtbt_ref, al_ref, alt_ref,
                dsk_ref, nw_ref, o_ref, tail_ref, xp_ref, st_ref, y_ref):
    L = SSD_CHUNK
    P = HEAD_DIM
    N = SSD_STATE
    c = pl.program_id(1)

    @pl.when(c == 0)
    def _():
        tail_ref[...] = jnp.zeros(tail_ref.shape, F32)
        st_ref[...] = jnp.zeros(st_ref.shape, F32)

    xin = xbc_ref[0]
    xp_ref[0:8, :] = tail_ref[...]
    xp_ref[8:8 + L, :] = xin
    tail_ref[...] = xin[L - 8:L, :]
    conv = cb_ref[...] + jnp.zeros_like(xin)
    for k in range(SSD_CONV):
        conv = conv + cw_ref[k:k + 1, :] * xp_ref[pl.ds(8 - (SSD_CONV - 1) + k, L), :]
    u = conv * (1.0 / (1.0 + jnp.exp(-conv)))
    xs = u[:, :SSD_WIDTH]

    dt_c = _softplus(dt_ref[0] + dtb_ref[...])
    dt_r = _softplus(dtt_ref[0] + dtbt_ref[...])
    a_c = -jnp.exp(al_ref[...])
    a_r = -jnp.exp(alt_ref[...])
    li = lax.broadcasted_iota(jnp.int32, (L, L), 0)
    si = lax.broadcasted_iota(jnp.int32, (L, L), 1)
    causal = li >= si
    tri = jnp.where(causal, 1.0, 0.0)
    tri_t = jnp.where(li <= si, 1.0, 0.0)
    acs_c = jnp.dot(tri, dt_c * a_c, preferred_element_type=F32, precision=HIGHEST)
    acs_r = jnp.dot(dt_r * a_r, tri_t, preferred_element_type=F32, precision=HIGHEST)

    for g in range(SSD_GROUPS):
        bm = u[:, SSD_WIDTH + g * N:SSD_WIDTH + (g + 1) * N]
        cm = u[:, SSD_WIDTH + SSD_GROUPS * N + g * N:SSD_WIDTH + SSD_GROUPS * N + (g + 1) * N]
        bm_b = bm.astype(BF16)
        cm_b = cm.astype(BF16)
        cb = _nt(cm_b, bm_b)
        bm_t = jnp.transpose(bm)
        for r in range(SSD_HEADS // SSD_GROUPS):
            hh = g * (SSD_HEADS // SSD_GROUPS) + r
            col = acs_c[:, hh:hh + 1]
            row = acs_r[hh:hh + 1, :]
            last = acs_r[hh:hh + 1, L - 1:L]
            lm = jnp.exp(jnp.where(causal, col - row, NEG_INF))
            x_h = xs[:, hh * P:(hh + 1) * P]
            xd = x_h * dt_c[:, hh:hh + 1]
            y_d = jnp.dot((cb * lm).astype(BF16), xd.astype(BF16), preferred_element_type=F32)
            prev = st_ref[hh]
            y_o = jnp.dot(cm_b, prev.astype(BF16), preferred_element_type=F32) * jnp.exp(col)
            dec = jnp.exp(last - row)
            st_new = jnp.dot((bm_t * dec).astype(BF16), xd.astype(BF16), preferred_element_type=F32)
            st_ref[hh] = prev * jnp.exp(last) + st_new
            y_ref[:, hh * P:(hh + 1) * P] = y_d + y_o + x_h * dsk_ref[0:1, hh:hh + 1]

    zz = z_ref[0]
    y = y_ref[...] * (zz * (1.0 / (1.0 + jnp.exp(-zz))))
    ms = jnp.mean(y * y, axis=-1, keepdims=True)
    o_ref[0] = (y * lax.rsqrt(ms + NORM_EPS) * nw_ref[...]).astype(o_ref.dtype)


def _ssd_call(xbc, z, dt, dtt, conv_w, conv_b, dt_bias, a_log, d_skip, norm_w):
    bsz, s, cch = xbc.shape
    L = SSD_CHUNK
    pad = lambda v: jnp.pad(v.reshape(1, -1), ((0, 0), (0, 128 - v.size)))
    full = lambda a: pl.BlockSpec(a.shape, lambda b, i: (0,) * a.ndim)
    args = (xbc, z, dt, dtt, conv_w, conv_b.reshape(1, -1), pad(dt_bias), dt_bias.reshape(-1, 1),
            pad(a_log), a_log.reshape(-1, 1), pad(d_skip), norm_w.reshape(1, -1))
    in_specs = [pl.BlockSpec((1, L, cch), lambda b, i: (b, i, 0)),
                pl.BlockSpec((1, L, SSD_WIDTH), lambda b, i: (b, i, 0)),
                pl.BlockSpec((1, L, 128), lambda b, i: (b, i, 0)),
                pl.BlockSpec((1, SSD_HEADS, L), lambda b, i: (b, 0, i))] + [full(a) for a in args[4:]]
    return pl.pallas_call(
        _ssd_kernel,
        grid=(bsz, s // L),
        in_specs=in_specs,
        out_specs=pl.BlockSpec((1, L, SSD_WIDTH), lambda b, i: (b, i, 0)),
        out_shape=jax.ShapeDtypeStruct((bsz, s, SSD_WIDTH), BF16),
        scratch_shapes=[pltpu.VMEM((8, cch), F32), pltpu.VMEM((8 + L, cch), F32),
                        pltpu.VMEM((SSD_HEADS, SSD_STATE, HEAD_DIM), F32), pltpu.VMEM((L, SSD_WIDTH), F32)],
        compiler_params=_cparams(("parallel", "arbitrary")),
        name="ssd",
    )(*args)


def _outproj_kernel(on_ref, os_ref, x_ref, mod_ref, nnw_ref, n2w_ref, wo_ref, wq_ref, qnw_ref, sk_ref,
                    x1_ref, h2_ref, sc_ref):
    o = on_ref[0]
    ms = jnp.mean(o * o, axis=-1, keepdims=True)
    on = (o * lax.rsqrt(ms + NORM_EPS) * nnw_ref[...]).astype(BF16)
    mix = jnp.dot(on, wo_ref[0:NSA_WIDTH, :], preferred_element_type=F32) + \
        jnp.dot(os_ref[0], wo_ref[NSA_WIDTH:, :], preferred_element_type=F32)
    x1 = x_ref[0] + mod_ref[0, 2:3, :] * mix
    x1_ref[0] = x1
    ms2 = jnp.mean(x1 * x1, axis=-1, keepdims=True)
    h2 = ((x1 * lax.rsqrt(ms2 + NORM_EPS) * n2w_ref[...]) * (1.0 + mod_ref[0, 4:5, :]) + mod_ref[0, 3:4, :]).astype(BF16)
    h2_ref[0] = h2
    half = PEER_QDIM // 2
    for hh in range(PEER_HEADS):
        qh = jnp.dot(h2, wq_ref[:, hh * PEER_QDIM:(hh + 1) * PEER_QDIM], preferred_element_type=F32)
        qn = (qh * lax.rsqrt(jnp.mean(qh * qh, axis=-1, keepdims=True) + NORM_EPS) * qnw_ref[...]).astype(BF16)
        for k in range(2):
            sc_ref[2 * hh + k] = _nt(sk_ref[hh, k], qn[:, k * half:(k + 1) * half])


def _outproj_call(o_nsa, o_ssd, x, mod, nsa_nw, n2w, w_out, wq, qnw, sub_keys, ts):
    bsz, s, d = x.shape
    nblk = s // ts
    tok = lambda w: pl.BlockSpec((1, ts, w), lambda b, i: (b, i, 0))
    full = lambda a: pl.BlockSpec(a.shape, lambda b, i: (0,) * a.ndim)
    return pl.pallas_call(
        _outproj_kernel,
        grid=(bsz, nblk),
        in_specs=[tok(NSA_WIDTH), tok(SSD_WIDTH), tok(d), pl.BlockSpec((1, 6, d), lambda b, i: (b, 0, 0)),
                  full(nsa_nw), full(n2w), full(w_out), full(wq), full(qnw), full(sub_keys)],
        out_specs=[tok(d), tok(d),
                   pl.BlockSpec((2 * PEER_HEADS, PEER_NKEYS, ts), lambda b, i: (0, 0, b * nblk + i))],
        out_shape=[jax.ShapeDtypeStruct((bsz, s, d), F32), jax.ShapeDtypeStruct((bsz, s, d), BF16),
                   jax.ShapeDtypeStruct((2 * PEER_HEADS, PEER_NKEYS, bsz * s), F32)],
        compiler_params=_cparams(("parallel", "parallel")),
        name="outproj",
    )(o_nsa, o_ssd, x, mod, nsa_nw, n2w, w_out, wq, qnw, sub_keys)


PEER_TT = 128


def _peersel_kernel(sc_ref, c_ref, e1_ref, r2_ref, e2_ref):
    nk = PEER_NKEYS
    tt = PEER_TT
    K = PEER_TOPK
    kidx = lax.broadcasted_iota(jnp.int32, (nk, tt), 0).astype(F32)
    i8 = lax.broadcasted_iota(jnp.int32, (8, tt), 0)

    def topk_sorted(x):
        rank = jnp.full((nk, tt), float(K), F32)
        vals = []
        for j in range(K):
            mx = jnp.max(x, axis=0, keepdims=True)
            idx = jnp.min(jnp.where(x == mx, kidx, float(nk)), axis=0, keepdims=True)
            hit = kidx == idx
            rank = jnp.where(hit, float(j), rank)
            x = jnp.where(hit, -jnp.inf, x)
            vals.append(mx)
        return rank, vals

    def head(hh, carry):
        s1 = sc_ref[2 * hh]
        s2 = sc_ref[2 * hh + 1]
        r1, v1 = topk_sorted(s1)
        r2, v2 = topk_sorted(s2)
        v1_lo = jnp.zeros((8, tt), F32)
        v1_hi = jnp.zeros((8, tt), F32)
        for i in range(8):
            v1_lo = jnp.where(i8 == i, v1[i], v1_lo)
            v1_hi = jnp.where(i8 == i, v1[8 + i], v1_hi)
        cands = [(0, 0, v1_lo + v2[0]), (8, 0, v1_hi + v2[0])]
        for j in range(1, K):
            cands.append((0, j, jnp.where(i8 < K // (j + 1), v1_lo + v2[j], -jnp.inf)))
        pids = [((i8 + ib) * K + j).astype(F32) for ib, j, _ in cands]
        vals = [v for _, _, v in cands]
        cmax = v1[0] + v2[0]
        n_lo = jnp.zeros((8, tt), F32)
        n_hi = jnp.zeros((8, tt), F32)
        zsum = jnp.zeros((1, tt), F32)
        for _ in range(K):
            mx8 = vals[0]
            for v in vals[1:]:
                mx8 = jnp.maximum(mx8, v)
            mx = jnp.max(mx8, axis=0, keepdims=True)
            pm8 = jnp.where(vals[0] == mx, pids[0], float(K * K))
            for v, pid in zip(vals[1:], pids[1:]):
                pm8 = jnp.minimum(pm8, jnp.where(v == mx, pid, float(K * K)))
            pm = jnp.min(pm8, axis=0, keepdims=True)
            for ci in range(len(vals)):
                hit = pids[ci] == pm
                if ci == 1:
                    n_hi = n_hi + jnp.where(hit, 1.0, 0.0)
                else:
                    n_lo = n_lo + jnp.where(hit, 1.0, 0.0)
                vals[ci] = jnp.where(hit, -jnp.inf, vals[ci])
            zsum = zsum + jnp.exp(mx - cmax)
        cnt = jnp.zeros((nk, tt), F32)
        for i in range(K):
            n_i = n_lo[i:i + 1, :] if i < 8 else n_hi[i - 8:i - 7, :]
            cnt = jnp.where(r1 == float(i), n_i, cnt)
        c_ref[hh] = cnt
        e1_ref[hh] = jnp.exp(s1 - v1[0]) / zsum
        r2_ref[hh] = r2
        e2_ref[hh] = jnp.exp(s2 - v2[0])
        return carry

    lax.fori_loop(0, PEER_HEADS, head, 0)


def _peersel_call(sc):
    _, nk, t = sc.shape
    tt = PEER_TT
    ospec = pl.BlockSpec((PEER_HEADS, nk, tt), lambda i: (0, 0, i))
    sd = jax.ShapeDtypeStruct((PEER_HEADS, nk, t), F32)
    return pl.pallas_call(
        _peersel_kernel,
        grid=(t // tt,),
        in_specs=[pl.BlockSpec((2 * PEER_HEADS, nk, tt), lambda i: (0, 0, i))],
        out_specs=[ospec] * 4,
        out_shape=[sd] * 4,
        compiler_params=_cparams(("parallel",)),
        name="peersel",
    )(sc)


PEER_TB = 512
PEER_EC = 1024


def _peer_kernel(h2_ref, c_ref, e1_ref, r2_ref, e2_ref, down_ref, upt_ref, x1_ref, mod_ref, fw_ref,
                 o_ref, acc_ref):
    j = pl.program_id(1)
    nk = PEER_NKEYS
    na = PEER_EC // nk

    @pl.when(j == 0)
    def _():
        acc_ref[...] = jnp.zeros(acc_ref.shape, F32)

    act = _gelu_tanh(_nt(down_ref[...], h2_ref[...]))
    pieces = []
    for ai in range(na):
        a = j * na + ai
        w = jnp.zeros((nk, PEER_TB), F32)
        for hh in range(PEER_HEADS):
            cc = c_ref[hh, pl.ds(a, 1), :]
            ee = e1_ref[hh, pl.ds(a, 1), :]
            w = w + jnp.where(r2_ref[hh] < cc, e2_ref[hh], 0.0) * ee
        pieces.append((w * act[ai * nk:(ai + 1) * nk, :]).astype(BF16))
    wa = jnp.concatenate(pieces, axis=0)
    acc_ref[...] += jnp.dot(upt_ref[...], wa, preferred_element_type=F32)

    @pl.when(j == pl.num_programs(1) - 1)
    def _():
        y = jnp.transpose(acc_ref[...])
        x2 = x1_ref[...] + mod_ref[0, 5:6, :] * y
        ms = jnp.mean(x2 * x2, axis=-1, keepdims=True)
        o_ref[...] = x2 * lax.rsqrt(ms + NORM_EPS) * fw_ref[...]


def _peer_call(h2, cnt, e1, r2, e2, down, upt, x1, mod, fw, s):
    t, d = h2.shape
    ne = down.shape[0]
    tb, ec = PEER_TB, PEER_EC
    per_b = s // tb
    dspec = pl.BlockSpec((PEER_HEADS, PEER_NKEYS, tb), lambda i, j: (0, 0, i))
    return pl.pallas_call(
        _peer_kernel,
        grid=(t // tb, ne // ec),
        in_specs=[pl.BlockSpec((tb, d), lambda i, j: (i, 0)), dspec, dspec, dspec, dspec,
                  pl.BlockSpec((ec, d), lambda i, j: (j, 0)),
                  pl.BlockSpec((d, ec), lambda i, j: (0, j)),
                  pl.BlockSpec((tb, d), lambda i, j: (i, 0)),
                  pl.BlockSpec((1, 6, d), lambda i, j: (i // per_b, 0, 0)),
                  pl.BlockSpec((1, d), lambda i, j: (0, 0))],
        out_specs=pl.BlockSpec((tb, d), lambda i, j: (i, 0)),
        out_shape=jax.ShapeDtypeStruct((t, d), F32),
        scratch_shapes=[pltpu.VMEM((d, tb), F32)],
        compiler_params=_cparams(("parallel", "arbitrary")),
        name="peer",
    )(h2, cnt, e1, r2, e2, down, upt, x1, mod, fw)


def _rope_tables(s):
    half = ROPE_DIM // 2
    inv_freq = ROPE_THETA ** (-jnp.arange(half, dtype=F32) / half)
    ang = jnp.arange(s).astype(F32)[:, None] * inv_freq[None, :]
    cos, sin = jnp.cos(ang), jnp.sin(ang)
    one = jnp.ones((s, HEAD_DIM - ROPE_DIM), F32)
    cos64 = jnp.concatenate([cos, cos, one], axis=1)
    sin64 = jnp.concatenate([-sin, sin, 0.0 * one], axis=1)
    return jnp.concatenate([cos64, cos64], axis=1), jnp.concatenate([sin64, sin64], axis=1)


def _swap_cols(w):
    d, n = w.shape
    wh = w.reshape(d, n // HEAD_DIM, HEAD_DIM)
    half = ROPE_DIM // 2
    sw = jnp.concatenate([wh[..., half:ROPE_DIM], wh[..., :half], jnp.zeros_like(wh[..., ROPE_DIM:])], axis=-1)
    return sw.reshape(d, n)


def _pack_w_in(w):
    o = np.cumsum((0, 512, 128, 128, 128, 128, 128, 128, 24, 512, 1024, 8))
    q, kc, vc, ksel, vsel, kwin, vwin, gl, z, xbc, dtr = (w[:, o[i]:o[i + 1]] for i in range(11))
    gl = gl.reshape(-1, NSA_GROUPS, NSA_HPG * N_BRANCH)
    gl = jnp.pad(gl, ((0, 0), (0, 0), (0, 128 - NSA_HPG * N_BRANCH))).reshape(-1, 256)
    dtr = jnp.pad(dtr, ((0, 0), (0, 128 - SSD_HEADS)))
    cols = [q, _swap_cols(q), ksel, _swap_cols(ksel), kwin, _swap_cols(kwin), vsel, vwin, kc, vc, gl, z, xbc, dtr]
    return jnp.concatenate(cols, axis=1).astype(BF16)


def _overlap_t(s):
    n_cmp_pad = s // CMP_STRIDE
    n_sel = s // SEL_BLOCK
    cs = np.arange(n_cmp_pad) * CMP_STRIDE
    ss = np.arange(n_sel) * SEL_BLOCK
    ov = np.maximum(np.minimum(cs[None, :] + CMP_BLOCK, ss[:, None] + SEL_BLOCK)
                    - np.maximum(cs[None, :], ss[:, None]), 0).astype(np.float32) / CMP_BLOCK
    ov[:, n_cmp_pad - 1] = 0.0
    return jnp.asarray(ov, BF16)


def kernel(x, c, w_ada, b_ada, norm1_w, w_in, cmp_pe_k, cmp_pe_v, cmp_w1_k, cmp_w2_k, cmp_w1_v, cmp_w2_v,
           nsa_norm_w, conv_w, conv_b, dt_bias, a_log, d_skip, ssd_norm_w, w_out, norm2_w,
           peer_wq, peer_qnorm_w, peer_sub_keys, peer_down, peer_up, final_norm_w):
    bsz, s, d = x.shape
    assert d == D_MODEL and s % 512 == 0 and w_ada.shape[0] == 1
    lyr = 0
    ts = 512

    mod = _mod_call(c, w_ada[lyr], b_ada[lyr]).reshape(bsz, 6, d)
    cos_t, sin_t = _rope_tables(s)
    (qp, qr, ksel, kwin, vsel, vwin, kcmp, vcmp, gates, z, xbc, dtr) = _inproj_call(
        x, mod, norm1_w[lyr].reshape(1, d), _pack_w_in(w_in[lyr]), cos_t, sin_t, ts)

    nch = s // CMP_STRIDE

    def chunks(t):
        return t.reshape(bsz, nch, CMP_STRIDE, NSA_GROUPS, HEAD_DIM).transpose(0, 3, 1, 2, 4).reshape(
            bsz, NSA_GROUPS, nch, CMP_STRIDE * HEAD_DIM)

    cw = CMP_STRIDE * HEAD_DIM
    kc, vc = _compress_call(
        chunks(kcmp), chunks(vcmp), cmp_pe_k[lyr].reshape(2, cw), cmp_pe_v[lyr].reshape(2, cw),
        cmp_w1_k[lyr].reshape(2, cw, CMP_HIDDEN).astype(BF16), cmp_w2_k[lyr].astype(BF16),
        cmp_w1_v[lyr].reshape(2, cw, CMP_HIDDEN).astype(BF16), cmp_w2_v[lyr].astype(BF16))

    o_nsa = _nsa_call(qp, qr, kc, vc, ksel, vsel, kwin, vwin, gates, _overlap_t(s))

    dtt = jnp.transpose(dtr[:, :, :SSD_HEADS], (0, 2, 1))
    o_ssd = _ssd_call(xbc, z, dtr, dtt, conv_w[lyr], conv_b[lyr], dt_bias[lyr], a_log[lyr], d_skip[lyr],
                      ssd_norm_w[lyr])

    x1, h2, sc = _outproj_call(
        o_nsa, o_ssd, x, mod, nsa_norm_w[lyr].reshape(1, -1), norm2_w[lyr].reshape(1, d),
        w_out[lyr].astype(BF16), peer_wq[lyr].astype(BF16), peer_qnorm_w[lyr].reshape(1, -1),
        peer_sub_keys[lyr].astype(BF16), ts)

    cnt, e1, r2, e2 = _peersel_call(sc)

    out = _peer_call(h2.reshape(bsz * s, d), cnt, e1, r2, e2, peer_down[lyr].astype(BF16),
                     jnp.transpose(peer_up[lyr]).astype(BF16), x1.reshape(bsz * s, d), mod,
                     final_norm_w.reshape(1, d), s)
    return out.reshape(bsz, s, d)
```

```python
import functools
import math

import numpy as np
import jax
import jax.numpy as jnp
from jax import lax
from jax.experimental import pallas as pl
from jax.experimental.pallas import tpu as pltpu

F32 = jnp.float32
BF16 = jnp.bfloat16
HIGHEST = lax.Precision.HIGHEST

D_MODEL = 1024
NSA_WIDTH = 512
SSD_WIDTH = 512
HEAD_DIM = 64
NSA_HEADS = 8
NSA_GROUPS = 2
NSA_HPG = 4
N_BRANCH = 3
CMP_BLOCK = 32
CMP_STRIDE = 16
CMP_HIDDEN = 256
SEL_BLOCK = 64
SEL_TOPN = 16
WINDOW = 512
ROPE_THETA = 500000.0
ROPE_DIM = 16
SSD_HEADS = 8
SSD_GROUPS = 2
SSD_STATE = 128
SSD_CONV = 4
SSD_CHUNK = 128
PEER_HEADS = 8
PEER_NKEYS = 128
PEER_QDIM = 256
PEER_TOPK = 16
NORM_EPS = 1e-6
NEG_INF = -1e30
FORCE_BONUS = 1e4

LANES = 128
VMEM_LIMIT = 56 * 1024 * 1024

C_Q, C_QSW, C_KSEL, C_KSELSW, C_KWIN, C_KWINSW = 0, 512, 1024, 1152, 1280, 1408
C_VSEL, C_VWIN, C_KCMP, C_VCMP, C_GATE, C_Z, C_XBC, C_DT, C_END = 1536, 1664, 1792, 1920, 2048, 2304, 2816, 3840, 3968


def _gelu_tanh(x):
    c = math.sqrt(2.0 / math.pi)
    return 0.5 * x * (1.0 + jnp.tanh(c * (x + 0.044715 * (x * x * x))))


def _nt(a, b):
    return lax.dot_general(a, b, (((1,), (1,)), ((), ())), preferred_element_type=F32)


def _cparams(sem):
    return pltpu.CompilerParams(dimension_semantics=sem, vmem_limit_bytes=VMEM_LIMIT)


def _mod_kernel(c_ref, w_ref, b_ref, o_ref):
    o_ref[...] = jnp.dot(c_ref[...], w_ref[...], preferred_element_type=F32, precision=HIGHEST) + b_ref[...]


def _mod_call(c, w_ada, b_ada):
    bsz = c.shape[0]
    n = w_ada.shape[1]
    return pl.pallas_call(
        _mod_kernel,
        grid=(n // D_MODEL,),
        in_specs=[pl.BlockSpec((bsz, D_MODEL), lambda j: (0, 0)),
                  pl.BlockSpec((D_MODEL, D_MODEL), lambda j: (0, j)),
                  pl.BlockSpec((1, D_MODEL), lambda j: (0, j))],
        out_specs=pl.BlockSpec((bsz, D_MODEL), lambda j: (0, j)),
        out_shape=jax.ShapeDtypeStruct((bsz, n), F32),
        compiler_params=_cparams(("arbitrary",)),
        name="mod",
    )(c, w_ada, b_ada.reshape(1, n))


def _inproj_kernel(x_ref, mod_ref, nw_ref, w_ref, cos_ref, sin_ref,
                   qp_ref, qr_ref, ksel_ref, kwin_ref, vsel_ref, vwin_ref,
                   kcmp_ref, vcmp_ref, gate_ref, z_ref, xbc_ref, dt_ref):
    x = x_ref[0]
    ms = jnp.mean(x * x, axis=-1, keepdims=True)
    y = x * lax.rsqrt(ms + NORM_EPS) * nw_ref[...]
    h = (y * (1.0 + mod_ref[0, 1:2, :]) + mod_ref[0, 0:1, :]).astype(BF16)

    def proj(lo, hi):
        return jnp.dot(h, w_ref[:, lo:hi], preferred_element_type=F32)

    cos = cos_ref[...]
    sin = sin_ref[...]
    scale = HEAD_DIM ** -0.5
    q = proj(C_Q, C_Q + 512)
    qp_ref[0] = (q * scale).astype(BF16)
    qsw = proj(C_QSW, C_QSW + 512)
    cos4 = jnp.concatenate([cos] * 4, axis=1)
    sin4 = jnp.concatenate([sin] * 4, axis=1)
    qr_ref[0] = ((q * cos4 + qsw * sin4) * scale).astype(BF16)

    ks = proj(C_KSEL, C_KSEL + 128) * cos + proj(C_KSELSW, C_KSELSW + 128) * sin
    kw = proj(C_KWIN, C_KWIN + 128) * cos + proj(C_KWINSW, C_KWINSW + 128) * sin
    vs = proj(C_VSEL, C_VSEL + 128)
    vw = proj(C_VWIN, C_VWIN + 128)
    for ref, val in ((ksel_ref, ks), (kwin_ref, kw), (vsel_ref, vs), (vwin_ref, vw)):
        ref[0, 0] = val[:, :HEAD_DIM].astype(BF16)
        ref[0, 1] = val[:, HEAD_DIM:].astype(BF16)
    kcmp_ref[0] = proj(C_KCMP, C_KCMP + 128)
    vcmp_ref[0] = proj(C_VCMP, C_VCMP + 128)
    gl = proj(C_GATE, C_GATE + 256)
    sg = 1.0 / (1.0 + jnp.exp(-gl))
    gate_ref[0, 0] = sg[:, :128]
    gate_ref[0, 1] = sg[:, 128:]
    z_ref[0] = proj(C_Z, C_Z + 512)
    xbc_ref[0] = proj(C_XBC, C_XBC + 1024)
    dt_ref[0] = proj(C_DT, C_DT + 128)


def _inproj_call(x, mod, norm_w, w_ext, cos_t, sin_t, ts):
    bsz, s, d = x.shape
    grid = (bsz, s // ts)
    tok = lambda w: pl.BlockSpec((1, ts, w), lambda b, i: (b, i, 0))
    hm = pl.BlockSpec((1, NSA_GROUPS, ts, HEAD_DIM), lambda b, i: (b, 0, i, 0))
    sd = jax.ShapeDtypeStruct
    out_shape = [sd((bsz, s, 512), BF16), sd((bsz, s, 512), BF16)] + \
                [sd((bsz, NSA_GROUPS, s, HEAD_DIM), BF16)] * 4 + \
                [sd((bsz, s, 128), F32), sd((bsz, s, 128), F32),
                 sd((bsz, NSA_GROUPS, s, 128), F32),
                 sd((bsz, s, 512), F32), sd((bsz, s, 1024), F32), sd((bsz, s, 128), F32)]
    out_specs = [tok(512), tok(512), hm, hm, hm, hm, tok(128), tok(128),
                 pl.BlockSpec((1, NSA_GROUPS, ts, 128), lambda b, i: (b, 0, i, 0)),
                 tok(512), tok(1024), tok(128)]
    return pl.pallas_call(
        _inproj_kernel,
        grid=grid,
        in_specs=[tok(d),
                  pl.BlockSpec((1, 6, d), lambda b, i: (b, 0, 0)),
                  pl.BlockSpec((1, d), lambda b, i: (0, 0)),
                  pl.BlockSpec((d, C_END), lambda b, i: (0, 0)),
                  pl.BlockSpec((ts, 128), lambda b, i: (i, 0)),
                  pl.BlockSpec((ts, 128), lambda b, i: (i, 0))],
        out_specs=out_specs,
        out_shape=out_shape,
        compiler_params=_cparams(("parallel", "parallel")),
        name="inproj",
    )(x, mod, norm_w, w_ext, cos_t, sin_t)


def _compress_kernel(kx_ref, vx_ref, pek_ref, pev_ref, w1k_ref, w2k_ref, w1v_ref, w2v_ref,
                     kc_ref, vc_ref, hb_ref):
    nch = kx_ref.shape[2]

    def one(x_ref, pe_ref, w1_ref, w2_ref, o_ref):
        x = x_ref[0, 0]
        top = jnp.dot((x + pe_ref[0:1, :]).astype(BF16), w1_ref[0], preferred_element_type=F32)
        bot = jnp.dot((x + pe_ref[1:2, :]).astype(BF16), w1_ref[1], preferred_element_type=F32)
        hb_ref[0:nch, :] = bot
        hb_ref[nch:nch + 8, :] = jnp.zeros((8, CMP_HIDDEN), F32)
        pre = top + hb_ref[pl.ds(1, nch), :]
        out = jnp.dot(_gelu_tanh(pre).astype(BF16), w2_ref[...], preferred_element_type=F32)
        row = lax.broadcasted_iota(jnp.int32, out.shape, 0)
        o_ref[0, 0] = jnp.where(row < nch - 1, out, 0.0).astype(BF16)

    one(kx_ref, pek_ref, w1k_ref, w2k_ref, kc_ref)
    one(vx_ref, pev_ref, w1v_ref, w2v_ref, vc_ref)


def _compress_call(kx, vx, pek, pev, w1k, w2k, w1v, w2v):
    bsz, g, nch, cw = kx.shape
    xs = pl.BlockSpec((1, 1, nch, cw), lambda b, gi: (b, gi, 0, 0))
    full = lambda a: pl.BlockSpec(a.shape, lambda b, gi: (0,) * a.ndim)
    os_ = pl.BlockSpec((1, 1, nch, HEAD_DIM), lambda b, gi: (b, gi, 0, 0))
    sd = jax.ShapeDtypeStruct((bsz, g, nch, HEAD_DIM), BF16)
    return pl.pallas_call(
        _compress_kernel,
        grid=(bsz, g),
        in_specs=[xs, xs, full(pek), full(pev), full(w1k), full(w2k), full(w1v), full(w2v)],
        out_specs=[os_, os_],
        out_shape=[sd, sd],
        scratch_shapes=[pltpu.VMEM((nch + 8, CMP_HIDDEN), F32)],
        compiler_params=_cparams(("parallel", "parallel")),
        name="compress",
    )(kx, vx, pek, pev, w1k, w2k, w1v, w2v)


NSA_TQ = 128
NSA_TK = 1024


def _nsa_kernel(qp_ref, qr_ref, kc_ref, vc_ref, ks_ref, vs_ref, kw_ref, vw_ref, gate_ref, ovt_ref, o_ref):
    tq = NSA_TQ
    rows = NSA_HPG * tq
    qt = pl.program_id(2)
    q0 = qt * tq
    ncmp = kc_ref.shape[2]
    nsel = ovt_ref.shape[0]

    qp = jnp.concatenate([qp_ref[0, :, r * HEAD_DIM:(r + 1) * HEAD_DIM] for r in range(NSA_HPG)], axis=0)
    qr = jnp.concatenate([qr_ref[0, :, r * HEAD_DIM:(r + 1) * HEAD_DIM] for r in range(NSA_HPG)], axis=0)

    t_row = q0 + lax.broadcasted_iota(jnp.int32, (tq, ncmp), 0)
    cend = lax.broadcasted_iota(jnp.int32, (tq, ncmp), 1) * CMP_STRIDE + (CMP_BLOCK - 1)
    cbias1 = jnp.where(cend <= t_row, 0.0, NEG_INF)
    cbias = jnp.concatenate([cbias1] * NSA_HPG, axis=0)
    s = _nt(qp, kc_ref[0, 0])
    s = jnp.where(cbias == 0.0, s, NEG_INF)
    m = jnp.max(s, axis=-1, keepdims=True)
    e = jnp.exp(s - m)
    p = jnp.where(cbias == 0.0, e / jnp.sum(e, axis=-1, keepdims=True), 0.0)
    o_c = jnp.dot(p.astype(BF16), vc_ref[0, 0], preferred_element_type=F32)

    psum = p[0:tq] + p[tq:2 * tq] + p[2 * tq:3 * tq] + p[3 * tq:4 * tq]
    p_hi = psum.astype(BF16)
    p_lo = (psum - p_hi.astype(F32)).astype(BF16)
    ovt = ovt_ref[...]
    imp = _nt(ovt, p_hi) + _nt(ovt, p_lo)
    jblk = lax.broadcasted_iota(jnp.int32, (nsel, tq), 0)
    tt = q0 + lax.broadcasted_iota(jnp.int32, (nsel, tq), 1)
    blk_t = jnp.right_shift(tt, 6)
    forced = (jblk == 0) | (jblk == blk_t) | (jblk == blk_t - 1)
    xs = jnp.where(jblk * SEL_BLOCK <= tt, imp + jnp.where(forced, FORCE_BONUS, 0.0), NEG_INF)
    sel_t = jnp.zeros((nsel, tq), F32)
    jblk_f = jblk.astype(F32)
    for _ in range(SEL_TOPN):
        mx = jnp.max(xs, axis=0, keepdims=True)
        idx = jnp.min(jnp.where(xs == mx, jblk_f, float(nsel)), axis=0, keepdims=True)
        hit = jblk_f == idx
        sel_t = jnp.where(hit, 1.0, sel_t)
        xs = jnp.where(hit, -jnp.inf, xs)
    sel = jnp.transpose(sel_t).astype(BF16)

    def online(carry, sc, v):
        m_i, l_i, acc = carry
        m_new = jnp.maximum(m_i, jnp.max(sc, axis=-1, keepdims=True))
        alpha = jnp.exp(m_i - m_new)
        pe = jnp.exp(sc - m_new)
        l_new = alpha * l_i + jnp.sum(pe, axis=-1, keepdims=True)
        acc_new = alpha * acc + jnp.dot(pe.astype(BF16), v, preferred_element_type=F32)
        return m_new, l_new, acc_new

    init = (jnp.full((rows, 1), NEG_INF, F32), jnp.zeros((rows, 1), F32), jnp.zeros((rows, HEAD_DIM), F32))

    tk = NSA_TK
    tq_s = q0 + lax.broadcasted_iota(jnp.int32, (tq, tk), 0)

    def sel_body(kt, carry):
        kbase = pl.multiple_of(kt * tk, tk)
        k = ks_ref[0, 0, pl.ds(kbase, tk), :]
        v = vs_ref[0, 0, pl.ds(kbase, tk), :]
        eb = lax.broadcasted_iota(jnp.int32, (nsel, tk), 0) == \
            jnp.right_shift(kbase + lax.broadcasted_iota(jnp.int32, (nsel, tk), 1), 6)
        mk = jnp.dot(sel, jnp.where(eb, 1.0, 0.0).astype(BF16), preferred_element_type=F32)
        kpos = kbase + lax.broadcasted_iota(jnp.int32, (tq, tk), 1)
        bias1 = jnp.where((mk > 0.5) & (kpos <= tq_s), 0.0, NEG_INF)
        sc = _nt(qr, k) + jnp.concatenate([bias1] * NSA_HPG, axis=0)
        return online(carry, sc, v)

    n_kt = (q0 + tq + tk - 1) // tk
    _, l_s, acc_s = lax.fori_loop(0, n_kt, sel_body, init)
    o_s = acc_s / l_s

    wk = WINDOW + tq
    wstart = pl.multiple_of(jnp.maximum(q0 - WINDOW, 0), tq)
    tq_w = q0 + lax.broadcasted_iota(jnp.int32, (tq, wk), 0)
    kpos_w = wstart + lax.broadcasted_iota(jnp.int32, (tq, wk), 1)
    bias_w = jnp.where((kpos_w <= tq_w) & (tq_w - kpos_w < WINDOW), 0.0, NEG_INF)
    sw = _nt(qr, kw_ref[0, 0, pl.ds(wstart, wk), :]) + jnp.concatenate([bias_w] * NSA_HPG, axis=0)
    ew = jnp.exp(sw - jnp.max(sw, axis=-1, keepdims=True))
    o_w = jnp.dot(ew.astype(BF16), vw_ref[0, 0, pl.ds(wstart, wk), :], preferred_element_type=F32) \
        / jnp.sum(ew, axis=-1, keepdims=True)

    gates = gate_ref[0, 0]
    for r in range(NSA_HPG):
        sl = slice(r * tq, (r + 1) * tq)
        o_r = (gates[:, 3 * r:3 * r + 1] * o_c[sl] + gates[:, 3 * r + 1:3 * r + 2] * o_s[sl]
               + gates[:, 3 * r + 2:3 * r + 3] * o_w[sl])
        o_ref[0, :, r * HEAD_DIM:(r + 1) * HEAD_DIM] = o_r


def _nsa_call(qp, qr, kc, vc, ks, vs, kw, vw, gates, ovt):
    bsz, s, _ = qp.shape
    tq = NSA_TQ
    ncmp = kc.shape[2]
    gw = NSA_HPG * HEAD_DIM
    qspec = pl.BlockSpec((1, tq, gw), lambda b, g, i: (b, i, g))
    cspec = pl.BlockSpec((1, 1, ncmp, HEAD_DIM), lambda b, g, i: (b, g, 0, 0))
    kvspec = pl.BlockSpec((1, 1, s, HEAD_DIM), lambda b, g, i: (b, g, 0, 0))
    return pl.pallas_call(
        _nsa_kernel,
        grid=(bsz, NSA_GROUPS, s // tq),
        in_specs=[qspec, qspec, cspec, cspec, kvspec, kvspec, kvspec, kvspec,
                  pl.BlockSpec((1, 1, tq, 128), lambda b, g, i: (b, g, i, 0)),
                  pl.BlockSpec(ovt.shape, lambda b, g, i: (0, 0))],
        out_specs=pl.BlockSpec((1, tq, gw), lambda b, g, i: (b, i, g)),
        out_shape=jax.ShapeDtypeStruct((bsz, s, NSA_WIDTH), F32),
        compiler_params=_cparams(("parallel", "parallel", "arbitrary")),
        name="nsa",
    )(qp, qr, kc, vc, ks, vs, kw, vw, gates, ovt)


def _softplus(x):
    return jnp.maximum(x, 0.0) + jnp.log1p(jnp.exp(-jnp.abs(x)))


def _ssd_kernel(xbc_ref, z_ref, dt_ref, dtt_ref, cw_ref, cb_ref, dtb_ref, dtbt_ref, al_ref, alt_ref,
                dsk_ref, nw_ref, o_ref, tail_ref, xp_ref, st_ref, y_ref):
    L = SSD_CHUNK
    P = HEAD_DIM
    N = SSD_STATE
    c = pl.program_id(1)

    @pl.when(c == 0)
    def _():
        tail_ref[...] = jnp.zeros(tail_ref.shape, F32)
        st_ref[...] = jnp.zeros(st_ref.shape, F32)

    xin = xbc_ref[0]
    xp_ref[0:8, :] = tail_ref[...]
    xp_ref[8:8 + L, :] = xin
    tail_ref[...] = xin[L - 8:L, :]
    conv = cb_ref[...] + jnp.zeros_like(xin)
    for k in range(SSD_CONV):
        conv = conv + cw_ref[k:k + 1, :] * xp_ref[pl.ds(8 - (SSD_CONV - 1) + k, L), :]
    u = conv * (1.0 / (1.0 + jnp.exp(-conv)))
    xs = u[:, :SSD_WIDTH]

    dt_c = _softplus(dt_ref[0] + dtb_ref[...])
    dt_r = _softplus(dtt_ref[0] + dtbt_ref[...])
    a_c = -jnp.exp(al_ref[...])
    a_r = -jnp.exp(alt_ref[...])
    li = lax.broadcasted_iota(jnp.int32, (L, L), 0)
    si = lax.broadcasted_iota(jnp.int32, (L, L), 1)
    causal = li >= si
    tri = jnp.where(causal, 1.0, 0.0)
    tri_t = jnp.where(li <= si, 1.0, 0.0)
    acs_c = jnp.dot(tri, dt_c * a_c, preferred_element_type=F32, precision=HIGHEST)
    acs_r = jnp.dot(dt_r * a_r, tri_t, preferred_element_type=F32, precision=HIGHEST)

    for g in range(SSD_GROUPS):
        bm = u[:, SSD_WIDTH + g * N:SSD_WIDTH + (g + 1) * N]
        cm = u[:, SSD_WIDTH + SSD_GROUPS * N + g * N:SSD_WIDTH + SSD_GROUPS * N + (g + 1) * N]
        bm_b = bm.astype(BF16)
        cm_b = cm.astype(BF16)
        cb = _nt(cm_b, bm_b)
        bm_t = jnp.transpose(bm)
        for r in range(SSD_HEADS // SSD_GROUPS):
            hh = g * (SSD_HEADS // SSD_GROUPS) + r
            col = acs_c[:, hh:hh + 1]
            row = acs_r[hh:hh + 1, :]
            last = acs_r[hh:hh + 1, L - 1:L]
            lm = jnp.exp(jnp.where(causal, col - row, NEG_INF))
            x_h = xs[:, hh * P:(hh + 1) * P]
            xd = x_h * dt_c[:, hh:hh + 1]
            y_d = jnp.dot((cb * lm).astype(BF16), xd.astype(BF16), preferred_element_type=F32)
            prev = st_ref[hh]
            y_o = jnp.dot(cm_b, prev.astype(BF16), preferred_element_type=F32) * jnp.exp(col)
            dec = jnp.exp(last - row)
            st_new = jnp.dot((bm_t * dec).astype(BF16), xd.astype(BF16), preferred_element_type=F32)
            st_ref[hh] = prev * jnp.exp(last) + st_new
            y_ref[:, hh * P:(hh + 1) * P] = y_d + y_o + x_h * dsk_ref[0:1, hh:hh + 1]

    zz = z_ref[0]
    y = y_ref[...] * (zz * (1.0 / (1.0 + jnp.exp(-zz))))
    ms = jnp.mean(y * y, axis=-1, keepdims=True)
    o_ref[0] = (y * lax.rsqrt(ms + NORM_EPS) * nw_ref[...]).astype(o_ref.dtype)


def _ssd_call(xbc, z, dt, dtt, conv_w, conv_b, dt_bias, a_log, d_skip, norm_w):
    bsz, s, cch = xbc.shape
    L = SSD_CHUNK
    pad = lambda v: jnp.pad(v.reshape(1, -1), ((0, 0), (0, 128 - v.size)))
    full = lambda a: pl.BlockSpec(a.shape, lambda b, i: (0,) * a.ndim)
    args = (xbc, z, dt, dtt, conv_w, conv_b.reshape(1, -1), pad(dt_bias), dt_bias.reshape(-1, 1),
            pad(a_log), a_log.reshape(-1, 1), pad(d_skip), norm_w.reshape(1, -1))
    in_specs = [pl.BlockSpec((1, L, cch), lambda b, i: (b, i, 0)),
                pl.BlockSpec((1, L, SSD_WIDTH), lambda b, i: (b, i, 0)),
                pl.BlockSpec((1, L, 128), lambda b, i: (b, i, 0)),
                pl.BlockSpec((1, SSD_HEADS, L), lambda b, i: (b, 0, i))] + [full(a) for a in args[4:]]
    return pl.pallas_call(
        _ssd_kernel,
        grid=(bsz, s // L),
        in_specs=in_specs,
        out_specs=pl.BlockSpec((1, L, SSD_WIDTH), lambda b, i: (b, i, 0)),
        out_shape=jax.ShapeDtypeStruct((bsz, s, SSD_WIDTH), BF16),
        scratch_shapes=[pltpu.VMEM((8, cch), F32), pltpu.VMEM((8 + L, cch), F32),
                        pltpu.VMEM((SSD_HEADS, SSD_STATE, HEAD_DIM), F32), pltpu.VMEM((L, SSD_WIDTH), F32)],
        compiler_params=_cparams(("parallel", "arbitrary")),
        name="ssd",
    )(*args)


def _outproj_kernel(on_ref, os_ref, x_ref, mod_ref, nnw_ref, n2w_ref, wo_ref, wq_ref, qnw_ref, sk_ref,
                    x1_ref, h2_ref, sc_ref):
    o = on_ref[0]
    ms = jnp.mean(o * o, axis=-1, keepdims=True)
    on = (o * lax.rsqrt(ms + NORM_EPS) * nnw_ref[...]).astype(BF16)
    mix = jnp.dot(on, wo_ref[0:NSA_WIDTH, :], preferred_element_type=F32) + \
        jnp.dot(os_ref[0], wo_ref[NSA_WIDTH:, :], preferred_element_type=F32)
    x1 = x_ref[0] + mod_ref[0, 2:3, :] * mix
    x1_ref[0] = x1
    ms2 = jnp.mean(x1 * x1, axis=-1, keepdims=True)
    h2 = ((x1 * lax.rsqrt(ms2 + NORM_EPS) * n2w_ref[...]) * (1.0 + mod_ref[0, 4:5, :]) + mod_ref[0, 3:4, :]).astype(BF16)
    h2_ref[0] = h2
    half = PEER_QDIM // 2
    for hh in range(PEER_HEADS):
        qh = jnp.dot(h2, wq_ref[:, hh * PEER_QDIM:(hh + 1) * PEER_QDIM], preferred_element_type=F32)
        qn = (qh * lax.rsqrt(jnp.mean(qh * qh, axis=-1, keepdims=True) + NORM_EPS) * qnw_ref[...]).astype(BF16)
        for k in range(2):
            sc_ref[2 * hh + k] = _nt(sk_ref[hh, k], qn[:, k * half:(k + 1) * half])


def _outproj_call(o_nsa, o_ssd, x, mod, nsa_nw, n2w, w_out, wq, qnw, sub_keys, ts):
    bsz, s, d = x.shape
    nblk = s // ts
    tok = lambda w: pl.BlockSpec((1, ts, w), lambda b, i: (b, i, 0))
    full = lambda a: pl.BlockSpec(a.shape, lambda b, i: (0,) * a.ndim)
    return pl.pallas_call(
        _outproj_kernel,
        grid=(bsz, nblk),
        in_specs=[tok(NSA_WIDTH), tok(SSD_WIDTH), tok(d), pl.BlockSpec((1, 6, d), lambda b, i: (b, 0, 0)),
                  full(nsa_nw), full(n2w), full(w_out), full(wq), full(qnw), full(sub_keys)],
        out_specs=[tok(d), tok(d),
                   pl.BlockSpec((2 * PEER_HEADS, PEER_NKEYS, ts), lambda b, i: (0, 0, b * nblk + i))],
        out_shape=[jax.ShapeDtypeStruct((bsz, s, d), F32), jax.ShapeDtypeStruct((bsz, s, d), BF16),
                   jax.ShapeDtypeStruct((2 * PEER_HEADS, PEER_NKEYS, bsz * s), F32)],
        compiler_params=_cparams(("parallel", "parallel")),
        name="outproj",
    )(o_nsa, o_ssd, x, mod, nsa_nw, n2w, w_out, wq, qnw, sub_keys)


PEER_TT = 128


def _peersel_kernel(sc_ref, c_ref, e1_ref, r2_ref, e2_ref):
    nk = PEER_NKEYS
    tt = PEER_TT
    K = PEER_TOPK
    kidx = lax.broadcasted_iota(jnp.int32, (nk, tt), 0).astype(F32)
    i16 = lax.broadcasted_iota(jnp.int32, (K, tt), 0).astype(F32)

    def topk_sorted(x):
        rank = jnp.full((nk, tt), float(K), F32)
        vals = []
        for j in range(K):
            mx = jnp.max(x, axis=0, keepdims=True)
            idx = jnp.min(jnp.where(x == mx, kidx, float(nk)), axis=0, keepdims=True)
            hit = kidx == idx
            rank = jnp.where(hit, float(j), rank)
            x = jnp.where(hit, -jnp.inf, x)
            vals.append(mx)
        return rank, vals

    def head(hh, carry):
        s1 = sc_ref[2 * hh]
        s2 = sc_ref[2 * hh + 1]
        r1, v1 = topk_sorted(s1)
        r2, v2 = topk_sorted(s2)
        v1a = jnp.zeros((K, tt), F32)
        v2a = jnp.zeros((K, tt), F32)
        for i in range(K):
            v1a = jnp.where(i16 == float(i), v1[i], v1a)
            v2a = jnp.where(i16 == float(i), v2[i], v2a)
        cmax = v1[0] + v2[0]
        n = jnp.zeros((K, tt), F32)
        f = v1a + v2[0]
        zsum = jnp.zeros((1, tt), F32)
        for _ in range(K):
            mx = jnp.max(f, axis=0, keepdims=True)
            iw = jnp.min(jnp.where(f == mx, i16, float(K)), axis=0, keepdims=True)
            hit = i16 == iw
            n = n + jnp.where(hit, 1.0, 0.0)
            zsum = zsum + jnp.exp(mx - cmax)
            nstar = jnp.sum(jnp.where(hit, n, 0.0), axis=0, keepdims=True)
            v2n = jnp.sum(jnp.where(i16 == nstar, v2a, 0.0), axis=0, keepdims=True)
            f = jnp.where(hit, v1a + v2n, f)
        cnt = jnp.zeros((nk, tt), F32)
        for i in range(K):
            cnt = jnp.where(r1 == float(i), n[i:i + 1, :], cnt)
        c_ref[hh] = cnt
        e1_ref[hh] = jnp.exp(s1 - v1[0]) / zsum
        r2_ref[hh] = r2.astype(BF16)
        e2_ref[hh] = jnp.exp(s2 - v2[0]).astype(BF16)
        return carry

    lax.fori_loop(0, PEER_HEADS, head, 0)


def _peersel_call(sc):
    _, nk, t = sc.shape
    tt = PEER_TT
    ospec = pl.BlockSpec((PEER_HEADS, nk, tt), lambda i: (0, 0, i))
    sd = lambda dt: jax.ShapeDtypeStruct((PEER_HEADS, nk, t), dt)
    return pl.pallas_call(
        _peersel_kernel,
        grid=(t // tt,),
        in_specs=[pl.BlockSpec((2 * PEER_HEADS, nk, tt), lambda i: (0, 0, i))],
        out_specs=[ospec] * 4,
        out_shape=[sd(F32), sd(F32), sd(BF16), sd(BF16)],
        compiler_params=_cparams(("parallel",)),
        name="peersel",
    )(sc)


PEER_TB = 512
PEER_EC = 1024


def _peer_kernel(h2_ref, c_ref, e1_ref, r2_ref, e2_ref, down_ref, upt_ref, x1_ref, mod_ref, fw_ref,
                 o_ref, acc_ref):
    j = pl.program_id(1)
    nk = PEER_NKEYS
    na = PEER_EC // nk

    @pl.when(j == 0)
    def _():
        acc_ref[...] = jnp.zeros(acc_ref.shape, F32)

    act = _gelu_tanh(_nt(down_ref[...], h2_ref[...]).astype(BF16))

    a0 = pl.multiple_of(j * na, na)

    def rows(ref, hh, ai):
        grp = ref[hh, pl.ds(a0, na), :]
        r16 = jnp.broadcast_to(grp[ai:ai + 1, :], (16, PEER_TB)).astype(BF16)
        return jnp.concatenate([r16] * (nk // 16), axis=0)

    pieces = []
    for ai in range(na):
        w = None
        for hh in range(PEER_HEADS):
            term = jnp.where(r2_ref[hh] < rows(c_ref, hh, ai), e2_ref[hh], 0.0) * rows(e1_ref, hh, ai)
            w = term if w is None else w + term
        pieces.append(w * act[ai * nk:(ai + 1) * nk, :])
    wa = jnp.concatenate(pieces, axis=0)
    acc_ref[...] += jnp.dot(upt_ref[...], wa, preferred_element_type=F32)

    @pl.when(j == pl.num_programs(1) - 1)
    def _():
        y = jnp.transpose(acc_ref[...])
        x2 = x1_ref[...] + mod_ref[0, 5:6, :] * y
        ms = jnp.mean(x2 * x2, axis=-1, keepdims=True)
        o_ref[...] = x2 * lax.rsqrt(ms + NORM_EPS) * fw_ref[...]


def _peer_call(h2, cnt, e1, r2, e2, down, upt, x1, mod, fw, s):
    t, d = h2.shape
    ne = down.shape[0]
    tb, ec = PEER_TB, PEER_EC
    per_b = s // tb
    dspec = pl.BlockSpec((PEER_HEADS, PEER_NKEYS, tb), lambda i, j: (0, 0, i))
    return pl.pallas_call(
        _peer_kernel,
        grid=(t // tb, ne // ec),
        in_specs=[pl.BlockSpec((tb, d), lambda i, j: (i, 0)), dspec, dspec, dspec, dspec,
                  pl.BlockSpec((ec, d), lambda i, j: (j, 0)),
                  pl.BlockSpec((d, ec), lambda i, j: (0, j)),
                  pl.BlockSpec((tb, d), lambda i, j: (i, 0)),
                  pl.BlockSpec((1, 6, d), lambda i, j: (i // per_b, 0, 0)),
                  pl.BlockSpec((1, d), lambda i, j: (0, 0))],
        out_specs=pl.BlockSpec((tb, d), lambda i, j: (i, 0)),
        out_shape=jax.ShapeDtypeStruct((t, d), F32),
        scratch_shapes=[pltpu.VMEM((d, tb), F32)],
        compiler_params=_cparams(("parallel", "arbitrary")),
        name="peer",
    )(h2, cnt, e1, r2, e2, down, upt, x1, mod, fw)


def _rope_tables(s):
    half = ROPE_DIM // 2
    inv_freq = ROPE_THETA ** (-jnp.arange(half, dtype=F32) / half)
    ang = jnp.arange(s).astype(F32)[:, None] * inv_freq[None, :]
    cos, sin = jnp.cos(ang), jnp.sin(ang)
    one = jnp.ones((s, HEAD_DIM - ROPE_DIM), F32)
    cos64 = jnp.concatenate([cos, cos, one], axis=1)
    sin64 = jnp.concatenate([-sin, sin, 0.0 * one], axis=1)
    return jnp.concatenate([cos64, cos64], axis=1), jnp.concatenate([sin64, sin64], axis=1)


def _swap_cols(w):
    d, n = w.shape
    wh = w.reshape(d, n // HEAD_DIM, HEAD_DIM)
    half = ROPE_DIM // 2
    sw = jnp.concatenate([wh[..., half:ROPE_DIM], wh[..., :half], jnp.zeros_like(wh[..., ROPE_DIM:])], axis=-1)
    return sw.reshape(d, n)


def _pack_w_in(w):
    o = np.cumsum((0, 512, 128, 128, 128, 128, 128, 128, 24, 512, 1024, 8))
    q, kc, vc, ksel, vsel, kwin, vwin, gl, z, xbc, dtr = (w[:, o[i]:o[i + 1]] for i in range(11))
    gl = gl.reshape(-1, NSA_GROUPS, NSA_HPG * N_BRANCH)
    gl = jnp.pad(gl, ((0, 0), (0, 0), (0, 128 - NSA_HPG * N_BRANCH))).reshape(-1, 256)
    dtr = jnp.pad(dtr, ((0, 0), (0, 128 - SSD_HEADS)))
    cols = [q, _swap_cols(q), ksel, _swap_cols(ksel), kwin, _swap_cols(kwin), vsel, vwin, kc, vc, gl, z, xbc, dtr]
    return jnp.concatenate(cols, axis=1).astype(BF16)


def _overlap_t(s):
    n_cmp_pad = s // CMP_STRIDE
    n_sel = s // SEL_BLOCK
    cs = np.arange(n_cmp_pad) * CMP_STRIDE
    ss = np.arange(n_sel) * SEL_BLOCK
    ov = np.maximum(np.minimum(cs[None, :] + CMP_BLOCK, ss[:, None] + SEL_BLOCK)
                    - np.maximum(cs[None, :], ss[:, None]), 0).astype(np.float32) / CMP_BLOCK
    ov[:, n_cmp_pad - 1] = 0.0
    return jnp.asarray(ov, BF16)


def kernel(x, c, w_ada, b_ada, norm1_w, w_in, cmp_pe_k, cmp_pe_v, cmp_w1_k, cmp_w2_k, cmp_w1_v, cmp_w2_v,
           nsa_norm_w, conv_w, conv_b, dt_bias, a_log, d_skip, ssd_norm_w, w_out, norm2_w,
           peer_wq, peer_qnorm_w, peer_sub_keys, peer_down, peer_up, final_norm_w):
    bsz, s, d = x.shape
    assert d == D_MODEL and s % NSA_TK == 0 and w_ada.shape[0] == 1
    lyr = 0
    ts = 512

    mod = _mod_call(c, w_ada[lyr], b_ada[lyr]).reshape(bsz, 6, d)
    cos_t, sin_t = _rope_tables(s)
    (qp, qr, ksel, kwin, vsel, vwin, kcmp, vcmp, gates, z, xbc, dtr) = _inproj_call(
        x, mod, norm1_w[lyr].reshape(1, d), _pack_w_in(w_in[lyr]), cos_t, sin_t, ts)

    nch = s // CMP_STRIDE

    def chunks(t):
        return t.reshape(bsz, nch, CMP_STRIDE, NSA_GROUPS, HEAD_DIM).transpose(0, 3, 1, 2, 4).reshape(
            bsz, NSA_GROUPS, nch, CMP_STRIDE * HEAD_DIM)

    cw = CMP_STRIDE * HEAD_DIM
    kc, vc = _compress_call(
        chunks(kcmp), chunks(vcmp), cmp_pe_k[lyr].reshape(2, cw), cmp_pe_v[lyr].reshape(2, cw),
        cmp_w1_k[lyr].reshape(2, cw, CMP_HIDDEN).astype(BF16), cmp_w2_k[lyr].astype(BF16),
        cmp_w1_v[lyr].reshape(2, cw, CMP_HIDDEN).astype(BF16), cmp_w2_v[lyr].astype(BF16))

    o_nsa = _nsa_call(qp, qr, kc, vc, ksel, vsel, kwin, vwin, gates, _overlap_t(s))

    dtt = jnp.transpose(dtr[:, :, :SSD_HEADS], (0, 2, 1))
    o_ssd = _ssd_call(xbc, z, dtr, dtt, conv_w[lyr], conv_b[lyr], dt_bias[lyr], a_log[lyr], d_skip[lyr],
                      ssd_norm_w[lyr])

    x1, h2, sc = _outproj_call(
        o_nsa, o_ssd, x, mod, nsa_norm_w[lyr].reshape(1, -1), norm2_w[lyr].reshape(1, d),
        w_out[lyr].astype(BF16), peer_wq[lyr].astype(BF16), peer_qnorm_w[lyr].reshape(1, -1),
        peer_sub_keys[lyr].astype(BF16), ts)

    cnt, e1, r2, e2 = _peersel_call(sc)

    out = _peer_call(h2.reshape(bsz * s, d), cnt, e1, r2, e2, peer_down[lyr].astype(BF16),
                     jnp.transpose(peer_up[lyr]).astype(BF16), x1.reshape(bsz * s, d), mod,
                     final_norm_w.reshape(1, d), s)
    return out.reshape(bsz, s, d)
```

```python
import functools
import math

import numpy as np
import jax
import jax.numpy as jnp
from jax import lax
from jax.experimental import pallas as pl
from jax.experimental.pallas import tpu as pltpu

F32 = jnp.float32
BF16 = jnp.bfloat16
HIGHEST = lax.Precision.HIGHEST

D_MODEL = 1024
NSA_WIDTH = 512
SSD_WIDTH = 512
HEAD_DIM = 64
NSA_HEADS = 8
NSA_GROUPS = 2
NSA_HPG = 4
N_BRANCH = 3
CMP_BLOCK = 32
CMP_STRIDE = 16
CMP_HIDDEN = 256
SEL_BLOCK = 64
SEL_TOPN = 16
WINDOW = 512
ROPE_THETA = 500000.0
ROPE_DIM = 16
SSD_HEADS = 8
SSD_GROUPS = 2
SSD_STATE = 128
SSD_CONV = 4
SSD_CHUNK = 128
PEER_HEADS = 8
PEER_NKEYS = 128
PEER_QDIM = 256
PEER_TOPK = 16
NORM_EPS = 1e-6
NEG_INF = -1e30
FORCE_BONUS = 1e4

LANES = 128
VMEM_LIMIT = 56 * 1024 * 1024

C_Q, C_QSW, C_KSEL, C_KSELSW, C_KWIN, C_KWINSW = 0, 512, 1024, 1152, 1280, 1408
C_VSEL, C_VWIN, C_KCMP, C_VCMP, C_GATE, C_Z, C_XBC, C_DT, C_END = 1536, 1664, 1792, 1920, 2048, 2304, 2816, 3840, 3968


def _gelu_tanh(x):
    c = math.sqrt(2.0 / math.pi)
    return 0.5 * x * (1.0 + jnp.tanh(c * (x + 0.044715 * (x * x * x))))


def _nt(a, b):
    return lax.dot_general(a, b, (((1,), (1,)), ((), ())), preferred_element_type=F32)


def _cparams(sem):
    return pltpu.CompilerParams(dimension_semantics=sem, vmem_limit_bytes=VMEM_LIMIT)


def _mod_kernel(c_ref, w_ref, b_ref, o_ref):
    o_ref[...] = jnp.dot(c_ref[...], w_ref[...], preferred_element_type=F32, precision=HIGHEST) + b_ref[...]


def _mod_call(c, w_ada, b_ada):
    bsz = c.shape[0]
    n = w_ada.shape[1]
    return pl.pallas_call(
        _mod_kernel,
        grid=(n // D_MODEL,),
        in_specs=[pl.BlockSpec((bsz, D_MODEL), lambda j: (0, 0)),
                  pl.BlockSpec((D_MODEL, D_MODEL), lambda j: (0, j)),
                  pl.BlockSpec((1, D_MODEL), lambda j: (0, j))],
        out_specs=pl.BlockSpec((bsz, D_MODEL), lambda j: (0, j)),
        out_shape=jax.ShapeDtypeStruct((bsz, n), F32),
        compiler_params=_cparams(("arbitrary",)),
        name="mod",
    )(c, w_ada, b_ada.reshape(1, n))


def _inproj_kernel(x_ref, mod_ref, nw_ref, w_ref, cos_ref, sin_ref,
                   qp_ref, qr_ref, ksel_ref, kwin_ref, vsel_ref, vwin_ref,
                   kcmp_ref, vcmp_ref, gate_ref, z_ref, xbc_ref, dt_ref):
    x = x_ref[0]
    ms = jnp.mean(x * x, axis=-1, keepdims=True)
    y = x * lax.rsqrt(ms + NORM_EPS) * nw_ref[...]
    h = (y * (1.0 + mod_ref[0, 1:2, :]) + mod_ref[0, 0:1, :]).astype(BF16)

    def proj(lo, hi):
        return jnp.dot(h, w_ref[:, lo:hi], preferred_element_type=F32)

    cos = cos_ref[...]
    sin = sin_ref[...]
    scale = HEAD_DIM ** -0.5
    q = proj(C_Q, C_Q + 512)
    qp_ref[0] = (q * scale).astype(BF16)
    qsw = proj(C_QSW, C_QSW + 512)
    cos4 = jnp.concatenate([cos] * 4, axis=1)
    sin4 = jnp.concatenate([sin] * 4, axis=1)
    qr_ref[0] = ((q * cos4 + qsw * sin4) * scale).astype(BF16)

    ks = proj(C_KSEL, C_KSEL + 128) * cos + proj(C_KSELSW, C_KSELSW + 128) * sin
    kw = proj(C_KWIN, C_KWIN + 128) * cos + proj(C_KWINSW, C_KWINSW + 128) * sin
    vs = proj(C_VSEL, C_VSEL + 128)
    vw = proj(C_VWIN, C_VWIN + 128)
    for ref, val in ((ksel_ref, ks), (kwin_ref, kw), (vsel_ref, vs), (vwin_ref, vw)):
        ref[0, 0] = val[:, :HEAD_DIM].astype(BF16)
        ref[0, 1] = val[:, HEAD_DIM:].astype(BF16)
    kcmp_ref[0] = proj(C_KCMP, C_KCMP + 128)
    vcmp_ref[0] = proj(C_VCMP, C_VCMP + 128)
    gl = proj(C_GATE, C_GATE + 256)
    sg = 1.0 / (1.0 + jnp.exp(-gl))
    gate_ref[0, 0] = sg[:, :128]
    gate_ref[0, 1] = sg[:, 128:]
    z_ref[0] = proj(C_Z, C_Z + 512)
    xbc_ref[0] = proj(C_XBC, C_XBC + 1024)
    dt_ref[0] = proj(C_DT, C_DT + 128)


def _inproj_call(x, mod, norm_w, w_ext, cos_t, sin_t, ts):
    bsz, s, d = x.shape
    grid = (bsz, s // ts)
    tok = lambda w: pl.BlockSpec((1, ts, w), lambda b, i: (b, i, 0))
    hm = pl.BlockSpec((1, NSA_GROUPS, ts, HEAD_DIM), lambda b, i: (b, 0, i, 0))
    sd = jax.ShapeDtypeStruct
    out_shape = [sd((bsz, s, 512), BF16), sd((bsz, s, 512), BF16)] + \
                [sd((bsz, NSA_GROUPS, s, HEAD_DIM), BF16)] * 4 + \
                [sd((bsz, s, 128), F32), sd((bsz, s, 128), F32),
                 sd((bsz, NSA_GROUPS, s, 128), F32),
                 sd((bsz, s, 512), F32), sd((bsz, s, 1024), F32), sd((bsz, s, 128), F32)]
    out_specs = [tok(512), tok(512), hm, hm, hm, hm, tok(128), tok(128),
                 pl.BlockSpec((1, NSA_GROUPS, ts, 128), lambda b, i: (b, 0, i, 0)),
                 tok(512), tok(1024), tok(128)]
    return pl.pallas_call(
        _inproj_kernel,
        grid=grid,
        in_specs=[tok(d),
                  pl.BlockSpec((1, 6, d), lambda b, i: (b, 0, 0)),
                  pl.BlockSpec((1, d), lambda b, i: (0, 0)),
                  pl.BlockSpec((d, C_END), lambda b, i: (0, 0)),
                  pl.BlockSpec((ts, 128), lambda b, i: (i, 0)),
                  pl.BlockSpec((ts, 128), lambda b, i: (i, 0))],
        out_specs=out_specs,
        out_shape=out_shape,
        compiler_params=_cparams(("parallel", "parallel")),
        name="inproj",
    )(x, mod, norm_w, w_ext, cos_t, sin_t)


def _compress_kernel(kx_ref, vx_ref, pek_ref, pev_ref, w1k_ref, w2k_ref, w1v_ref, w2v_ref,
                     kc_ref, vc_ref, hb_ref):
    nch = kx_ref.shape[2]

    def one(x_ref, pe_ref, w1_ref, w2_ref, o_ref):
        x = x_ref[0, 0]
        top = jnp.dot((x + pe_ref[0:1, :]).astype(BF16), w1_ref[0], preferred_element_type=F32)
        bot = jnp.dot((x + pe_ref[1:2, :]).astype(BF16), w1_ref[1], preferred_element_type=F32)
        hb_ref[0:nch, :] = bot
        hb_ref[nch:nch + 8, :] = jnp.zeros((8, CMP_HIDDEN), F32)
        pre = top + hb_ref[pl.ds(1, nch), :]
        out = jnp.dot(_gelu_tanh(pre).astype(BF16), w2_ref[...], preferred_element_type=F32)
        row = lax.broadcasted_iota(jnp.int32, out.shape, 0)
        o_ref[0, 0] = jnp.where(row < nch - 1, out, 0.0).astype(BF16)

    one(kx_ref, pek_ref, w1k_ref, w2k_ref, kc_ref)
    one(vx_ref, pev_ref, w1v_ref, w2v_ref, vc_ref)


def _compress_call(kx, vx, pek, pev, w1k, w2k, w1v, w2v):
    bsz, g, nch, cw = kx.shape
    xs = pl.BlockSpec((1, 1, nch, cw), lambda b, gi: (b, gi, 0, 0))
    full = lambda a: pl.BlockSpec(a.shape, lambda b, gi: (0,) * a.ndim)
    os_ = pl.BlockSpec((1, 1, nch, HEAD_DIM), lambda b, gi: (b, gi, 0, 0))
    sd = jax.ShapeDtypeStruct((bsz, g, nch, HEAD_DIM), BF16)
    return pl.pallas_call(
        _compress_kernel,
        grid=(bsz, g),
        in_specs=[xs, xs, full(pek), full(pev), full(w1k), full(w2k), full(w1v), full(w2v)],
        out_specs=[os_, os_],
        out_shape=[sd, sd],
        scratch_shapes=[pltpu.VMEM((nch + 8, CMP_HIDDEN), F32)],
        compiler_params=_cparams(("parallel", "parallel")),
        name="compress",
    )(kx, vx, pek, pev, w1k, w2k, w1v, w2v)


NSA_TQ = 128
NSA_TK = 1024


def _nsa_kernel(qp_ref, qr_ref, kc_ref, vc_ref, ks_ref, vs_ref, kw_ref, vw_ref, gate_ref, ovt_ref, exp_ref, o_ref):
    tq = NSA_TQ
    tk = NSA_TK
    rows = NSA_HPG * tq
    qt = pl.program_id(2)
    q0 = qt * tq
    ncmp = kc_ref.shape[2]
    nsel = ovt_ref.shape[0]
    stack = lambda a: jnp.concatenate([a] * NSA_HPG, axis=0)
    qp = jnp.concatenate([qp_ref[0, :, r * HEAD_DIM:(r + 1) * HEAD_DIM] for r in range(NSA_HPG)], axis=0)
    qr = jnp.concatenate([qr_ref[0, :, r * HEAD_DIM:(r + 1) * HEAD_DIM] for r in range(NSA_HPG)], axis=0)

    t_row = q0 + lax.broadcasted_iota(jnp.int32, (tq, ncmp), 0)
    cend = lax.broadcasted_iota(jnp.int32, (tq, ncmp), 1) * CMP_STRIDE + (CMP_BLOCK - 1)
    cbias = stack(jnp.where(cend <= t_row, 0.0, NEG_INF))
    s = jnp.where(cbias == 0.0, _nt(qp, kc_ref[0, 0]), NEG_INF)
    e = jnp.exp(s - jnp.max(s, axis=-1, keepdims=True))
    p = jnp.where(cbias == 0.0, e / jnp.sum(e, axis=-1, keepdims=True), 0.0)
    o_c = jnp.dot(p.astype(BF16), vc_ref[0, 0], preferred_element_type=F32)

    psum = p[0:tq] + p[tq:2 * tq] + p[2 * tq:3 * tq] + p[3 * tq:4 * tq]
    p_hi = psum.astype(BF16)
    p_lo = (psum - p_hi.astype(F32)).astype(BF16)
    ovt = ovt_ref[...]
    imp = _nt(ovt, p_hi) + _nt(ovt, p_lo)
    jblk = lax.broadcasted_iota(jnp.int32, (nsel, tq), 0)
    tt = q0 + lax.broadcasted_iota(jnp.int32, (nsel, tq), 1)
    blk_t = jnp.right_shift(tt, 6)
    forced = (jblk == 0) | (jblk == blk_t) | (jblk == blk_t - 1)
    xs = jnp.where(jblk * SEL_BLOCK <= tt, imp + jnp.where(forced, FORCE_BONUS, 0.0), NEG_INF)
    drop_t = jnp.full((nsel, tq), NEG_INF, F32)
    jblk_f = jblk.astype(F32)
    for _ in range(SEL_TOPN):
        mx = jnp.max(xs, axis=0, keepdims=True)
        idx = jnp.min(jnp.where(xs == mx, jblk_f, float(nsel)), axis=0, keepdims=True)
        hit = jblk_f == idx
        drop_t = jnp.where(hit, 0.0, drop_t)
        xs = jnp.where(hit, -jnp.inf, xs)
    drop = jnp.transpose(drop_t).astype(BF16)

    wk = WINDOW + tq
    wstart = pl.multiple_of(jnp.maximum(q0 - WINDOW, 0), tq)
    tq_w = q0 + lax.broadcasted_iota(jnp.int32, (tq, wk), 0)
    kpos_w = wstart + lax.broadcasted_iota(jnp.int32, (tq, wk), 1)
    bias_w = jnp.where((kpos_w <= tq_w) & (tq_w - kpos_w < WINDOW), 0.0, NEG_INF)
    sw = _nt(qr, kw_ref[0, 0, pl.ds(wstart, wk), :]) + stack(bias_w)
    ew = jnp.exp(sw - jnp.max(sw, axis=-1, keepdims=True))
    o_w = jnp.dot(ew.astype(BF16), vw_ref[0, 0, pl.ds(wstart, wk), :], preferred_element_type=F32) \
        / jnp.sum(ew, axis=-1, keepdims=True)

    tq_s = q0 + lax.broadcasted_iota(jnp.int32, (tq, tk), 0)

    def sel_body(kt, carry):
        m_i, l_i, acc = carry
        kbase = pl.multiple_of(kt * tk, tk)
        bias = jnp.dot(drop, exp_ref[kt], preferred_element_type=F32)
        kpos = kbase + lax.broadcasted_iota(jnp.int32, (tq, tk), 1)
        sc = _nt(qr, ks_ref[0, 0, pl.ds(kbase, tk), :]) + stack(jnp.where(kpos <= tq_s, bias, NEG_INF))
        m_new = jnp.maximum(m_i, jnp.max(sc, axis=-1, keepdims=True))
        alpha = jnp.exp(m_i - m_new)
        pe = jnp.exp(sc - m_new)
        l_new = alpha * l_i + jnp.sum(pe, axis=-1, keepdims=True)
        acc_new = alpha * acc + jnp.dot(pe.astype(BF16), vs_ref[0, 0, pl.ds(kbase, tk), :],
                                        preferred_element_type=F32)
        return m_new, l_new, acc_new

    init = (jnp.full((rows, 1), NEG_INF, F32), jnp.zeros((rows, 1), F32), jnp.zeros((rows, HEAD_DIM), F32))
    n_kt = (q0 + tq + tk - 1) // tk
    _, l_s, acc_s = lax.fori_loop(0, n_kt, sel_body, init)
    o_s = acc_s / l_s

    gates = gate_ref[0, 0]
    for r in range(NSA_HPG):
        sl = slice(r * tq, (r + 1) * tq)
        o_r = (gates[:, 3 * r:3 * r + 1] * o_c[sl] + gates[:, 3 * r + 1:3 * r + 2] * o_s[sl]
               + gates[:, 3 * r + 2:3 * r + 3] * o_w[sl])
        o_ref[0, :, r * HEAD_DIM:(r + 1) * HEAD_DIM] = o_r


def _nsa_call(qp, qr, kc, vc, ks, vs, kw, vw, gates, ovt, expand):
    bsz, s, _ = qp.shape
    tq = NSA_TQ
    ncmp = kc.shape[2]
    gw = NSA_HPG * HEAD_DIM
    qspec = pl.BlockSpec((1, tq, gw), lambda b, g, i: (b, i, g))
    cspec = pl.BlockSpec((1, 1, ncmp, HEAD_DIM), lambda b, g, i: (b, g, 0, 0))
    kvspec = pl.BlockSpec((1, 1, s, HEAD_DIM), lambda b, g, i: (b, g, 0, 0))
    return pl.pallas_call(
        _nsa_kernel,
        grid=(bsz, NSA_GROUPS, s // tq),
        in_specs=[qspec, qspec, cspec, cspec, kvspec, kvspec, kvspec, kvspec,
                  pl.BlockSpec((1, 1, tq, 128), lambda b, g, i: (b, g, i, 0)),
                  pl.BlockSpec(ovt.shape, lambda b, g, i: (0, 0)),
                  pl.BlockSpec(expand.shape, lambda b, g, i: (0, 0, 0))],
        out_specs=pl.BlockSpec((1, tq, gw), lambda b, g, i: (b, i, g)),
        out_shape=jax.ShapeDtypeStruct((bsz, s, NSA_WIDTH), F32),
        compiler_params=_cparams(("parallel", "parallel", "arbitrary")),
        name="nsa",
    )(qp, qr, kc, vc, ks, vs, kw, vw, gates, ovt, expand)


def _softplus(x):
    return jnp.maximum(x, 0.0) + jnp.log1p(jnp.exp(-jnp.abs(x)))


def _ssd_kernel(xbc_ref, z_ref, dt_ref, dtt_ref, cw_ref, cb_ref, dtb_ref, dtbt_ref, al_ref, alt_ref,
                dsk_ref, nw_ref, o_ref, tail_ref, xp_ref, st_ref, y_ref):
    L = SSD_CHUNK
    P = HEAD_DIM
    N = SSD_STATE
    c = pl.program_id(1)

    @pl.when(c == 0)
    def _():
        tail_ref[...] = jnp.zeros(tail_ref.shape, F32)
        st_ref[...] = jnp.zeros(st_ref.shape, F32)

    xin = xbc_ref[0]
    xp_ref[0:8, :] = tail_ref[...]
    xp_ref[8:8 + L, :] = xin
    tail_ref[...] = xin[L - 8:L, :]
    conv = cb_ref[...] + jnp.zeros_like(xin)
    for k in range(SSD_CONV):
        conv = conv + cw_ref[k:k + 1, :] * xp_ref[pl.ds(8 - (SSD_CONV - 1) + k, L), :]
    u = conv * (1.0 / (1.0 + jnp.exp(-conv)))
    xs = u[:, :SSD_WIDTH]

    dt_c = _softplus(dt_ref[0] + dtb_ref[...])
    dt_r = _softplus(dtt_ref[0] + dtbt_ref[...])
    a_c = -jnp.exp(al_ref[...])
    a_r = -jnp.exp(alt_ref[...])
    li = lax.broadcasted_iota(jnp.int32, (L, L), 0)
    si = lax.broadcasted_iota(jnp.int32, (L, L), 1)
    causal = li >= si
    tri = jnp.where(causal, 1.0, 0.0)
    tri_t = jnp.where(li <= si, 1.0, 0.0)
    acs_c = jnp.dot(tri, dt_c * a_c, preferred_element_type=F32, precision=HIGHEST)
    acs_r = jnp.dot(dt_r * a_r, tri_t, preferred_element_type=F32, precision=HIGHEST)

    for g in range(SSD_GROUPS):
        bm = u[:, SSD_WIDTH + g * N:SSD_WIDTH + (g + 1) * N]
        cm = u[:, SSD_WIDTH + SSD_GROUPS * N + g * N:SSD_WIDTH + SSD_GROUPS * N + (g + 1) * N]
        bm_b = bm.astype(BF16)
        cm_b = cm.astype(BF16)
        cb = _nt(cm_b, bm_b)
        bm_t = jnp.transpose(bm)
        for r in range(SSD_HEADS // SSD_GROUPS):
            hh = g * (SSD_HEADS // SSD_GROUPS) + r
            col = acs_c[:, hh:hh + 1]
            row = acs_r[hh:hh + 1, :]
            last = acs_r[hh:hh + 1, L - 1:L]
            lm = jnp.exp(jnp.where(causal, col - row, NEG_INF))
            x_h = xs[:, hh * P:(hh + 1) * P]
            xd = x_h * dt_c[:, hh:hh + 1]
            y_d = jnp.dot((cb * lm).astype(BF16), xd.astype(BF16), preferred_element_type=F32)
            prev = st_ref[hh]
            y_o = jnp.dot(cm_b, prev.astype(BF16), preferred_element_type=F32) * jnp.exp(col)
            dec = jnp.exp(last - row)
            st_new = jnp.dot((bm_t * dec).astype(BF16), xd.astype(BF16), preferred_element_type=F32)
            st_ref[hh] = prev * jnp.exp(last) + st_new
            y_ref[:, hh * P:(hh + 1) * P] = y_d + y_o + x_h * dsk_ref[0:1, hh:hh + 1]

    zz = z_ref[0]
    y = y_ref[...] * (zz * (1.0 / (1.0 + jnp.exp(-zz))))
    ms = jnp.mean(y * y, axis=-1, keepdims=True)
    o_ref[0] = (y * lax.rsqrt(ms + NORM_EPS) * nw_ref[...]).astype(o_ref.dtype)


def _ssd_call(xbc, z, dt, dtt, conv_w, conv_b, dt_bias, a_log, d_skip, norm_w):
    bsz, s, cch = xbc.shape
    L = SSD_CHUNK
    pad = lambda v: jnp.pad(v.reshape(1, -1), ((0, 0), (0, 128 - v.size)))
    full = lambda a: pl.BlockSpec(a.shape, lambda b, i: (0,) * a.ndim)
    args = (xbc, z, dt, dtt, conv_w, conv_b.reshape(1, -1), pad(dt_bias), dt_bias.reshape(-1, 1),
            pad(a_log), a_log.reshape(-1, 1), pad(d_skip), norm_w.reshape(1, -1))
    in_specs = [pl.BlockSpec((1, L, cch), lambda b, i: (b, i, 0)),
                pl.BlockSpec((1, L, SSD_WIDTH), lambda b, i: (b, i, 0)),
                pl.BlockSpec((1, L, 128), lambda b, i: (b, i, 0)),
                pl.BlockSpec((1, SSD_HEADS, L), lambda b, i: (b, 0, i))] + [full(a) for a in args[4:]]
    return pl.pallas_call(
        _ssd_kernel,
        grid=(bsz, s // L),
        in_specs=in_specs,
        out_specs=pl.BlockSpec((1, L, SSD_WIDTH), lambda b, i: (b, i, 0)),
        out_shape=jax.ShapeDtypeStruct((bsz, s, SSD_WIDTH), BF16),
        scratch_shapes=[pltpu.VMEM((8, cch), F32), pltpu.VMEM((8 + L, cch), F32),
                        pltpu.VMEM((SSD_HEADS, SSD_STATE, HEAD_DIM), F32), pltpu.VMEM((L, SSD_WIDTH), F32)],
        compiler_params=_cparams(("parallel", "arbitrary")),
        name="ssd",
    )(*args)


def _outproj_kernel(on_ref, os_ref, x_ref, mod_ref, nnw_ref, n2w_ref, wo_ref, wq_ref, qnw_ref, sk_ref,
                    x1_ref, h2_ref, sc_ref):
    o = on_ref[0]
    ms = jnp.mean(o * o, axis=-1, keepdims=True)
    on = (o * lax.rsqrt(ms + NORM_EPS) * nnw_ref[...]).astype(BF16)
    mix = jnp.dot(on, wo_ref[0:NSA_WIDTH, :], preferred_element_type=F32) + \
        jnp.dot(os_ref[0], wo_ref[NSA_WIDTH:, :], preferred_element_type=F32)
    x1 = x_ref[0] + mod_ref[0, 2:3, :] * mix
    x1_ref[0] = x1
    ms2 = jnp.mean(x1 * x1, axis=-1, keepdims=True)
    h2 = ((x1 * lax.rsqrt(ms2 + NORM_EPS) * n2w_ref[...]) * (1.0 + mod_ref[0, 4:5, :]) + mod_ref[0, 3:4, :]).astype(BF16)
    h2_ref[0] = h2
    half = PEER_QDIM // 2
    for hh in range(PEER_HEADS):
        qh = jnp.dot(h2, wq_ref[:, hh * PEER_QDIM:(hh + 1) * PEER_QDIM], preferred_element_type=F32)
        qn = (qh * lax.rsqrt(jnp.mean(qh * qh, axis=-1, keepdims=True) + NORM_EPS) * qnw_ref[...]).astype(BF16)
        for k in range(2):
            sc_ref[2 * hh + k] = _nt(sk_ref[hh, k], qn[:, k * half:(k + 1) * half])


def _outproj_call(o_nsa, o_ssd, x, mod, nsa_nw, n2w, w_out, wq, qnw, sub_keys, ts):
    bsz, s, d = x.shape
    nblk = s // ts
    tok = lambda w: pl.BlockSpec((1, ts, w), lambda b, i: (b, i, 0))
    full = lambda a: pl.BlockSpec(a.shape, lambda b, i: (0,) * a.ndim)
    return pl.pallas_call(
        _outproj_kernel,
        grid=(bsz, nblk),
        in_specs=[tok(NSA_WIDTH), tok(SSD_WIDTH), tok(d), pl.BlockSpec((1, 6, d), lambda b, i: (b, 0, 0)),
                  full(nsa_nw), full(n2w), full(w_out), full(wq), full(qnw), full(sub_keys)],
        out_specs=[tok(d), tok(d),
                   pl.BlockSpec((2 * PEER_HEADS, PEER_NKEYS, ts), lambda b, i: (0, 0, b * nblk + i))],
        out_shape=[jax.ShapeDtypeStruct((bsz, s, d), F32), jax.ShapeDtypeStruct((bsz, s, d), BF16),
                   jax.ShapeDtypeStruct((2 * PEER_HEADS, PEER_NKEYS, bsz * s), F32)],
        compiler_params=_cparams(("parallel", "parallel")),
        name="outproj",
    )(o_nsa, o_ssd, x, mod, nsa_nw, n2w, w_out, wq, qnw, sub_keys)


PEER_TT = 128
PEER_HG = 2


def _peersel_kernel(sc_ref, c_ref, e1_ref, r2_ref, e2_ref):
    nk = PEER_NKEYS
    tt = PEER_TT
    K = PEER_TOPK
    kidx = lax.broadcasted_iota(jnp.int32, (nk, tt), 0).astype(F32)
    i16 = lax.broadcasted_iota(jnp.int32, (K, tt), 0).astype(F32)

    def topk_sorted(x, exact_ties):
        rank = jnp.full((nk, tt), float(K), F32)
        vals = jnp.zeros((K, tt), F32)
        for j in range(K):
            mx = jnp.max(x, axis=0, keepdims=True)
            hit = x == mx
            if exact_ties:
                hit = kidx == jnp.min(jnp.where(hit, kidx, float(nk)), axis=0, keepdims=True)
            rank = jnp.where(hit, float(j), rank)
            x = jnp.where(hit, -jnp.inf, x)
            vals = jnp.where(i16 == float(j), mx, vals)
        taken = jnp.sum(jnp.where(x == -jnp.inf, 1.0, 0.0), axis=0, keepdims=True)
        return rank, vals, taken

    def first_stage(hh0, nh):
        ss = [sc_ref[2 * hh0 + i] for i in range(2 * nh)]
        fast = [topk_sorted(s, False) for s in ss]
        most = fast[0][2]
        for f in fast[1:]:
            most = jnp.maximum(most, f[2])
        res = lax.cond(
            jnp.max(most) > float(K),
            lambda: tuple(a for s in ss for a in topk_sorted(s, True)[:2]),
            lambda: tuple(a for f in fast for a in f[:2]))
        return [(ss[2 * i], ss[2 * i + 1]) + tuple(res[4 * i:4 * i + 4]) for i in range(nh)]

    def second_stage(hh, s1, s2, r1, v1a, r2, v2a):
        v1 = [v1a[i:i + 1, :] for i in range(K)]
        v2 = [v2a[i:i + 1, :] for i in range(K)]
        cmax = v1[0] + v2[0]
        n = jnp.zeros((K, tt), F32)
        f = v1a + v2[0]
        zsum = jnp.zeros((1, tt), F32)
        for _ in range(K):
            mx = jnp.max(f, axis=0, keepdims=True)
            iw = jnp.min(jnp.where(f == mx, i16, float(K)), axis=0, keepdims=True)
            hit = i16 == iw
            n = n + jnp.where(hit, 1.0, 0.0)
            zsum = zsum + jnp.exp(mx - cmax)
            nstar = jnp.sum(jnp.where(hit, n, 0.0), axis=0, keepdims=True)
            v2n = jnp.sum(jnp.where(i16 == nstar, v2a, 0.0), axis=0, keepdims=True)
            f = jnp.where(hit, v1a + v2n, f)
        cnt = jnp.zeros((nk, tt), F32)
        for i in range(K):
            cnt = jnp.where(r1 == float(i), n[i:i + 1, :], cnt)
        c_ref[hh] = cnt
        e1_ref[hh] = jnp.exp(s1 - v1[0]) / zsum
        r2_ref[hh] = r2.astype(BF16)
        e2_ref[hh] = jnp.exp(s2 - v2[0]).astype(BF16)

    for hh0 in range(0, PEER_HEADS, PEER_HG):
        for i, args in enumerate(first_stage(hh0, PEER_HG)):
            second_stage(hh0 + i, *args)


def _peersel_call(sc):
    _, nk, t = sc.shape
    tt = PEER_TT
    ospec = pl.BlockSpec((PEER_HEADS, nk, tt), lambda i: (0, 0, i))
    sd = lambda dt: jax.ShapeDtypeStruct((PEER_HEADS, nk, t), dt)
    return pl.pallas_call(
        _peersel_kernel,
        grid=(t // tt,),
        in_specs=[pl.BlockSpec((2 * PEER_HEADS, nk, tt), lambda i: (0, 0, i))],
        out_specs=[ospec] * 4,
        out_shape=[sd(F32), sd(F32), sd(BF16), sd(BF16)],
        compiler_params=_cparams(("parallel",)),
        name="peersel",
    )(sc)


PEER_TB = 512
PEER_EC = 1024


def _peer_kernel(h2_ref, c_ref, e1_ref, r2_ref, e2_ref, down_ref, upt_ref, x1_ref, mod_ref, fw_ref,
                 o_ref, acc_ref):
    j = pl.program_id(1)
    nk = PEER_NKEYS
    na = PEER_EC // nk

    @pl.when(j == 0)
    def _():
        acc_ref[...] = jnp.zeros(acc_ref.shape, F32)

    a0 = pl.multiple_of(j * na, na)

    def rows(ref, hh, ai):
        grp = ref[hh, pl.ds(a0, na), :]
        r16 = jnp.broadcast_to(grp[ai:ai + 1, :], (16, PEER_TB)).astype(BF16)
        return jnp.concatenate([r16] * (nk // 16), axis=0)

    act = _gelu_tanh(_nt(down_ref[...], h2_ref[...]).astype(BF16))
    pieces = []
    for ai in range(na):
        w = None
        for hh in range(PEER_HEADS):
            term = jnp.where(r2_ref[hh] < rows(c_ref, hh, ai), e2_ref[hh], 0.0) * rows(e1_ref, hh, ai)
            w = term if w is None else w + term
        pieces.append(w * act[ai * nk:(ai + 1) * nk, :])
    wa = jnp.concatenate(pieces, axis=0)
    acc_ref[...] += jnp.dot(upt_ref[...], wa, preferred_element_type=F32)

    @pl.when(j == pl.num_programs(1) - 1)
    def _():
        y = jnp.transpose(acc_ref[...])
        x2 = x1_ref[...] + mod_ref[0, 5:6, :] * y
        ms = jnp.mean(x2 * x2, axis=-1, keepdims=True)
        o_ref[...] = x2 * lax.rsqrt(ms + NORM_EPS) * fw_ref[...]


def _peer_call(h2, cnt, e1, r2, e2, down, upt, x1, mod, fw, s):
    t, d = h2.shape
    ne = down.shape[0]
    tb, ec = PEER_TB, PEER_EC
    per_b = s // tb
    dspec = pl.BlockSpec((PEER_HEADS, PEER_NKEYS, tb), lambda i, j: (0, 0, i))
    return pl.pallas_call(
        _peer_kernel,
        grid=(t // tb, ne // ec),
        in_specs=[pl.BlockSpec((tb, d), lambda i, j: (i, 0)), dspec, dspec, dspec, dspec,
                  pl.BlockSpec((ec, d), lambda i, j: (j, 0)),
                  pl.BlockSpec((d, ec), lambda i, j: (0, j)),
                  pl.BlockSpec((tb, d), lambda i, j: (i, 0)),
                  pl.BlockSpec((1, 6, d), lambda i, j: (i // per_b, 0, 0)),
                  pl.BlockSpec((1, d), lambda i, j: (0, 0))],
        out_specs=pl.BlockSpec((tb, d), lambda i, j: (i, 0)),
        out_shape=jax.ShapeDtypeStruct((t, d), F32),
        scratch_shapes=[pltpu.VMEM((d, tb), F32)],
        compiler_params=_cparams(("parallel", "arbitrary")),
        name="peer",
    )(h2, cnt, e1, r2, e2, down, upt, x1, mod, fw)


def _rope_tables(s):
    half = ROPE_DIM // 2
    inv_freq = ROPE_THETA ** (-jnp.arange(half, dtype=F32) / half)
    ang = jnp.arange(s).astype(F32)[:, None] * inv_freq[None, :]
    cos, sin = jnp.cos(ang), jnp.sin(ang)
    one = jnp.ones((s, HEAD_DIM - ROPE_DIM), F32)
    cos64 = jnp.concatenate([cos, cos, one], axis=1)
    sin64 = jnp.concatenate([-sin, sin, 0.0 * one], axis=1)
    return jnp.concatenate([cos64, cos64], axis=1), jnp.concatenate([sin64, sin64], axis=1)


def _swap_cols(w):
    d, n = w.shape
    wh = w.reshape(d, n // HEAD_DIM, HEAD_DIM)
    half = ROPE_DIM // 2
    sw = jnp.concatenate([wh[..., half:ROPE_DIM], wh[..., :half], jnp.zeros_like(wh[..., ROPE_DIM:])], axis=-1)
    return sw.reshape(d, n)


def _pack_w_in(w):
    o = np.cumsum((0, 512, 128, 128, 128, 128, 128, 128, 24, 512, 1024, 8))
    q, kc, vc, ksel, vsel, kwin, vwin, gl, z, xbc, dtr = (w[:, o[i]:o[i + 1]] for i in range(11))
    gl = gl.reshape(-1, NSA_GROUPS, NSA_HPG * N_BRANCH)
    gl = jnp.pad(gl, ((0, 0), (0, 0), (0, 128 - NSA_HPG * N_BRANCH))).reshape(-1, 256)
    dtr = jnp.pad(dtr, ((0, 0), (0, 128 - SSD_HEADS)))
    cols = [q, _swap_cols(q), ksel, _swap_cols(ksel), kwin, _swap_cols(kwin), vsel, vwin, kc, vc, gl, z, xbc, dtr]
    return jnp.concatenate(cols, axis=1).astype(BF16)


def _overlap_t(s):
    n_cmp_pad = s // CMP_STRIDE
    n_sel = s // SEL_BLOCK
    cs = np.arange(n_cmp_pad) * CMP_STRIDE
    ss = np.arange(n_sel) * SEL_BLOCK
    ov = np.maximum(np.minimum(cs[None, :] + CMP_BLOCK, ss[:, None] + SEL_BLOCK)
                    - np.maximum(cs[None, :], ss[:, None]), 0).astype(np.float32) / CMP_BLOCK
    ov[:, n_cmp_pad - 1] = 0.0
    return jnp.asarray(ov, BF16)


def _block_expand(s):
    kpos = np.arange(s).reshape(s // NSA_TK, 1, NSA_TK)
    blk = np.arange(s // SEL_BLOCK).reshape(1, -1, 1)
    return jnp.asarray(kpos // SEL_BLOCK == blk, BF16)


def kernel(x, c, w_ada, b_ada, norm1_w, w_in, cmp_pe_k, cmp_pe_v, cmp_w1_k, cmp_w2_k, cmp_w1_v, cmp_w2_v,
           nsa_norm_w, conv_w, conv_b, dt_bias, a_log, d_skip, ssd_norm_w, w_out, norm2_w,
           peer_wq, peer_qnorm_w, peer_sub_keys, peer_down, peer_up, final_norm_w):
    bsz, s, d = x.shape
    assert d == D_MODEL and s % NSA_TK == 0 and w_ada.shape[0] == 1
    lyr = 0
    ts = 512

    mod = _mod_call(c, w_ada[lyr], b_ada[lyr]).reshape(bsz, 6, d)
    cos_t, sin_t = _rope_tables(s)
    (qp, qr, ksel, kwin, vsel, vwin, kcmp, vcmp, gates, z, xbc, dtr) = _inproj_call(
        x, mod, norm1_w[lyr].reshape(1, d), _pack_w_in(w_in[lyr]), cos_t, sin_t, ts)

    nch = s // CMP_STRIDE

    def chunks(t):
        return t.reshape(bsz, nch, CMP_STRIDE, NSA_GROUPS, HEAD_DIM).transpose(0, 3, 1, 2, 4).reshape(
            bsz, NSA_GROUPS, nch, CMP_STRIDE * HEAD_DIM)

    cw = CMP_STRIDE * HEAD_DIM
    kc, vc = _compress_call(
        chunks(kcmp), chunks(vcmp), cmp_pe_k[lyr].reshape(2, cw), cmp_pe_v[lyr].reshape(2, cw),
        cmp_w1_k[lyr].reshape(2, cw, CMP_HIDDEN).astype(BF16), cmp_w2_k[lyr].astype(BF16),
        cmp_w1_v[lyr].reshape(2, cw, CMP_HIDDEN).astype(BF16), cmp_w2_v[lyr].astype(BF16))

    o_nsa = _nsa_call(qp, qr, kc, vc, ksel, vsel, kwin, vwin, gates, _overlap_t(s), _block_expand(s))

    dtt = jnp.transpose(dtr[:, :, :SSD_HEADS], (0, 2, 1))
    o_ssd = _ssd_call(xbc, z, dtr, dtt, conv_w[lyr], conv_b[lyr], dt_bias[lyr], a_log[lyr], d_skip[lyr],
                      ssd_norm_w[lyr])

    x1, h2, sc = _outproj_call(
        o_nsa, o_ssd, x, mod, nsa_norm_w[lyr].reshape(1, -1), norm2_w[lyr].reshape(1, d),
        w_out[lyr].astype(BF16), peer_wq[lyr].astype(BF16), peer_qnorm_w[lyr].reshape(1, -1),
        peer_sub_keys[lyr].astype(BF16), ts)

    cnt, e1, r2, e2 = _peersel_call(sc)

    out = _peer_call(h2.reshape(bsz * s, d), cnt, e1, r2, e2, peer_down[lyr].astype(BF16),
                     jnp.transpose(peer_up[lyr]).astype(BF16), x1.reshape(bsz * s, d), mod,
                     final_norm_w.reshape(1, d), s)
    return out.reshape(bsz, s, d)
```

```python
import functools
import math

import numpy as np
import jax
import jax.numpy as jnp
from jax import lax
from jax.experimental import pallas as pl
from jax.experimental.pallas import tpu as pltpu

F32 = jnp.float32
BF16 = jnp.bfloat16
HIGHEST = lax.Precision.HIGHEST

D_MODEL = 1024
NSA_WIDTH = 512
SSD_WIDTH = 512
HEAD_DIM = 64
NSA_HEADS = 8
NSA_GROUPS = 2
NSA_HPG = 4
N_BRANCH = 3
CMP_BLOCK = 32
CMP_STRIDE = 16
CMP_HIDDEN = 256
SEL_BLOCK = 64
SEL_TOPN = 16
WINDOW = 512
ROPE_THETA = 500000.0
ROPE_DIM = 16
SSD_HEADS = 8
SSD_GROUPS = 2
SSD_STATE = 128
SSD_CONV = 4
SSD_CHUNK = 128
PEER_HEADS = 8
PEER_NKEYS = 128
PEER_QDIM = 256
PEER_TOPK = 16
NORM_EPS = 1e-6
NEG_INF = -1e30
FORCE_BONUS = 1e4

LANES = 128
VMEM_LIMIT = 56 * 1024 * 1024

C_Q, C_QSW, C_KSEL, C_KSELSW, C_KWIN, C_KWINSW = 0, 512, 1024, 1152, 1280, 1408
C_VSEL, C_VWIN, C_KCMP, C_VCMP, C_GATE, C_Z, C_XBC, C_DT, C_END = 1536, 1664, 1792, 1920, 2048, 2304, 2816, 3840, 3968


def _gelu_tanh(x):
    c = math.sqrt(2.0 / math.pi)
    return 0.5 * x * (1.0 + jnp.tanh(c * (x + 0.044715 * (x * x * x))))


def _nt(a, b):
    return lax.dot_general(a, b, (((1,), (1,)), ((), ())), preferred_element_type=F32)


def _cparams(sem):
    return pltpu.CompilerParams(dimension_semantics=sem, vmem_limit_bytes=VMEM_LIMIT)


def _mod_kernel(c_ref, w_ref, b_ref, o_ref):
    o_ref[...] = jnp.dot(c_ref[...], w_ref[...], preferred_element_type=F32, precision=HIGHEST) + b_ref[...]


def _mod_call(c, w_ada, b_ada):
    bsz = c.shape[0]
    n = w_ada.shape[1]
    return pl.pallas_call(
        _mod_kernel,
        grid=(n // D_MODEL,),
        in_specs=[pl.BlockSpec((bsz, D_MODEL), lambda j: (0, 0)),
                  pl.BlockSpec((D_MODEL, D_MODEL), lambda j: (0, j)),
                  pl.BlockSpec((1, D_MODEL), lambda j: (0, j))],
        out_specs=pl.BlockSpec((bsz, D_MODEL), lambda j: (0, j)),
        out_shape=jax.ShapeDtypeStruct((bsz, n), F32),
        compiler_params=_cparams(("arbitrary",)),
        name="mod",
    )(c, w_ada, b_ada.reshape(1, n))


def _inproj_kernel(x_ref, mod_ref, nw_ref, w_ref, cos_ref, sin_ref,
                   qp_ref, qr_ref, ksel_ref, kwin_ref, vsel_ref, vwin_ref,
                   kcmp_ref, vcmp_ref, gate_ref, z_ref, xbc_ref, dt_ref):
    x = x_ref[0]
    ms = jnp.mean(x * x, axis=-1, keepdims=True)
    y = x * lax.rsqrt(ms + NORM_EPS) * nw_ref[...]
    h = (y * (1.0 + mod_ref[0, 1:2, :]) + mod_ref[0, 0:1, :]).astype(BF16)

    def proj(lo, hi):
        return jnp.dot(h, w_ref[:, lo:hi], preferred_element_type=F32)

    cos = cos_ref[...]
    sin = sin_ref[...]
    scale = HEAD_DIM ** -0.5
    q = proj(C_Q, C_Q + 512)
    qp_ref[0] = (q * scale).astype(BF16)
    qsw = proj(C_QSW, C_QSW + 512)
    cos4 = jnp.concatenate([cos] * 4, axis=1)
    sin4 = jnp.concatenate([sin] * 4, axis=1)
    qr_ref[0] = ((q * cos4 + qsw * sin4) * scale).astype(BF16)

    ks = proj(C_KSEL, C_KSEL + 128) * cos + proj(C_KSELSW, C_KSELSW + 128) * sin
    kw = proj(C_KWIN, C_KWIN + 128) * cos + proj(C_KWINSW, C_KWINSW + 128) * sin
    vs = proj(C_VSEL, C_VSEL + 128)
    vw = proj(C_VWIN, C_VWIN + 128)
    for ref, val in ((ksel_ref, ks), (kwin_ref, kw), (vsel_ref, vs), (vwin_ref, vw)):
        ref[0, 0] = val[:, :HEAD_DIM].astype(BF16)
        ref[0, 1] = val[:, HEAD_DIM:].astype(BF16)
    kcmp_ref[0] = proj(C_KCMP, C_KCMP + 128)
    vcmp_ref[0] = proj(C_VCMP, C_VCMP + 128)
    gl = proj(C_GATE, C_GATE + 256)
    sg = 1.0 / (1.0 + jnp.exp(-gl))
    gate_ref[0, 0] = sg[:, :128]
    gate_ref[0, 1] = sg[:, 128:]
    z_ref[0] = proj(C_Z, C_Z + 512)
    xbc_ref[0] = proj(C_XBC, C_XBC + 1024)
    dt_ref[0] = proj(C_DT, C_DT + 128)


def _inproj_call(x, mod, norm_w, w_ext, cos_t, sin_t, ts):
    bsz, s, d = x.shape
    grid = (bsz, s // ts)
    tok = lambda w: pl.BlockSpec((1, ts, w), lambda b, i: (b, i, 0))
    hm = pl.BlockSpec((1, NSA_GROUPS, ts, HEAD_DIM), lambda b, i: (b, 0, i, 0))
    sd = jax.ShapeDtypeStruct
    out_shape = [sd((bsz, s, 512), BF16), sd((bsz, s, 512), BF16)] + \
                [sd((bsz, NSA_GROUPS, s, HEAD_DIM), BF16)] * 4 + \
                [sd((bsz, s, 128), F32), sd((bsz, s, 128), F32),
                 sd((bsz, NSA_GROUPS, s, 128), F32),
                 sd((bsz, s, 512), F32), sd((bsz, s, 1024), F32), sd((bsz, s, 128), F32)]
    out_specs = [tok(512), tok(512), hm, hm, hm, hm, tok(128), tok(128),
                 pl.BlockSpec((1, NSA_GROUPS, ts, 128), lambda b, i: (b, 0, i, 0)),
                 tok(512), tok(1024), tok(128)]
    return pl.pallas_call(
        _inproj_kernel,
        grid=grid,
        in_specs=[tok(d),
                  pl.BlockSpec((1, 6, d), lambda b, i: (b, 0, 0)),
                  pl.BlockSpec((1, d), lambda b, i: (0, 0)),
                  pl.BlockSpec((d, C_END), lambda b, i: (0, 0)),
                  pl.BlockSpec((ts, 128), lambda b, i: (i, 0)),
                  pl.BlockSpec((ts, 128), lambda b, i: (i, 0))],
        out_specs=out_specs,
        out_shape=out_shape,
        compiler_params=_cparams(("parallel", "parallel")),
        name="inproj",
    )(x, mod, norm_w, w_ext, cos_t, sin_t)


def _compress_kernel(kx_ref, vx_ref, pek_ref, pev_ref, w1k_ref, w2k_ref, w1v_ref, w2v_ref,
                     kc_ref, vc_ref, hb_ref):
    nch = kx_ref.shape[2]

    def one(x_ref, pe_ref, w1_ref, w2_ref, o_ref):
        x = x_ref[0, 0]
        top = jnp.dot((x + pe_ref[0:1, :]).astype(BF16), w1_ref[0], preferred_element_type=F32)
        bot = jnp.dot((x + pe_ref[1:2, :]).astype(BF16), w1_ref[1], preferred_element_type=F32)
        hb_ref[0:nch, :] = bot
        hb_ref[nch:nch + 8, :] = jnp.zeros((8, CMP_HIDDEN), F32)
        pre = top + hb_ref[pl.ds(1, nch), :]
        out = jnp.dot(_gelu_tanh(pre).astype(BF16), w2_ref[...], preferred_element_type=F32)
        row = lax.broadcasted_iota(jnp.int32, out.shape, 0)
        o_ref[0, 0] = jnp.where(row < nch - 1, out, 0.0).astype(BF16)

    one(kx_ref, pek_ref, w1k_ref, w2k_ref, kc_ref)
    one(vx_ref, pev_ref, w1v_ref, w2v_ref, vc_ref)


def _compress_call(kx, vx, pek, pev, w1k, w2k, w1v, w2v):
    bsz, g, nch, cw = kx.shape
    xs = pl.BlockSpec((1, 1, nch, cw), lambda b, gi: (b, gi, 0, 0))
    full = lambda a: pl.BlockSpec(a.shape, lambda b, gi: (0,) * a.ndim)
    os_ = pl.BlockSpec((1, 1, nch, HEAD_DIM), lambda b, gi: (b, gi, 0, 0))
    sd = jax.ShapeDtypeStruct((bsz, g, nch, HEAD_DIM), BF16)
    return pl.pallas_call(
        _compress_kernel,
        grid=(bsz, g),
        in_specs=[xs, xs, full(pek), full(pev), full(w1k), full(w2k), full(w1v), full(w2v)],
        out_specs=[os_, os_],
        out_shape=[sd, sd],
        scratch_shapes=[pltpu.VMEM((nch + 8, CMP_HIDDEN), F32)],
        compiler_params=_cparams(("parallel", "parallel")),
        name="compress",
    )(kx, vx, pek, pev, w1k, w2k, w1v, w2v)


NSA_TQ = 128
NSA_TK = 1024


def _nsa_kernel(qp_ref, qr_ref, kc_ref, vc_ref, ks_ref, vs_ref, kw_ref, vw_ref, gate_ref, ovt_ref, exp_ref, o_ref):
    tq = NSA_TQ
    tk = NSA_TK
    rows = NSA_HPG * tq
    qt = pl.program_id(2)
    q0 = qt * tq
    ncmp = kc_ref.shape[2]
    nsel = ovt_ref.shape[0]
    stack = lambda a: jnp.concatenate([a] * NSA_HPG, axis=0)
    qp = jnp.concatenate([qp_ref[0, :, r * HEAD_DIM:(r + 1) * HEAD_DIM] for r in range(NSA_HPG)], axis=0)
    qr = jnp.concatenate([qr_ref[0, :, r * HEAD_DIM:(r + 1) * HEAD_DIM] for r in range(NSA_HPG)], axis=0)

    t_row = q0 + lax.broadcasted_iota(jnp.int32, (tq, ncmp), 0)
    cend = lax.broadcasted_iota(jnp.int32, (tq, ncmp), 1) * CMP_STRIDE + (CMP_BLOCK - 1)
    cbias = stack(jnp.where(cend <= t_row, 0.0, NEG_INF))
    s = jnp.where(cbias == 0.0, _nt(qp, kc_ref[0, 0]), NEG_INF)
    e = jnp.exp(s - jnp.max(s, axis=-1, keepdims=True))
    p = jnp.where(cbias == 0.0, e / jnp.sum(e, axis=-1, keepdims=True), 0.0)
    o_c = jnp.dot(p.astype(BF16), vc_ref[0, 0], preferred_element_type=F32)

    psum = p[0:tq] + p[tq:2 * tq] + p[2 * tq:3 * tq] + p[3 * tq:4 * tq]
    p_hi = psum.astype(BF16)
    p_lo = (psum - p_hi.astype(F32)).astype(BF16)
    ovt = ovt_ref[...]
    imp = _nt(ovt, p_hi) + _nt(ovt, p_lo)
    jblk = lax.broadcasted_iota(jnp.int32, (nsel, tq), 0)
    tt = q0 + lax.broadcasted_iota(jnp.int32, (nsel, tq), 1)
    blk_t = jnp.right_shift(tt, 6)
    forced = (jblk == 0) | (jblk == blk_t) | (jblk == blk_t - 1)
    xs = jnp.where(jblk * SEL_BLOCK <= tt, imp + jnp.where(forced, FORCE_BONUS, 0.0), NEG_INF)
    drop_t = jnp.full((nsel, tq), NEG_INF, F32)
    jblk_f = jblk.astype(F32)
    for _ in range(SEL_TOPN):
        mx = jnp.max(xs, axis=0, keepdims=True)
        idx = jnp.min(jnp.where(xs == mx, jblk_f, float(nsel)), axis=0, keepdims=True)
        hit = jblk_f == idx
        drop_t = jnp.where(hit, 0.0, drop_t)
        xs = jnp.where(hit, -jnp.inf, xs)
    drop = jnp.transpose(drop_t).astype(BF16)

    wk = WINDOW + tq
    wstart = pl.multiple_of(jnp.maximum(q0 - WINDOW, 0), tq)
    tq_w = q0 + lax.broadcasted_iota(jnp.int32, (tq, wk), 0)
    kpos_w = wstart + lax.broadcasted_iota(jnp.int32, (tq, wk), 1)
    bias_w = jnp.where((kpos_w <= tq_w) & (tq_w - kpos_w < WINDOW), 0.0, NEG_INF)
    sw = _nt(qr, kw_ref[0, 0, pl.ds(wstart, wk), :]) + stack(bias_w)
    ew = jnp.exp(sw - jnp.max(sw, axis=-1, keepdims=True))
    o_w = jnp.dot(ew.astype(BF16), vw_ref[0, 0, pl.ds(wstart, wk), :], preferred_element_type=F32) \
        / jnp.sum(ew, axis=-1, keepdims=True)

    tq_s = q0 + lax.broadcasted_iota(jnp.int32, (tq, tk), 0)

    def sel_body(kt, carry):
        m_i, l_i, acc = carry
        kbase = pl.multiple_of(kt * tk, tk)
        bias = jnp.dot(drop, exp_ref[kt], preferred_element_type=F32)
        kpos = kbase + lax.broadcasted_iota(jnp.int32, (tq, tk), 1)
        sc = _nt(qr, ks_ref[0, 0, pl.ds(kbase, tk), :]) + stack(jnp.where(kpos <= tq_s, bias, NEG_INF))
        m_new = jnp.maximum(m_i, jnp.max(sc, axis=-1, keepdims=True))
        alpha = jnp.exp(m_i - m_new)
        pe = jnp.exp(sc - m_new)
        l_new = alpha * l_i + jnp.sum(pe, axis=-1, keepdims=True)
        acc_new = alpha * acc + jnp.dot(pe.astype(BF16), vs_ref[0, 0, pl.ds(kbase, tk), :],
                                        preferred_element_type=F32)
        return m_new, l_new, acc_new

    init = (jnp.full((rows, 1), NEG_INF, F32), jnp.zeros((rows, 1), F32), jnp.zeros((rows, HEAD_DIM), F32))
    n_kt = (q0 + tq + tk - 1) // tk
    _, l_s, acc_s = lax.fori_loop(0, n_kt, sel_body, init)
    o_s = acc_s / l_s

    gates = gate_ref[0, 0]
    for r in range(NSA_HPG):
        sl = slice(r * tq, (r + 1) * tq)
        o_r = (gates[:, 3 * r:3 * r + 1] * o_c[sl] + gates[:, 3 * r + 1:3 * r + 2] * o_s[sl]
               + gates[:, 3 * r + 2:3 * r + 3] * o_w[sl])
        o_ref[0, :, r * HEAD_DIM:(r + 1) * HEAD_DIM] = o_r


def _nsa_call(qp, qr, kc, vc, ks, vs, kw, vw, gates, ovt, expand):
    bsz, s, _ = qp.shape
    tq = NSA_TQ
    ncmp = kc.shape[2]
    gw = NSA_HPG * HEAD_DIM
    qspec = pl.BlockSpec((1, tq, gw), lambda b, g, i: (b, i, g))
    cspec = pl.BlockSpec((1, 1, ncmp, HEAD_DIM), lambda b, g, i: (b, g, 0, 0))
    kvspec = pl.BlockSpec((1, 1, s, HEAD_DIM), lambda b, g, i: (b, g, 0, 0))
    return pl.pallas_call(
        _nsa_kernel,
        grid=(bsz, NSA_GROUPS, s // tq),
        in_specs=[qspec, qspec, cspec, cspec, kvspec, kvspec, kvspec, kvspec,
                  pl.BlockSpec((1, 1, tq, 128), lambda b, g, i: (b, g, i, 0)),
                  pl.BlockSpec(ovt.shape, lambda b, g, i: (0, 0)),
                  pl.BlockSpec(expand.shape, lambda b, g, i: (0, 0, 0))],
        out_specs=pl.BlockSpec((1, tq, gw), lambda b, g, i: (b, i, g)),
        out_shape=jax.ShapeDtypeStruct((bsz, s, NSA_WIDTH), F32),
        compiler_params=_cparams(("parallel", "parallel", "arbitrary")),
        name="nsa",
    )(qp, qr, kc, vc, ks, vs, kw, vw, gates, ovt, expand)


def _softplus(x):
    return jnp.maximum(x, 0.0) + jnp.log1p(jnp.exp(-jnp.abs(x)))


def _ssd_kernel(xbc_ref, z_ref, dt_ref, dtt_ref, cw_ref, cb_ref, dtb_ref, dtbt_ref, al_ref, alt_ref,
                dsk_ref, nw_ref, o_ref, tail_ref, xp_ref, st_ref, y_ref):
    L = SSD_CHUNK
    P = HEAD_DIM
    N = SSD_STATE
    c = pl.program_id(1)

    @pl.when(c == 0)
    def _():
        tail_ref[...] = jnp.zeros(tail_ref.shape, F32)
        st_ref[...] = jnp.zeros(st_ref.shape, F32)

    xin = xbc_ref[0]
    xp_ref[0:8, :] = tail_ref[...]
    xp_ref[8:8 + L, :] = xin
    tail_ref[...] = xin[L - 8:L, :]
    conv = cb_ref[...] + jnp.zeros_like(xin)
    for k in range(SSD_CONV):
        conv = conv + cw_ref[k:k + 1, :] * xp_ref[pl.ds(8 - (SSD_CONV - 1) + k, L), :]
    u = conv * (1.0 / (1.0 + jnp.exp(-conv)))
    xs = u[:, :SSD_WIDTH]

    dt_c = _softplus(dt_ref[0] + dtb_ref[...])
    dt_r = _softplus(dtt_ref[0] + dtbt_ref[...])
    a_c = -jnp.exp(al_ref[...])
    a_r = -jnp.exp(alt_ref[...])
    li = lax.broadcasted_iota(jnp.int32, (L, L), 0)
    si = lax.broadcasted_iota(jnp.int32, (L, L), 1)
    causal = li >= si
    tri = jnp.where(causal, 1.0, 0.0)
    tri_t = jnp.where(li <= si, 1.0, 0.0)
    acs_c = jnp.dot(tri, dt_c * a_c, preferred_element_type=F32, precision=HIGHEST)
    acs_r = jnp.dot(dt_r * a_r, tri_t, preferred_element_type=F32, precision=HIGHEST)

    for g in range(SSD_GROUPS):
        bm = u[:, SSD_WIDTH + g * N:SSD_WIDTH + (g + 1) * N]
        cm = u[:, SSD_WIDTH + SSD_GROUPS * N + g * N:SSD_WIDTH + SSD_GROUPS * N + (g + 1) * N]
        bm_b = bm.astype(BF16)
        cm_b = cm.astype(BF16)
        cb = _nt(cm_b, bm_b)
        bm_t = jnp.transpose(bm)
        for r in range(SSD_HEADS // SSD_GROUPS):
            hh = g * (SSD_HEADS // SSD_GROUPS) + r
            col = acs_c[:, hh:hh + 1]
            row = acs_r[hh:hh + 1, :]
            last = acs_r[hh:hh + 1, L - 1:L]
            lm = jnp.exp(jnp.where(causal, col - row, NEG_INF))
            x_h = xs[:, hh * P:(hh + 1) * P]
            xd = x_h * dt_c[:, hh:hh + 1]
            y_d = jnp.dot((cb * lm).astype(BF16), xd.astype(BF16), preferred_element_type=F32)
            prev = st_ref[hh]
            y_o = jnp.dot(cm_b, prev.astype(BF16), preferred_element_type=F32) * jnp.exp(col)
            dec = jnp.exp(last - row)
            st_new = jnp.dot((bm_t * dec).astype(BF16), xd.astype(BF16), preferred_element_type=F32)
            st_ref[hh] = prev * jnp.exp(last) + st_new
            y_ref[:, hh * P:(hh + 1) * P] = y_d + y_o + x_h * dsk_ref[0:1, hh:hh + 1]

    zz = z_ref[0]
    y = y_ref[...] * (zz * (1.0 / (1.0 + jnp.exp(-zz))))
    ms = jnp.mean(y * y, axis=-1, keepdims=True)
    o_ref[0] = (y * lax.rsqrt(ms + NORM_EPS) * nw_ref[...]).astype(o_ref.dtype)


def _ssd_call(xbc, z, dt, dtt, conv_w, conv_b, dt_bias, a_log, d_skip, norm_w):
    bsz, s, cch = xbc.shape
    L = SSD_CHUNK
    pad = lambda v: jnp.pad(v.reshape(1, -1), ((0, 0), (0, 128 - v.size)))
    full = lambda a: pl.BlockSpec(a.shape, lambda b, i: (0,) * a.ndim)
    args = (xbc, z, dt, dtt, conv_w, conv_b.reshape(1, -1), pad(dt_bias), dt_bias.reshape(-1, 1),
            pad(a_log), a_log.reshape(-1, 1), pad(d_skip), norm_w.reshape(1, -1))
    in_specs = [pl.BlockSpec((1, L, cch), lambda b, i: (b, i, 0)),
                pl.BlockSpec((1, L, SSD_WIDTH), lambda b, i: (b, i, 0)),
                pl.BlockSpec((1, L, 128), lambda b, i: (b, i, 0)),
                pl.BlockSpec((1, SSD_HEADS, L), lambda b, i: (b, 0, i))] + [full(a) for a in args[4:]]
    return pl.pallas_call(
        _ssd_kernel,
        grid=(bsz, s // L),
        in_specs=in_specs,
        out_specs=pl.BlockSpec((1, L, SSD_WIDTH), lambda b, i: (b, i, 0)),
        out_shape=jax.ShapeDtypeStruct((bsz, s, SSD_WIDTH), BF16),
        scratch_shapes=[pltpu.VMEM((8, cch), F32), pltpu.VMEM((8 + L, cch), F32),
                        pltpu.VMEM((SSD_HEADS, SSD_STATE, HEAD_DIM), F32), pltpu.VMEM((L, SSD_WIDTH), F32)],
        compiler_params=_cparams(("parallel", "arbitrary")),
        name="ssd",
    )(*args)


def _outproj_kernel(on_ref, os_ref, x_ref, mod_ref, nnw_ref, n2w_ref, wo_ref, wq_ref, qnw_ref, sk_ref,
                    x1_ref, h2_ref, sc_ref):
    o = on_ref[0]
    ms = jnp.mean(o * o, axis=-1, keepdims=True)
    on = (o * lax.rsqrt(ms + NORM_EPS) * nnw_ref[...]).astype(BF16)
    mix = jnp.dot(on, wo_ref[0:NSA_WIDTH, :], preferred_element_type=F32) + \
        jnp.dot(os_ref[0], wo_ref[NSA_WIDTH:, :], preferred_element_type=F32)
    x1 = x_ref[0] + mod_ref[0, 2:3, :] * mix
    x1_ref[0] = x1
    ms2 = jnp.mean(x1 * x1, axis=-1, keepdims=True)
    h2 = ((x1 * lax.rsqrt(ms2 + NORM_EPS) * n2w_ref[...]) * (1.0 + mod_ref[0, 4:5, :]) + mod_ref[0, 3:4, :]).astype(BF16)
    h2_ref[0] = h2
    half = PEER_QDIM // 2
    for hh in range(PEER_HEADS):
        qh = jnp.dot(h2, wq_ref[:, hh * PEER_QDIM:(hh + 1) * PEER_QDIM], preferred_element_type=F32)
        qn = (qh * lax.rsqrt(jnp.mean(qh * qh, axis=-1, keepdims=True) + NORM_EPS) * qnw_ref[...]).astype(BF16)
        for k in range(2):
            sc_ref[2 * hh + k] = _nt(sk_ref[hh, k], qn[:, k * half:(k + 1) * half])


def _outproj_call(o_nsa, o_ssd, x, mod, nsa_nw, n2w, w_out, wq, qnw, sub_keys, ts):
    bsz, s, d = x.shape
    nblk = s // ts
    tok = lambda w: pl.BlockSpec((1, ts, w), lambda b, i: (b, i, 0))
    full = lambda a: pl.BlockSpec(a.shape, lambda b, i: (0,) * a.ndim)
    return pl.pallas_call(
        _outproj_kernel,
        grid=(bsz, nblk),
        in_specs=[tok(NSA_WIDTH), tok(SSD_WIDTH), tok(d), pl.BlockSpec((1, 6, d), lambda b, i: (b, 0, 0)),
                  full(nsa_nw), full(n2w), full(w_out), full(wq), full(qnw), full(sub_keys)],
        out_specs=[tok(d), tok(d),
                   pl.BlockSpec((2 * PEER_HEADS, PEER_NKEYS, ts), lambda b, i: (0, 0, b * nblk + i))],
        out_shape=[jax.ShapeDtypeStruct((bsz, s, d), F32), jax.ShapeDtypeStruct((bsz, s, d), BF16),
                   jax.ShapeDtypeStruct((2 * PEER_HEADS, PEER_NKEYS, bsz * s), F32)],
        compiler_params=_cparams(("parallel", "parallel")),
        name="outproj",
    )(o_nsa, o_ssd, x, mod, nsa_nw, n2w, w_out, wq, qnw, sub_keys)


PEER_TT = 128
PEER_HG = 2


def _peersel_kernel(sc_ref, c_ref, e1_ref, r2_ref, e2_ref):
    nk = PEER_NKEYS
    tt = PEER_TT
    K = PEER_TOPK
    kidx = lax.broadcasted_iota(jnp.int32, (nk, tt), 0).astype(F32)
    i16 = lax.broadcasted_iota(jnp.int32, (K, tt), 0).astype(F32)

    def topk_sorted(x, exact_ties):
        rank = jnp.full((nk, tt), float(K), F32)
        vals = jnp.zeros((K, tt), F32)
        for j in range(K):
            mx = jnp.max(x, axis=0, keepdims=True)
            hit = x == mx
            if exact_ties:
                hit = kidx == jnp.min(jnp.where(hit, kidx, float(nk)), axis=0, keepdims=True)
            rank = jnp.where(hit, float(j), rank)
            x = jnp.where(hit, -jnp.inf, x)
            vals = jnp.where(i16 == float(j), mx, vals)
        taken = jnp.sum(jnp.where(x == -jnp.inf, 1.0, 0.0), axis=0, keepdims=True)
        return rank, vals, taken

    def first_stage(hh0, nh):
        ss = [sc_ref[2 * hh0 + i] for i in range(2 * nh)]
        fast = [topk_sorted(s, False) for s in ss]
        most = fast[0][2]
        for f in fast[1:]:
            most = jnp.maximum(most, f[2])
        res = lax.cond(
            jnp.max(most) > float(K),
            lambda: tuple(a for s in ss for a in topk_sorted(s, True)[:2]),
            lambda: tuple(a for f in fast for a in f[:2]))
        return [(ss[2 * i], ss[2 * i + 1]) + tuple(res[4 * i:4 * i + 4]) for i in range(nh)]

    def second_stage(hh, s1, s2, r1, v1a, r2, v2a):
        v1 = [v1a[i:i + 1, :] for i in range(K)]
        v2 = [v2a[i:i + 1, :] for i in range(K)]
        cmax = v1[0] + v2[0]
        n = jnp.zeros((K, tt), F32)
        f = v1a + v2[0]
        zsum = jnp.zeros((1, tt), F32)
        for _ in range(K):
            mx = jnp.max(f, axis=0, keepdims=True)
            iw = jnp.min(jnp.where(f == mx, i16, float(K)), axis=0, keepdims=True)
            hit = i16 == iw
            n = n + jnp.where(hit, 1.0, 0.0)
            zsum = zsum + jnp.exp(mx - cmax)
            nstar = jnp.sum(jnp.where(hit, n, 0.0), axis=0, keepdims=True)
            v2n = jnp.sum(jnp.where(i16 == nstar, v2a, 0.0), axis=0, keepdims=True)
            f = jnp.where(hit, v1a + v2n, f)
        cnt = jnp.zeros((nk, tt), F32)
        for i in range(K):
            cnt = jnp.where(r1 == float(i), n[i:i + 1, :], cnt)
        c_ref[hh] = cnt
        e1_ref[hh] = jnp.exp(s1 - v1[0]) / zsum
        r2_ref[hh] = r2.astype(BF16)
        e2_ref[hh] = jnp.exp(s2 - v2[0]).astype(BF16)

    for hh0 in range(0, PEER_HEADS, PEER_HG):
        for i, args in enumerate(first_stage(hh0, PEER_HG)):
            second_stage(hh0 + i, *args)


def _peersel_call(sc):
    _, nk, t = sc.shape
    tt = PEER_TT
    ospec = pl.BlockSpec((PEER_HEADS, nk, tt), lambda i: (0, 0, i))
    sd = lambda dt: jax.ShapeDtypeStruct((PEER_HEADS, nk, t), dt)
    return pl.pallas_call(
        _peersel_kernel,
        grid=(t // tt,),
        in_specs=[pl.BlockSpec((2 * PEER_HEADS, nk, tt), lambda i: (0, 0, i))],
        out_specs=[ospec] * 4,
        out_shape=[sd(F32), sd(F32), sd(BF16), sd(BF16)],
        compiler_params=_cparams(("parallel",)),
        name="peersel",
    )(sc)


PEER_TB = 512
PEER_EC = 1024


def _peer_kernel(h2_ref, c_ref, e1_ref, r2_ref, e2_ref, down0_ref, downn_ref, upt_ref, x1_ref, mod_ref, fw_ref,
                 o_ref, acc_ref, act_ref, w_ref):
    j = pl.program_id(1)
    nj = pl.num_programs(1)
    nk = PEER_NKEYS
    na = PEER_EC // nk

    def build_gates(chunk):
        a0 = pl.multiple_of(chunk * na, na)

        def rows(ref, hh, ai):
            grp = ref[hh, pl.ds(a0, na), :]
            r16 = jnp.broadcast_to(grp[ai:ai + 1, :], (16, PEER_TB)).astype(BF16)
            return jnp.concatenate([r16] * (nk // 16), axis=0)

        for ai in range(na):
            w = None
            for hh in range(PEER_HEADS):
                term = jnp.where(r2_ref[hh] < rows(c_ref, hh, ai), e2_ref[hh], 0.0) * rows(e1_ref, hh, ai)
                w = term if w is None else w + term
            w_ref[ai * nk:(ai + 1) * nk, :] = w

    @pl.when(j == 0)
    def _():
        acc_ref[...] = jnp.zeros(acc_ref.shape, F32)
        act_ref[...] = _gelu_tanh(_nt(down0_ref[...], h2_ref[...]).astype(BF16))
        build_gates(0)

    wa = w_ref[...] * act_ref[...]
    acc_ref[...] += jnp.dot(upt_ref[...], wa, preferred_element_type=F32)
    act_ref[...] = _gelu_tanh(_nt(downn_ref[...], h2_ref[...]).astype(BF16))
    build_gates(jnp.minimum(j + 1, nj - 1))

    @pl.when(j == nj - 1)
    def _():
        y = jnp.transpose(acc_ref[...])
        x2 = x1_ref[...] + mod_ref[0, 5:6, :] * y
        ms = jnp.mean(x2 * x2, axis=-1, keepdims=True)
        o_ref[...] = x2 * lax.rsqrt(ms + NORM_EPS) * fw_ref[...]


def _peer_call(h2, cnt, e1, r2, e2, down, upt, x1, mod, fw, s):
    t, d = h2.shape
    ne = down.shape[0]
    tb, ec = PEER_TB, PEER_EC
    per_b = s // tb
    last = ne // ec - 1
    dspec = pl.BlockSpec((PEER_HEADS, PEER_NKEYS, tb), lambda i, j: (0, 0, i))
    return pl.pallas_call(
        _peer_kernel,
        grid=(t // tb, ne // ec),
        in_specs=[pl.BlockSpec((tb, d), lambda i, j: (i, 0)), dspec, dspec, dspec, dspec,
                  pl.BlockSpec((ec, d), lambda i, j: (0, 0)),
                  pl.BlockSpec((ec, d), lambda i, j: (jnp.minimum(j + 1, last), 0)),
                  pl.BlockSpec((d, ec), lambda i, j: (0, j)),
                  pl.BlockSpec((tb, d), lambda i, j: (i, 0)),
                  pl.BlockSpec((1, 6, d), lambda i, j: (i // per_b, 0, 0)),
                  pl.BlockSpec((1, d), lambda i, j: (0, 0))],
        out_specs=pl.BlockSpec((tb, d), lambda i, j: (i, 0)),
        out_shape=jax.ShapeDtypeStruct((t, d), F32),
        scratch_shapes=[pltpu.VMEM((d, tb), F32), pltpu.VMEM((ec, tb), BF16), pltpu.VMEM((ec, tb), BF16)],
        compiler_params=_cparams(("parallel", "arbitrary")),
        name="peer",
    )(h2, cnt, e1, r2, e2, down, down, upt, x1, mod, fw)


def _rope_tables(s):
    half = ROPE_DIM // 2
    inv_freq = ROPE_THETA ** (-jnp.arange(half, dtype=F32) / half)
    ang = jnp.arange(s).astype(F32)[:, None] * inv_freq[None, :]
    cos, sin = jnp.cos(ang), jnp.sin(ang)
    one = jnp.ones((s, HEAD_DIM - ROPE_DIM), F32)
    cos64 = jnp.concatenate([cos, cos, one], axis=1)
    sin64 = jnp.concatenate([-sin, sin, 0.0 * one], axis=1)
    return jnp.concatenate([cos64, cos64], axis=1), jnp.concatenate([sin64, sin64], axis=1)


def _swap_cols(w):
    d, n = w.shape
    wh = w.reshape(d, n // HEAD_DIM, HEAD_DIM)
    half = ROPE_DIM // 2
    sw = jnp.concatenate([wh[..., half:ROPE_DIM], wh[..., :half], jnp.zeros_like(wh[..., ROPE_DIM:])], axis=-1)
    return sw.reshape(d, n)


def _pack_w_in(w):
    o = np.cumsum((0, 512, 128, 128, 128, 128, 128, 128, 24, 512, 1024, 8))
    q, kc, vc, ksel, vsel, kwin, vwin, gl, z, xbc, dtr = (w[:, o[i]:o[i + 1]] for i in range(11))
    gl = gl.reshape(-1, NSA_GROUPS, NSA_HPG * N_BRANCH)
    gl = jnp.pad(gl, ((0, 0), (0, 0), (0, 128 - NSA_HPG * N_BRANCH))).reshape(-1, 256)
    dtr = jnp.pad(dtr, ((0, 0), (0, 128 - SSD_HEADS)))
    cols = [q, _swap_cols(q), ksel, _swap_cols(ksel), kwin, _swap_cols(kwin), vsel, vwin, kc, vc, gl, z, xbc, dtr]
    return jnp.concatenate(cols, axis=1).astype(BF16)


def _overlap_t(s):
    n_cmp_pad = s // CMP_STRIDE
    n_sel = s // SEL_BLOCK
    cs = np.arange(n_cmp_pad) * CMP_STRIDE
    ss = np.arange(n_sel) * SEL_BLOCK
    ov = np.maximum(np.minimum(cs[None, :] + CMP_BLOCK, ss[:, None] + SEL_BLOCK)
                    - np.maximum(cs[None, :], ss[:, None]), 0).astype(np.float32) / CMP_BLOCK
    ov[:, n_cmp_pad - 1] = 0.0
    return jnp.asarray(ov, BF16)


def _block_expand(s):
    kpos = np.arange(s).reshape(s // NSA_TK, 1, NSA_TK)
    blk = np.arange(s // SEL_BLOCK).reshape(1, -1, 1)
    return jnp.asarray(kpos // SEL_BLOCK == blk, BF16)


def kernel(x, c, w_ada, b_ada, norm1_w, w_in, cmp_pe_k, cmp_pe_v, cmp_w1_k, cmp_w2_k, cmp_w1_v, cmp_w2_v,
           nsa_norm_w, conv_w, conv_b, dt_bias, a_log, d_skip, ssd_norm_w, w_out, norm2_w,
           peer_wq, peer_qnorm_w, peer_sub_keys, peer_down, peer_up, final_norm_w):
    bsz, s, d = x.shape
    assert d == D_MODEL and s % NSA_TK == 0 and w_ada.shape[0] == 1
    lyr = 0
    ts = 512

    mod = _mod_call(c, w_ada[lyr], b_ada[lyr]).reshape(bsz, 6, d)
    cos_t, sin_t = _rope_tables(s)
    (qp, qr, ksel, kwin, vsel, vwin, kcmp, vcmp, gates, z, xbc, dtr) = _inproj_call(
        x, mod, norm1_w[lyr].reshape(1, d), _pack_w_in(w_in[lyr]), cos_t, sin_t, ts)

    nch = s // CMP_STRIDE

    def chunks(t):
        return t.reshape(bsz, nch, CMP_STRIDE, NSA_GROUPS, HEAD_DIM).transpose(0, 3, 1, 2, 4).reshape(
            bsz, NSA_GROUPS, nch, CMP_STRIDE * HEAD_DIM)

    cw = CMP_STRIDE * HEAD_DIM
    kc, vc = _compress_call(
        chunks(kcmp), chunks(vcmp), cmp_pe_k[lyr].reshape(2, cw), cmp_pe_v[lyr].reshape(2, cw),
        cmp_w1_k[lyr].reshape(2, cw, CMP_HIDDEN).astype(BF16), cmp_w2_k[lyr].astype(BF16),
        cmp_w1_v[lyr].reshape(2, cw, CMP_HIDDEN).astype(BF16), cmp_w2_v[lyr].astype(BF16))

    o_nsa = _nsa_call(qp, qr, kc, vc, ksel, vsel, kwin, vwin, gates, _overlap_t(s), _block_expand(s))

    dtt = jnp.transpose(dtr[:, :, :SSD_HEADS], (0, 2, 1))
    o_ssd = _ssd_call(xbc, z, dtr, dtt, conv_w[lyr], conv_b[lyr], dt_bias[lyr], a_log[lyr], d_skip[lyr],
                      ssd_norm_w[lyr])

    x1, h2, sc = _outproj_call(
        o_nsa, o_ssd, x, mod, nsa_norm_w[lyr].reshape(1, -1), norm2_w[lyr].reshape(1, d),
        w_out[lyr].astype(BF16), peer_wq[lyr].astype(BF16), peer_qnorm_w[lyr].reshape(1, -1),
        peer_sub_keys[lyr].astype(BF16), ts)

    cnt, e1, r2, e2 = _peersel_call(sc)

    out = _peer_call(h2.reshape(bsz * s, d), cnt, e1, r2, e2, peer_down[lyr].astype(BF16),
                     jnp.transpose(peer_up[lyr]).astype(BF16), x1.reshape(bsz * s, d), mod,
                     final_norm_w.reshape(1, d), s)
    return out.reshape(bsz, s, d)
```

```python
import functools
import math

import numpy as np
import jax
import jax.numpy as jnp
from jax import lax
from jax.experimental import pallas as pl
from jax.experimental.pallas import tpu as pltpu

F32 = jnp.float32
BF16 = jnp.bfloat16
HIGHEST = lax.Precision.HIGHEST

D_MODEL = 1024
NSA_WIDTH = 512
SSD_WIDTH = 512
HEAD_DIM = 64
NSA_HEADS = 8
NSA_GROUPS = 2
NSA_HPG = 4
N_BRANCH = 3
CMP_BLOCK = 32
CMP_STRIDE = 16
CMP_HIDDEN = 256
SEL_BLOCK = 64
SEL_TOPN = 16
WINDOW = 512
ROPE_THETA = 500000.0
ROPE_DIM = 16
SSD_HEADS = 8
SSD_GROUPS = 2
SSD_STATE = 128
SSD_CONV = 4
SSD_CHUNK = 128
PEER_HEADS = 8
PEER_NKEYS = 128
PEER_QDIM = 256
PEER_TOPK = 16
NORM_EPS = 1e-6
NEG_INF = -1e30
LOG2E = math.log2(math.e)
FORCE_BONUS = 1e4

LANES = 128
VMEM_LIMIT = 56 * 1024 * 1024

C_Q, C_QSW, C_KSEL, C_KSELSW, C_KWIN, C_KWINSW = 0, 512, 1024, 1152, 1280, 1408
C_VSEL, C_VWIN, C_KCMP, C_VCMP, C_GATE, C_Z, C_XBC, C_DT, C_END = 1536, 1664, 1792, 1920, 2048, 2304, 2816, 3840, 3968


def _gelu_tanh(x):
    c = math.sqrt(2.0 / math.pi)
    return 0.5 * x * (1.0 + jnp.tanh(c * (x + 0.044715 * (x * x * x))))


def _nt(a, b):
    return lax.dot_general(a, b, (((1,), (1,)), ((), ())), preferred_element_type=F32)


def _cparams(sem):
    return pltpu.CompilerParams(dimension_semantics=sem, vmem_limit_bytes=VMEM_LIMIT)


def _mod_kernel(c_ref, w_ref, b_ref, o_ref):
    o_ref[...] = jnp.dot(c_ref[...], w_ref[...], preferred_element_type=F32, precision=HIGHEST) + b_ref[...]


def _mod_call(c, w_ada, b_ada):
    bsz = c.shape[0]
    n = w_ada.shape[1]
    return pl.pallas_call(
        _mod_kernel,
        grid=(n // D_MODEL,),
        in_specs=[pl.BlockSpec((bsz, D_MODEL), lambda j: (0, 0)),
                  pl.BlockSpec((D_MODEL, D_MODEL), lambda j: (0, j)),
                  pl.BlockSpec((1, D_MODEL), lambda j: (0, j))],
        out_specs=pl.BlockSpec((bsz, D_MODEL), lambda j: (0, j)),
        out_shape=jax.ShapeDtypeStruct((bsz, n), F32),
        compiler_params=_cparams(("arbitrary",)),
        name="mod",
    )(c, w_ada, b_ada.reshape(1, n))


def _inproj_kernel(x_ref, mod_ref, nw_ref, w_ref, cos_ref, sin_ref,
                   qp_ref, qr_ref, ksel_ref, kwin_ref, vsel_ref, vwin_ref,
                   kcmp_ref, vcmp_ref, gate_ref, z_ref, xbc_ref, dt_ref):
    x = x_ref[0]
    ms = jnp.mean(x * x, axis=-1, keepdims=True)
    y = x * lax.rsqrt(ms + NORM_EPS) * nw_ref[...]
    h = (y * (1.0 + mod_ref[0, 1:2, :]) + mod_ref[0, 0:1, :]).astype(BF16)

    def proj(lo, hi):
        return jnp.dot(h, w_ref[:, lo:hi], preferred_element_type=F32)

    cos = cos_ref[...]
    sin = sin_ref[...]
    scale = HEAD_DIM ** -0.5 * LOG2E
    q = proj(C_Q, C_Q + 512)
    qp_ref[0] = (q * scale).astype(BF16)
    qsw = proj(C_QSW, C_QSW + 512)
    cos4 = jnp.concatenate([cos] * 4, axis=1)
    sin4 = jnp.concatenate([sin] * 4, axis=1)
    qr_ref[0] = ((q * cos4 + qsw * sin4) * scale).astype(BF16)

    ks = proj(C_KSEL, C_KSEL + 128) * cos + proj(C_KSELSW, C_KSELSW + 128) * sin
    kw = proj(C_KWIN, C_KWIN + 128) * cos + proj(C_KWINSW, C_KWINSW + 128) * sin
    vs = proj(C_VSEL, C_VSEL + 128)
    vw = proj(C_VWIN, C_VWIN + 128)
    for ref, val in ((ksel_ref, ks), (kwin_ref, kw)):
        ref[0, 0] = val[:, :HEAD_DIM].astype(BF16)
        ref[0, 1] = val[:, HEAD_DIM:].astype(BF16)
    lane = lax.broadcasted_iota(jnp.int32, vs.shape, 1)
    ones_col = jnp.where(lane == HEAD_DIM, 1.0, 0.0)
    for ref, val in ((vsel_ref, vs), (vwin_ref, vw)):
        ref[0, 0] = jnp.where(lane < HEAD_DIM, val, ones_col).astype(BF16)
        ref[0, 1] = jnp.where(lane < HEAD_DIM, pltpu.roll(val, HEAD_DIM, axis=1), ones_col).astype(BF16)
    kcmp_ref[0] = proj(C_KCMP, C_KCMP + 128)
    vcmp_ref[0] = proj(C_VCMP, C_VCMP + 128)
    gl = proj(C_GATE, C_GATE + 256)
    sg = 1.0 / (1.0 + jnp.exp(-gl))
    gate_ref[0, 0] = sg[:, :128]
    gate_ref[0, 1] = sg[:, 128:]
    z_ref[0] = proj(C_Z, C_Z + 512)
    xbc_ref[0] = proj(C_XBC, C_XBC + 1024)
    dt_ref[0] = proj(C_DT, C_DT + 128)


def _inproj_call(x, mod, norm_w, w_ext, cos_t, sin_t, ts):
    bsz, s, d = x.shape
    grid = (bsz, s // ts)
    tok = lambda w: pl.BlockSpec((1, ts, w), lambda b, i: (b, i, 0))
    hm = pl.BlockSpec((1, NSA_GROUPS, ts, HEAD_DIM), lambda b, i: (b, 0, i, 0))
    hv = pl.BlockSpec((1, NSA_GROUPS, ts, 128), lambda b, i: (b, 0, i, 0))
    sd = jax.ShapeDtypeStruct
    out_shape = [sd((bsz, s, 512), BF16), sd((bsz, s, 512), BF16)] + \
                [sd((bsz, NSA_GROUPS, s, HEAD_DIM), BF16)] * 2 + [sd((bsz, NSA_GROUPS, s, 128), BF16)] * 2 + \
                [sd((bsz, s, 128), F32), sd((bsz, s, 128), F32),
                 sd((bsz, NSA_GROUPS, s, 128), F32),
                 sd((bsz, s, 512), F32), sd((bsz, s, 1024), F32), sd((bsz, s, 128), F32)]
    out_specs = [tok(512), tok(512), hm, hm, hv, hv, tok(128), tok(128),
                 pl.BlockSpec((1, NSA_GROUPS, ts, 128), lambda b, i: (b, 0, i, 0)),
                 tok(512), tok(1024), tok(128)]
    return pl.pallas_call(
        _inproj_kernel,
        grid=grid,
        in_specs=[tok(d),
                  pl.BlockSpec((1, 6, d), lambda b, i: (b, 0, 0)),
                  pl.BlockSpec((1, d), lambda b, i: (0, 0)),
                  pl.BlockSpec((d, C_END), lambda b, i: (0, 0)),
                  pl.BlockSpec((ts, 128), lambda b, i: (i, 0)),
                  pl.BlockSpec((ts, 128), lambda b, i: (i, 0))],
        out_specs=out_specs,
        out_shape=out_shape,
        compiler_params=_cparams(("parallel", "parallel")),
        name="inproj",
    )(x, mod, norm_w, w_ext, cos_t, sin_t)


def _compress_kernel(k_ref, v_ref, pek_ref, pev_ref, w1k_ref, w2k_ref, w1v_ref, w2v_ref,
                     kc_ref, vc_ref, hb_ref):
    nch = k_ref.shape[1] // CMP_STRIDE

    def one(x_ref, pe_ref, w1_ref, w2_ref, o_ref):
        xs = [x_ref[0, pl.ds(l, nch, stride=CMP_STRIDE), :] for l in range(CMP_STRIDE)]
        top = jnp.concatenate([(xs[l] + pe_ref[l:l + 1, :]).astype(BF16) for l in range(CMP_STRIDE)], axis=1)
        bot = jnp.concatenate([(xs[l] + pe_ref[CMP_STRIDE + l:CMP_STRIDE + l + 1, :]).astype(BF16)
                               for l in range(CMP_STRIDE)], axis=1)
        htop = jnp.dot(top, w1_ref[0, 0], preferred_element_type=F32)
        hb_ref[0:nch, :] = jnp.dot(bot, w1_ref[1, 0], preferred_element_type=F32)
        hb_ref[nch:nch + 8, :] = jnp.zeros((8, CMP_HIDDEN), F32)
        pre = htop + hb_ref[pl.ds(1, nch), :]
        out = jnp.dot(_gelu_tanh(pre).astype(BF16), w2_ref[...], preferred_element_type=F32)
        row = lax.broadcasted_iota(jnp.int32, out.shape, 0)
        o_ref[0, 0] = jnp.where(row < nch - 1, out, 0.0).astype(BF16)

    one(k_ref, pek_ref, w1k_ref, w2k_ref, kc_ref)
    one(v_ref, pev_ref, w1v_ref, w2v_ref, vc_ref)


def _compress_call(kcmp, vcmp, pek, pev, w1k, w2k, w1v, w2v):
    bsz, s, cw = kcmp.shape
    nch = s // CMP_STRIDE
    xs = pl.BlockSpec((1, s, cw), lambda b, gi: (b, 0, 0))
    full = lambda a: pl.BlockSpec(a.shape, lambda b, gi: (0,) * a.ndim)
    w1s = pl.BlockSpec((2, 1) + w1k.shape[2:], lambda b, gi: (0, gi, 0, 0))
    os_ = pl.BlockSpec((1, 1, nch, HEAD_DIM), lambda b, gi: (b, gi, 0, 0))
    sd = jax.ShapeDtypeStruct((bsz, NSA_GROUPS, nch, HEAD_DIM), BF16)
    return pl.pallas_call(
        _compress_kernel,
        grid=(bsz, NSA_GROUPS),
        in_specs=[xs, xs, full(pek), full(pev), w1s, full(w2k), w1s, full(w2v)],
        out_specs=[os_, os_],
        out_shape=[sd, sd],
        scratch_shapes=[pltpu.VMEM((nch + 8, CMP_HIDDEN), F32)],
        compiler_params=_cparams(("parallel", "parallel")),
        name="compress",
    )(kcmp, vcmp, pek, pev, w1k, w2k, w1v, w2v)


NSA_TQ = 128
NSA_TK = 1024


def _nsa_kernel(qp_ref, qr_ref, kc_ref, vc_ref, ks_ref, vs_ref, kw_ref, vw_ref, gate_ref, ovt_ref, exp_ref, o_ref):
    tq = NSA_TQ
    tk = NSA_TK
    rows = NSA_HPG * tq
    qt = pl.program_id(2)
    q0 = qt * tq
    ncmp = kc_ref.shape[2]
    nsel = ovt_ref.shape[0]
    stack = lambda a: jnp.concatenate([a] * NSA_HPG, axis=0)
    qp = jnp.concatenate([qp_ref[0, :, r * HEAD_DIM:(r + 1) * HEAD_DIM] for r in range(NSA_HPG)], axis=0)
    qr = jnp.concatenate([qr_ref[0, :, r * HEAD_DIM:(r + 1) * HEAD_DIM] for r in range(NSA_HPG)], axis=0)

    t_row = q0 + lax.broadcasted_iota(jnp.int32, (tq, ncmp), 0)
    cend = lax.broadcasted_iota(jnp.int32, (tq, ncmp), 1) * CMP_STRIDE + (CMP_BLOCK - 1)
    cbias = stack(jnp.where(cend <= t_row, 0.0, NEG_INF))
    s = jnp.where(cbias == 0.0, _nt(qp, kc_ref[0, 0]), NEG_INF)
    e = jnp.exp2(s - jnp.max(s, axis=-1, keepdims=True))
    p = jnp.where(cbias == 0.0, e / jnp.sum(e, axis=-1, keepdims=True), 0.0)
    o_c = jnp.dot(p.astype(BF16), vc_ref[0, 0], preferred_element_type=F32)

    psum = p[0:tq] + p[tq:2 * tq] + p[2 * tq:3 * tq] + p[3 * tq:4 * tq]
    p_hi = psum.astype(BF16)
    p_lo = (psum - p_hi.astype(F32)).astype(BF16)
    ovt = ovt_ref[...]
    imp = _nt(ovt, p_hi) + _nt(ovt, p_lo)
    jblk = lax.broadcasted_iota(jnp.int32, (nsel, tq), 0)
    tt = q0 + lax.broadcasted_iota(jnp.int32, (nsel, tq), 1)
    blk_t = jnp.right_shift(tt, 6)
    forced = (jblk == 0) | (jblk == blk_t) | (jblk == blk_t - 1)
    xs = jnp.where(jblk * SEL_BLOCK <= tt, imp + jnp.where(forced, FORCE_BONUS, 0.0), NEG_INF)
    drop_t = jnp.full((nsel, tq), NEG_INF, F32)
    jblk_f = jblk.astype(F32)
    for _ in range(SEL_TOPN):
        mx = jnp.max(xs, axis=0, keepdims=True)
        idx = jnp.min(jnp.where(xs == mx, jblk_f, float(nsel)), axis=0, keepdims=True)
        hit = jblk_f == idx
        drop_t = jnp.where(hit, 0.0, drop_t)
        xs = jnp.where(hit, -jnp.inf, xs)
    drop = jnp.transpose(drop_t).astype(BF16)

    wk = WINDOW + tq
    wstart = pl.multiple_of(jnp.maximum(q0 - WINDOW, 0), tq)
    tq_w = q0 + lax.broadcasted_iota(jnp.int32, (tq, wk), 0)
    kpos_w = wstart + lax.broadcasted_iota(jnp.int32, (tq, wk), 1)
    bias_w = jnp.where((kpos_w <= tq_w) & (tq_w - kpos_w < WINDOW), 0.0, NEG_INF)
    sw = _nt(qr, kw_ref[0, 0, pl.ds(wstart, wk), :]) + stack(bias_w)
    ew = jnp.exp2(sw - jnp.max(sw, axis=-1, keepdims=True))
    pv_w = jnp.dot(ew.astype(BF16), vw_ref[0, 0, pl.ds(wstart, wk), :], preferred_element_type=F32)
    o_w = pv_w[:, :HEAD_DIM] / pv_w[:, HEAD_DIM:HEAD_DIM + 1]

    tq_s = q0 + lax.broadcasted_iota(jnp.int32, (tq, tk), 0)

    def sel_body(kt, carry):
        m_i, acc = carry
        kbase = pl.multiple_of(kt * tk, tk)
        bias = jnp.dot(drop, exp_ref[kt], preferred_element_type=F32)
        kpos = kbase + lax.broadcasted_iota(jnp.int32, (tq, tk), 1)
        sc = _nt(qr, ks_ref[0, 0, pl.ds(kbase, tk), :]) + stack(jnp.where(kpos <= tq_s, bias, NEG_INF))
        m_new = jnp.maximum(m_i, jnp.max(sc, axis=-1, keepdims=True))
        pe = jnp.exp2(sc - m_new)
        acc_new = jnp.exp2(m_i - m_new) * acc + jnp.dot(pe.astype(BF16), vs_ref[0, 0, pl.ds(kbase, tk), :],
                                                        preferred_element_type=F32)
        return m_new, acc_new

    init = (jnp.full((rows, 1), NEG_INF, F32), jnp.zeros((rows, 2 * HEAD_DIM), F32))
    n_kt = (q0 + tq + tk - 1) // tk
    _, acc_s = lax.fori_loop(0, n_kt, sel_body, init)
    o_s = acc_s[:, :HEAD_DIM] / acc_s[:, HEAD_DIM:HEAD_DIM + 1]

    gates = gate_ref[0, 0]
    for r in range(NSA_HPG):
        sl = slice(r * tq, (r + 1) * tq)
        o_r = (gates[:, 3 * r:3 * r + 1] * o_c[sl] + gates[:, 3 * r + 1:3 * r + 2] * o_s[sl]
               + gates[:, 3 * r + 2:3 * r + 3] * o_w[sl])
        o_ref[0, :, r * HEAD_DIM:(r + 1) * HEAD_DIM] = o_r


def _nsa_call(qp, qr, kc, vc, ks, vs, kw, vw, gates, ovt, expand):
    bsz, s, _ = qp.shape
    tq = NSA_TQ
    ncmp = kc.shape[2]
    gw = NSA_HPG * HEAD_DIM
    qspec = pl.BlockSpec((1, tq, gw), lambda b, g, i: (b, i, g))
    cspec = pl.BlockSpec((1, 1, ncmp, HEAD_DIM), lambda b, g, i: (b, g, 0, 0))
    kspec = pl.BlockSpec((1, 1, s, HEAD_DIM), lambda b, g, i: (b, g, 0, 0))
    vspec = pl.BlockSpec((1, 1, s, 2 * HEAD_DIM), lambda b, g, i: (b, g, 0, 0))
    return pl.pallas_call(
        _nsa_kernel,
        grid=(bsz, NSA_GROUPS, s // tq),
        in_specs=[qspec, qspec, cspec, cspec, kspec, vspec, kspec, vspec,
                  pl.BlockSpec((1, 1, tq, 128), lambda b, g, i: (b, g, i, 0)),
                  pl.BlockSpec(ovt.shape, lambda b, g, i: (0, 0)),
                  pl.BlockSpec(expand.shape, lambda b, g, i: (0, 0, 0))],
        out_specs=pl.BlockSpec((1, tq, gw), lambda b, g, i: (b, i, g)),
        out_shape=jax.ShapeDtypeStruct((bsz, s, NSA_WIDTH), F32),
        compiler_params=_cparams(("parallel", "parallel", "arbitrary")),
        name="nsa",
    )(qp, qr, kc, vc, ks, vs, kw, vw, gates, ovt, expand)


def _softplus(x):
    return jnp.maximum(x, 0.0) + jnp.log1p(jnp.exp(-jnp.abs(x)))


def _ssd_kernel(xbc_ref, z_ref, dt_ref, dtt_ref, cw_ref, cb_ref, dtb_ref, dtbt_ref, al_ref, alt_ref,
                dsk_ref, nw_ref, o_ref, tail_ref, xp_ref, st_ref, y_ref):
    L = SSD_CHUNK
    P = HEAD_DIM
    N = SSD_STATE
    c = pl.program_id(1)

    @pl.when(c == 0)
    def _():
        tail_ref[...] = jnp.zeros(tail_ref.shape, F32)
        st_ref[...] = jnp.zeros(st_ref.shape, F32)

    xin = xbc_ref[0]
    xp_ref[0:8, :] = tail_ref[...]
    xp_ref[8:8 + L, :] = xin
    tail_ref[...] = xin[L - 8:L, :]
    conv = cb_ref[...] + jnp.zeros_like(xin)
    for k in range(SSD_CONV):
        conv = conv + cw_ref[k:k + 1, :] * xp_ref[pl.ds(8 - (SSD_CONV - 1) + k, L), :]
    u = conv * (1.0 / (1.0 + jnp.exp(-conv)))
    xs = u[:, :SSD_WIDTH]

    dt_c = _softplus(dt_ref[0] + dtb_ref[...])
    dt_r = _softplus(dtt_ref[0] + dtbt_ref[...])
    a_c = -jnp.exp(al_ref[...])
    a_r = -jnp.exp(alt_ref[...])
    li = lax.broadcasted_iota(jnp.int32, (L, L), 0)
    si = lax.broadcasted_iota(jnp.int32, (L, L), 1)
    causal = li >= si
    tri = jnp.where(causal, 1.0, 0.0)
    tri_t = jnp.where(li <= si, 1.0, 0.0)
    acs_c = jnp.dot(tri, dt_c * a_c, preferred_element_type=F32, precision=HIGHEST)
    acs_r = jnp.dot(dt_r * a_r, tri_t, preferred_element_type=F32, precision=HIGHEST)

    for g in range(SSD_GROUPS):
        bm = u[:, SSD_WIDTH + g * N:SSD_WIDTH + (g + 1) * N]
        cm = u[:, SSD_WIDTH + SSD_GROUPS * N + g * N:SSD_WIDTH + SSD_GROUPS * N + (g + 1) * N]
        bm_b = bm.astype(BF16)
        cm_b = cm.astype(BF16)
        cb = _nt(cm_b, bm_b)
        bm_t = jnp.transpose(bm)
        for r in range(SSD_HEADS // SSD_GROUPS):
            hh = g * (SSD_HEADS // SSD_GROUPS) + r
            col = acs_c[:, hh:hh + 1]
            row = acs_r[hh:hh + 1, :]
            last = acs_r[hh:hh + 1, L - 1:L]
            lm = jnp.exp(jnp.where(causal, col - row, NEG_INF))
            x_h = xs[:, hh * P:(hh + 1) * P]
            xd = x_h * dt_c[:, hh:hh + 1]
            y_d = jnp.dot((cb * lm).astype(BF16), xd.astype(BF16), preferred_element_type=F32)
            prev = st_ref[hh]
            y_o = jnp.dot(cm_b, prev.astype(BF16), preferred_element_type=F32) * jnp.exp(col)
            dec = jnp.exp(last - row)
            st_new = jnp.dot((bm_t * dec).astype(BF16), xd.astype(BF16), preferred_element_type=F32)
            st_ref[hh] = prev * jnp.exp(last) + st_new
            y_ref[:, hh * P:(hh + 1) * P] = y_d + y_o + x_h * dsk_ref[0:1, hh:hh + 1]

    zz = z_ref[0]
    y = y_ref[...] * (zz * (1.0 / (1.0 + jnp.exp(-zz))))
    ms = jnp.mean(y * y, axis=-1, keepdims=True)
    o_ref[0] = (y * lax.rsqrt(ms + NORM_EPS) * nw_ref[...]).astype(o_ref.dtype)


def _ssd_call(xbc, z, dt, dtt, conv_w, conv_b, dt_bias, a_log, d_skip, norm_w):
    bsz, s, cch = xbc.shape
    L = SSD_CHUNK
    pad = lambda v: jnp.pad(v.reshape(1, -1), ((0, 0), (0, 128 - v.size)))
    full = lambda a: pl.BlockSpec(a.shape, lambda b, i: (0,) * a.ndim)
    args = (xbc, z, dt, dtt, conv_w, conv_b.reshape(1, -1), pad(dt_bias), dt_bias.reshape(-1, 1),
            pad(a_log), a_log.reshape(-1, 1), pad(d_skip), norm_w.reshape(1, -1))
    in_specs = [pl.BlockSpec((1, L, cch), lambda b, i: (b, i, 0)),
                pl.BlockSpec((1, L, SSD_WIDTH), lambda b, i: (b, i, 0)),
                pl.BlockSpec((1, L, 128), lambda b, i: (b, i, 0)),
                pl.BlockSpec((1, SSD_HEADS, L), lambda b, i: (b, 0, i))] + [full(a) for a in args[4:]]
    return pl.pallas_call(
        _ssd_kernel,
        grid=(bsz, s // L),
        in_specs=in_specs,
        out_specs=pl.BlockSpec((1, L, SSD_WIDTH), lambda b, i: (b, i, 0)),
        out_shape=jax.ShapeDtypeStruct((bsz, s, SSD_WIDTH), BF16),
        scratch_shapes=[pltpu.VMEM((8, cch), F32), pltpu.VMEM((8 + L, cch), F32),
                        pltpu.VMEM((SSD_HEADS, SSD_STATE, HEAD_DIM), F32), pltpu.VMEM((L, SSD_WIDTH), F32)],
        compiler_params=_cparams(("parallel", "arbitrary")),
        name="ssd",
    )(*args)


def _outproj_kernel(on_ref, os_ref, x_ref, mod_ref, nnw_ref, n2w_ref, wo_ref, wq_ref, qnw_ref, sk_ref,
                    x1_ref, h2_ref, sc_ref):
    o = on_ref[0]
    ms = jnp.mean(o * o, axis=-1, keepdims=True)
    on = (o * lax.rsqrt(ms + NORM_EPS) * nnw_ref[...]).astype(BF16)
    mix = jnp.dot(on, wo_ref[0:NSA_WIDTH, :], preferred_element_type=F32) + \
        jnp.dot(os_ref[0], wo_ref[NSA_WIDTH:, :], preferred_element_type=F32)
    x1 = x_ref[0] + mod_ref[0, 2:3, :] * mix
    x1_ref[0] = x1
    ms2 = jnp.mean(x1 * x1, axis=-1, keepdims=True)
    h2 = ((x1 * lax.rsqrt(ms2 + NORM_EPS) * n2w_ref[...]) * (1.0 + mod_ref[0, 4:5, :]) + mod_ref[0, 3:4, :]).astype(BF16)
    h2_ref[0] = h2
    half = PEER_QDIM // 2
    for hh in range(PEER_HEADS):
        qh = jnp.dot(h2, wq_ref[:, hh * PEER_QDIM:(hh + 1) * PEER_QDIM], preferred_element_type=F32)
        qn = (qh * lax.rsqrt(jnp.mean(qh * qh, axis=-1, keepdims=True) + NORM_EPS) * qnw_ref[...]).astype(BF16)
        for k in range(2):
            sc_ref[2 * hh + k] = _nt(sk_ref[hh, k], qn[:, k * half:(k + 1) * half])


def _outproj_call(o_nsa, o_ssd, x, mod, nsa_nw, n2w, w_out, wq, qnw, sub_keys, ts):
    bsz, s, d = x.shape
    nblk = s // ts
    tok = lambda w: pl.BlockSpec((1, ts, w), lambda b, i: (b, i, 0))
    full = lambda a: pl.BlockSpec(a.shape, lambda b, i: (0,) * a.ndim)
    return pl.pallas_call(
        _outproj_kernel,
        grid=(bsz, nblk),
        in_specs=[tok(NSA_WIDTH), tok(SSD_WIDTH), tok(d), pl.BlockSpec((1, 6, d), lambda b, i: (b, 0, 0)),
                  full(nsa_nw), full(n2w), full(w_out), full(wq), full(qnw), full(sub_keys)],
        out_specs=[tok(d), tok(d),
                   pl.BlockSpec((2 * PEER_HEADS, PEER_NKEYS, ts), lambda b, i: (0, 0, b * nblk + i))],
        out_shape=[jax.ShapeDtypeStruct((bsz, s, d), F32), jax.ShapeDtypeStruct((bsz, s, d), BF16),
                   jax.ShapeDtypeStruct((2 * PEER_HEADS, PEER_NKEYS, bsz * s), F32)],
        compiler_params=_cparams(("parallel", "parallel")),
        name="outproj",
    )(o_nsa, o_ssd, x, mod, nsa_nw, n2w, w_out, wq, qnw, sub_keys)


PEER_TT = 128
PEER_HG = 2


def _peersel_kernel(sc_ref, c_ref, e1_ref, r2_ref, e2_ref):
    nk = PEER_NKEYS
    tt = PEER_TT
    K = PEER_TOPK
    kidx = lax.broadcasted_iota(jnp.int32, (nk, tt), 0).astype(F32)
    i16 = lax.broadcasted_iota(jnp.int32, (K, tt), 0).astype(F32)

    def topk_sorted(x, exact_ties):
        rank = jnp.full((nk, tt), float(K), F32)
        vals = jnp.zeros((K, tt), F32)
        for j in range(K):
            mx = jnp.max(x, axis=0, keepdims=True)
            hit = x == mx
            if exact_ties:
                hit = kidx == jnp.min(jnp.where(hit, kidx, float(nk)), axis=0, keepdims=True)
            rank = jnp.where(hit, float(j), rank)
            x = jnp.where(hit, -jnp.inf, x)
            vals = jnp.where(i16 == float(j), mx, vals)
        taken = jnp.sum(jnp.where(x == -jnp.inf, 1.0, 0.0), axis=0, keepdims=True)
        return rank, vals, taken

    def first_stage(hh0, nh):
        ss = [sc_ref[2 * hh0 + i] for i in range(2 * nh)]
        fast = [topk_sorted(s, False) for s in ss]
        most = fast[0][2]
        for f in fast[1:]:
            most = jnp.maximum(most, f[2])
        res = lax.cond(
            jnp.max(most) > float(K),
            lambda: tuple(a for s in ss for a in topk_sorted(s, True)[:2]),
            lambda: tuple(a for f in fast for a in f[:2]))
        return [(ss[2 * i], ss[2 * i + 1]) + tuple(res[4 * i:4 * i + 4]) for i in range(nh)]

    def second_stage(hh, s1, s2, r1, v1a, r2, v2a):
        v1 = [v1a[i:i + 1, :] for i in range(K)]
        v2 = [v2a[i:i + 1, :] for i in range(K)]
        cmax = v1[0] + v2[0]
        n = jnp.zeros((K, tt), F32)
        f = v1a + v2[0]
        zsum = jnp.zeros((1, tt), F32)
        for _ in range(K):
            mx = jnp.max(f, axis=0, keepdims=True)
            iw = jnp.min(jnp.where(f == mx, i16, float(K)), axis=0, keepdims=True)
            hit = i16 == iw
            n = n + jnp.where(hit, 1.0, 0.0)
            zsum = zsum + jnp.exp(mx - cmax)
            nstar = jnp.sum(jnp.where(hit, n, 0.0), axis=0, keepdims=True)
            v2n = jnp.sum(jnp.where(i16 == nstar, v2a, 0.0), axis=0, keepdims=True)
            f = jnp.where(hit, v1a + v2n, f)
        cnt = jnp.zeros((nk, tt), F32)
        for i in range(K):
            cnt = jnp.where(r1 == float(i), n[i:i + 1, :], cnt)
        c_ref[hh] = cnt
        e1_ref[hh] = jnp.exp(s1 - v1[0]) / zsum
        r2_ref[hh] = r2.astype(BF16)
        e2_ref[hh] = jnp.exp(s2 - v2[0]).astype(BF16)

    for hh0 in range(0, PEER_HEADS, PEER_HG):
        for i, args in enumerate(first_stage(hh0, PEER_HG)):
            second_stage(hh0 + i, *args)


def _peersel_call(sc):
    _, nk, t = sc.shape
    tt = PEER_TT
    ospec = pl.BlockSpec((PEER_HEADS, nk, tt), lambda i: (0, 0, i))
    sd = lambda dt: jax.ShapeDtypeStruct((PEER_HEADS, nk, t), dt)
    return pl.pallas_call(
        _peersel_kernel,
        grid=(t // tt,),
        in_specs=[pl.BlockSpec((2 * PEER_HEADS, nk, tt), lambda i: (0, 0, i))],
        out_specs=[ospec] * 4,
        out_shape=[sd(F32), sd(F32), sd(BF16), sd(BF16)],
        compiler_params=_cparams(("parallel",)),
        name="peersel",
    )(sc)


PEER_TB = 512
PEER_EC = 1024


def _peer_kernel(h2_ref, c_ref, e1_ref, r2_ref, e2_ref, down0_ref, downn_ref, upt_ref, x1_ref, mod_ref, fw_ref,
                 o_ref, acc_ref, act_ref, w_ref):
    j = pl.program_id(1)
    nj = pl.num_programs(1)
    nk = PEER_NKEYS
    na = PEER_EC // nk

    def build_gates(chunk):
        a0 = pl.multiple_of(chunk * na, na)

        def rows(ref, hh, ai):
            grp = ref[hh, pl.ds(a0, na), :]
            r16 = jnp.broadcast_to(grp[ai:ai + 1, :], (16, PEER_TB)).astype(BF16)
            return jnp.concatenate([r16] * (nk // 16), axis=0)

        for ai in range(na):
            w = None
            for hh in range(PEER_HEADS):
                term = jnp.where(r2_ref[hh] < rows(c_ref, hh, ai), e2_ref[hh], 0.0) * rows(e1_ref, hh, ai)
                w = term if w is None else w + term
            w_ref[ai * nk:(ai + 1) * nk, :] = w

    @pl.when(j == 0)
    def _():
        acc_ref[...] = jnp.zeros(acc_ref.shape, F32)
        act_ref[...] = _gelu_tanh(_nt(down0_ref[...], h2_ref[...]).astype(BF16))
        build_gates(0)

    wa = w_ref[...] * act_ref[...]
    acc_ref[...] += jnp.dot(upt_ref[...], wa, preferred_element_type=F32)
    act_ref[...] = _gelu_tanh(_nt(downn_ref[...], h2_ref[...]).astype(BF16))
    build_gates(jnp.minimum(j + 1, nj - 1))

    @pl.when(j == nj - 1)
    def _():
        y = jnp.transpose(acc_ref[...])
        x2 = x1_ref[...] + mod_ref[0, 5:6, :] * y
        ms = jnp.mean(x2 * x2, axis=-1, keepdims=True)
        o_ref[...] = x2 * lax.rsqrt(ms + NORM_EPS) * fw_ref[...]


def _peer_call(h2, cnt, e1, r2, e2, down, upt, x1, mod, fw, s):
    t, d = h2.shape
    ne = down.shape[0]
    tb, ec = PEER_TB, PEER_EC
    per_b = s // tb
    last = ne // ec - 1
    dspec = pl.BlockSpec((PEER_HEADS, PEER_NKEYS, tb), lambda i, j: (0, 0, i))
    return pl.pallas_call(
        _peer_kernel,
        grid=(t // tb, ne // ec),
        in_specs=[pl.BlockSpec((tb, d), lambda i, j: (i, 0)), dspec, dspec, dspec, dspec,
                  pl.BlockSpec((ec, d), lambda i, j: (0, 0)),
                  pl.BlockSpec((ec, d), lambda i, j: (jnp.minimum(j + 1, last), 0)),
                  pl.BlockSpec((d, ec), lambda i, j: (0, j)),
                  pl.BlockSpec((tb, d), lambda i, j: (i, 0)),
                  pl.BlockSpec((1, 6, d), lambda i, j: (i // per_b, 0, 0)),
                  pl.BlockSpec((1, d), lambda i, j: (0, 0))],
        out_specs=pl.BlockSpec((tb, d), lambda i, j: (i, 0)),
        out_shape=jax.ShapeDtypeStruct((t, d), F32),
        scratch_shapes=[pltpu.VMEM((d, tb), F32), pltpu.VMEM((ec, tb), BF16), pltpu.VMEM((ec, tb), BF16)],
        compiler_params=_cparams(("parallel", "arbitrary")),
        name="peer",
    )(h2, cnt, e1, r2, e2, down, down, upt, x1, mod, fw)


def _rope_tables(s):
    half = ROPE_DIM // 2
    inv_freq = ROPE_THETA ** (-jnp.arange(half, dtype=F32) / half)
    ang = jnp.arange(s).astype(F32)[:, None] * inv_freq[None, :]
    cos, sin = jnp.cos(ang), jnp.sin(ang)
    one = jnp.ones((s, HEAD_DIM - ROPE_DIM), F32)
    cos64 = jnp.concatenate([cos, cos, one], axis=1)
    sin64 = jnp.concatenate([-sin, sin, 0.0 * one], axis=1)
    return jnp.concatenate([cos64, cos64], axis=1), jnp.concatenate([sin64, sin64], axis=1)


def _swap_cols(w):
    d, n = w.shape
    wh = w.reshape(d, n // HEAD_DIM, HEAD_DIM)
    half = ROPE_DIM // 2
    sw = jnp.concatenate([wh[..., half:ROPE_DIM], wh[..., :half], jnp.zeros_like(wh[..., ROPE_DIM:])], axis=-1)
    return sw.reshape(d, n)


def _pack_w_in(w):
    o = np.cumsum((0, 512, 128, 128, 128, 128, 128, 128, 24, 512, 1024, 8))
    q, kc, vc, ksel, vsel, kwin, vwin, gl, z, xbc, dtr = (w[:, o[i]:o[i + 1]] for i in range(11))
    gl = gl.reshape(-1, NSA_GROUPS, NSA_HPG * N_BRANCH)
    gl = jnp.pad(gl, ((0, 0), (0, 0), (0, 128 - NSA_HPG * N_BRANCH))).reshape(-1, 256)
    dtr = jnp.pad(dtr, ((0, 0), (0, 128 - SSD_HEADS)))
    cols = [q, _swap_cols(q), ksel, _swap_cols(ksel), kwin, _swap_cols(kwin), vsel, vwin, kc, vc, gl, z, xbc, dtr]
    return jnp.concatenate(cols, axis=1).astype(BF16)


def _overlap_t(s):
    n_cmp_pad = s // CMP_STRIDE
    n_sel = s // SEL_BLOCK
    cs = np.arange(n_cmp_pad) * CMP_STRIDE
    ss = np.arange(n_sel) * SEL_BLOCK
    ov = np.maximum(np.minimum(cs[None, :] + CMP_BLOCK, ss[:, None] + SEL_BLOCK)
                    - np.maximum(cs[None, :], ss[:, None]), 0).astype(np.float32) / CMP_BLOCK
    ov[:, n_cmp_pad - 1] = 0.0
    return jnp.asarray(ov, BF16)


def _block_expand(s):
    kpos = np.arange(s).reshape(s // NSA_TK, 1, NSA_TK)
    blk = np.arange(s // SEL_BLOCK).reshape(1, -1, 1)
    return jnp.asarray(kpos // SEL_BLOCK == blk, BF16)


def kernel(x, c, w_ada, b_ada, norm1_w, w_in, cmp_pe_k, cmp_pe_v, cmp_w1_k, cmp_w2_k, cmp_w1_v, cmp_w2_v,
           nsa_norm_w, conv_w, conv_b, dt_bias, a_log, d_skip, ssd_norm_w, w_out, norm2_w,
           peer_wq, peer_qnorm_w, peer_sub_keys, peer_down, peer_up, final_norm_w):
    bsz, s, d = x.shape
    assert d == D_MODEL and s % NSA_TK == 0 and w_ada.shape[0] == 1
    lyr = 0
    ts = 512

    mod = _mod_call(c, w_ada[lyr], b_ada[lyr]).reshape(bsz, 6, d)
    cos_t, sin_t = _rope_tables(s)
    (qp, qr, ksel, kwin, vsel, vwin, kcmp, vcmp, gates, z, xbc, dtr) = _inproj_call(
        x, mod, norm1_w[lyr].reshape(1, d), _pack_w_in(w_in[lyr]), cos_t, sin_t, ts)

    def w1_groups(w1):
        w = w1.reshape(2, CMP_STRIDE, HEAD_DIM, CMP_HIDDEN)
        z = jnp.zeros_like(w)
        per_g = [jnp.concatenate([w, z], axis=2), jnp.concatenate([z, w], axis=2)]
        return jnp.stack(per_g, axis=1).reshape(2, NSA_GROUPS, CMP_STRIDE * 2 * HEAD_DIM, CMP_HIDDEN).astype(BF16)

    kc, vc = _compress_call(
        kcmp, vcmp, jnp.tile(cmp_pe_k[lyr], (1, NSA_GROUPS)), jnp.tile(cmp_pe_v[lyr], (1, NSA_GROUPS)),
        w1_groups(cmp_w1_k[lyr]), cmp_w2_k[lyr].astype(BF16), w1_groups(cmp_w1_v[lyr]), cmp_w2_v[lyr].astype(BF16))

    o_nsa = _nsa_call(qp, qr, kc, vc, ksel, vsel, kwin, vwin, gates, _overlap_t(s), _block_expand(s))

    dtt = jnp.transpose(dtr[:, :, :SSD_HEADS], (0, 2, 1))
    o_ssd = _ssd_call(xbc, z, dtr, dtt, conv_w[lyr], conv_b[lyr], dt_bias[lyr], a_log[lyr], d_skip[lyr],
                      ssd_norm_w[lyr])

    x1, h2, sc = _outproj_call(
        o_nsa, o_ssd, x, mod, nsa_norm_w[lyr].reshape(1, -1), norm2_w[lyr].reshape(1, d),
        w_out[lyr].astype(BF16), peer_wq[lyr].astype(BF16), peer_qnorm_w[lyr].reshape(1, -1),
        peer_sub_keys[lyr].astype(BF16), ts)

    cnt, e1, r2, e2 = _peersel_call(sc)

    out = _peer_call(h2.reshape(bsz * s, d), cnt, e1, r2, e2, peer_down[lyr].astype(BF16),
                     jnp.transpose(peer_up[lyr]).astype(BF16), x1.reshape(bsz * s, d), mod,
                     final_norm_w.reshape(1, d), s)
    return out.reshape(bsz, s, d)
```

```python
import functools
import math

import numpy as np
import jax
import jax.numpy as jnp
from jax import lax
from jax.experimental import pallas as pl
from jax.experimental.pallas import tpu as pltpu

F32 = jnp.float32
BF16 = jnp.bfloat16
HIGHEST = lax.Precision.HIGHEST

D_MODEL = 1024
NSA_WIDTH = 512
SSD_WIDTH = 512
HEAD_DIM = 64
NSA_HEADS = 8
NSA_GROUPS = 2
NSA_HPG = 4
N_BRANCH = 3
CMP_BLOCK = 32
CMP_STRIDE = 16
CMP_HIDDEN = 256
SEL_BLOCK = 64
SEL_TOPN = 16
WINDOW = 512
ROPE_THETA = 500000.0
ROPE_DIM = 16
SSD_HEADS = 8
SSD_GROUPS = 2
SSD_STATE = 128
SSD_CONV = 4
SSD_CHUNK = 128
PEER_HEADS = 8
PEER_NKEYS = 128
PEER_QDIM = 256
PEER_TOPK = 16
NORM_EPS = 1e-6
NEG_INF = -1e30
LOG2E = math.log2(math.e)
FORCE_BONUS = 1e4

LANES = 128
VMEM_LIMIT = 56 * 1024 * 1024

C_Q, C_QSW, C_KSEL, C_KSELSW, C_KWIN, C_KWINSW = 0, 512, 1024, 1152, 1280, 1408
C_VSEL, C_VWIN, C_KCMP, C_VCMP, C_GATE, C_Z, C_XBC, C_DT, C_END = 1536, 1664, 1792, 1920, 2048, 2304, 2816, 3840, 3968


def _gelu_tanh(x):
    c = math.sqrt(2.0 / math.pi)
    return 0.5 * x * (1.0 + jnp.tanh(c * (x + 0.044715 * (x * x * x))))


def _nt(a, b):
    return lax.dot_general(a, b, (((1,), (1,)), ((), ())), preferred_element_type=F32)


def _cparams(sem):
    return pltpu.CompilerParams(dimension_semantics=sem, vmem_limit_bytes=VMEM_LIMIT)


def _mod_kernel(c_ref, w_ref, b_ref, o_ref):
    o_ref[...] = jnp.dot(c_ref[...], w_ref[...], preferred_element_type=F32, precision=HIGHEST) + b_ref[...]


def _mod_call(c, w_ada, b_ada):
    bsz = c.shape[0]
    n = w_ada.shape[1]
    return pl.pallas_call(
        _mod_kernel,
        grid=(n // D_MODEL,),
        in_specs=[pl.BlockSpec((bsz, D_MODEL), lambda j: (0, 0)),
                  pl.BlockSpec((D_MODEL, D_MODEL), lambda j: (0, j)),
                  pl.BlockSpec((1, D_MODEL), lambda j: (0, j))],
        out_specs=pl.BlockSpec((bsz, D_MODEL), lambda j: (0, j)),
        out_shape=jax.ShapeDtypeStruct((bsz, n), F32),
        compiler_params=_cparams(("arbitrary",)),
        name="mod",
    )(c, w_ada, b_ada.reshape(1, n))


def _inproj_kernel(x_ref, mod_ref, nw_ref, w_ref, cos_ref, sin_ref,
                   qp_ref, qr_ref, ksel_ref, kwin_ref, vsel_ref, vwin_ref,
                   kcmp_ref, vcmp_ref, gate_ref, z_ref, xbc_ref, dt_ref):
    x = x_ref[0]
    ms = jnp.mean(x * x, axis=-1, keepdims=True)
    y = x * lax.rsqrt(ms + NORM_EPS) * nw_ref[...]
    h = (y * (1.0 + mod_ref[0, 1:2, :]) + mod_ref[0, 0:1, :]).astype(BF16)

    def proj(lo, hi):
        return jnp.dot(h, w_ref[:, lo:hi], preferred_element_type=F32)

    cos = cos_ref[...]
    sin = sin_ref[...]
    scale = HEAD_DIM ** -0.5 * LOG2E
    qa = proj(C_Q, C_KSEL)
    kv = proj(C_KSEL, C_GATE)
    rest = proj(C_GATE, C_END)
    q = qa[:, :512]
    qsw = qa[:, 512:]
    qp_ref[0] = (q * scale).astype(BF16)
    cos4 = jnp.concatenate([cos] * 4, axis=1)
    sin4 = jnp.concatenate([sin] * 4, axis=1)
    qr_ref[0] = ((q * cos4 + qsw * sin4) * scale).astype(BF16)

    part = lambda c: kv[:, c - C_KSEL:c - C_KSEL + 128]
    ks = part(C_KSEL) * cos + part(C_KSELSW) * sin
    kw = part(C_KWIN) * cos + part(C_KWINSW) * sin
    vs = part(C_VSEL)
    vw = part(C_VWIN)
    kwin_ref[0, 0] = kw[:, :HEAD_DIM].astype(BF16)
    kwin_ref[0, 1] = kw[:, HEAD_DIM:].astype(BF16)
    lane = lax.broadcasted_iota(jnp.int32, vs.shape, 1)
    blk = jnp.right_shift(pl.program_id(1) * ks.shape[0] + lax.broadcasted_iota(jnp.int32, ks.shape, 0), 6)
    hot_lo = jnp.where(lane - HEAD_DIM == blk, 1.0, 0.0)
    hot_hi = jnp.where(lane + HEAD_DIM == blk, 1.0, 0.0).astype(BF16)
    ksel_ref[0, 0, :, 0:128] = jnp.where(lane < HEAD_DIM, ks, hot_lo).astype(BF16)
    ksel_ref[0, 1, :, 0:128] = jnp.where(lane < HEAD_DIM, pltpu.roll(ks, HEAD_DIM, axis=1), hot_lo).astype(BF16)
    ksel_ref[0, 0, :, 128:256] = hot_hi
    ksel_ref[0, 1, :, 128:256] = hot_hi
    ones_col = jnp.where(lane == HEAD_DIM, 1.0, 0.0)
    for ref, val in ((vsel_ref, vs), (vwin_ref, vw)):
        ref[0, 0] = jnp.where(lane < HEAD_DIM, val, ones_col).astype(BF16)
        ref[0, 1] = jnp.where(lane < HEAD_DIM, pltpu.roll(val, HEAD_DIM, axis=1), ones_col).astype(BF16)
    kcmp_ref[0] = part(C_KCMP)
    vcmp_ref[0] = part(C_VCMP)
    sg = 1.0 / (1.0 + jnp.exp(-rest[:, :C_Z - C_GATE]))
    gate_ref[0, 0] = sg[:, :128]
    gate_ref[0, 1] = sg[:, 128:]
    z_ref[0] = rest[:, C_Z - C_GATE:C_XBC - C_GATE]
    xbc_ref[0] = rest[:, C_XBC - C_GATE:C_DT - C_GATE]
    dt_ref[0] = rest[:, C_DT - C_GATE:]


def _inproj_call(x, mod, norm_w, w_ext, cos_t, sin_t, ts):
    bsz, s, d = x.shape
    grid = (bsz, s // ts)
    tok = lambda w: pl.BlockSpec((1, ts, w), lambda b, i: (b, i, 0))
    hm = pl.BlockSpec((1, NSA_GROUPS, ts, HEAD_DIM), lambda b, i: (b, 0, i, 0))
    hv = pl.BlockSpec((1, NSA_GROUPS, ts, 128), lambda b, i: (b, 0, i, 0))
    hk = pl.BlockSpec((1, NSA_GROUPS, ts, 256), lambda b, i: (b, 0, i, 0))
    sd = jax.ShapeDtypeStruct
    out_shape = [sd((bsz, s, 512), BF16), sd((bsz, s, 512), BF16)] + \
                [sd((bsz, NSA_GROUPS, s, 256), BF16), sd((bsz, NSA_GROUPS, s, HEAD_DIM), BF16)] + \
                [sd((bsz, NSA_GROUPS, s, 128), BF16)] * 2 + \
                [sd((bsz, s, 128), F32), sd((bsz, s, 128), F32),
                 sd((bsz, NSA_GROUPS, s, 128), F32),
                 sd((bsz, s, 512), F32), sd((bsz, s, 1024), F32), sd((bsz, s, 128), F32)]
    out_specs = [tok(512), tok(512), hk, hm, hv, hv, tok(128), tok(128),
                 pl.BlockSpec((1, NSA_GROUPS, ts, 128), lambda b, i: (b, 0, i, 0)),
                 tok(512), tok(1024), tok(128)]
    return pl.pallas_call(
        _inproj_kernel,
        grid=grid,
        in_specs=[tok(d),
                  pl.BlockSpec((1, 6, d), lambda b, i: (b, 0, 0)),
                  pl.BlockSpec((1, d), lambda b, i: (0, 0)),
                  pl.BlockSpec((d, C_END), lambda b, i: (0, 0)),
                  pl.BlockSpec((ts, 128), lambda b, i: (i, 0)),
                  pl.BlockSpec((ts, 128), lambda b, i: (i, 0))],
        out_specs=out_specs,
        out_shape=out_shape,
        compiler_params=_cparams(("parallel", "parallel")),
        name="inproj",
    )(x, mod, norm_w, w_ext, cos_t, sin_t)


def _compress_kernel(k_ref, v_ref, pek_ref, pev_ref, w1k_ref, w2k_ref, w1v_ref, w2v_ref,
                     kc_ref, vc_ref, hb_ref):
    nch = k_ref.shape[1] // CMP_STRIDE

    def one(x_ref, pe_ref, w1_ref, w2_ref, o_ref):
        xs = [x_ref[0, pl.ds(l, nch, stride=CMP_STRIDE), :] for l in range(CMP_STRIDE)]
        top = jnp.concatenate([(xs[l] + pe_ref[l:l + 1, :]).astype(BF16) for l in range(CMP_STRIDE)], axis=1)
        bot = jnp.concatenate([(xs[l] + pe_ref[CMP_STRIDE + l:CMP_STRIDE + l + 1, :]).astype(BF16)
                               for l in range(CMP_STRIDE)], axis=1)
        htop = jnp.dot(top, w1_ref[0, 0], preferred_element_type=F32)
        hb_ref[0:nch, :] = jnp.dot(bot, w1_ref[1, 0], preferred_element_type=F32)
        hb_ref[nch:nch + 8, :] = jnp.zeros((8, CMP_HIDDEN), F32)
        pre = htop + hb_ref[pl.ds(1, nch), :]
        out = jnp.dot(_gelu_tanh(pre).astype(BF16), w2_ref[...], preferred_element_type=F32)
        row = lax.broadcasted_iota(jnp.int32, out.shape, 0)
        o_ref[0, 0] = jnp.where(row < nch - 1, out, 0.0).astype(BF16)

    one(k_ref, pek_ref, w1k_ref, w2k_ref, kc_ref)
    one(v_ref, pev_ref, w1v_ref, w2v_ref, vc_ref)


def _compress_call(kcmp, vcmp, pek, pev, w1k, w2k, w1v, w2v):
    bsz, s, cw = kcmp.shape
    nch = s // CMP_STRIDE
    xs = pl.BlockSpec((1, s, cw), lambda b, gi: (b, 0, 0))
    full = lambda a: pl.BlockSpec(a.shape, lambda b, gi: (0,) * a.ndim)
    w1s = pl.BlockSpec((2, 1) + w1k.shape[2:], lambda b, gi: (0, gi, 0, 0))
    os_ = pl.BlockSpec((1, 1, nch, HEAD_DIM), lambda b, gi: (b, gi, 0, 0))
    sd = jax.ShapeDtypeStruct((bsz, NSA_GROUPS, nch, HEAD_DIM), BF16)
    return pl.pallas_call(
        _compress_kernel,
        grid=(bsz, NSA_GROUPS),
        in_specs=[xs, xs, full(pek), full(pev), w1s, full(w2k), w1s, full(w2v)],
        out_specs=[os_, os_],
        out_shape=[sd, sd],
        scratch_shapes=[pltpu.VMEM((nch + 8, CMP_HIDDEN), F32)],
        compiler_params=_cparams(("parallel", "parallel")),
        name="compress",
    )(kcmp, vcmp, pek, pev, w1k, w2k, w1v, w2v)


NSA_TQ = 128
NSA_TK = 1024


def _nsa_kernel(qp_ref, qr_ref, kc_ref, vc_ref, ks_ref, vs_ref, kw_ref, vw_ref, gate_ref, ovt_ref, o_ref):
    tq = NSA_TQ
    tk = NSA_TK
    rows = NSA_HPG * tq
    qt = pl.program_id(2)
    q0 = qt * tq
    ncmp = kc_ref.shape[2]
    nsel = ovt_ref.shape[0]
    stack = lambda a: jnp.concatenate([a] * NSA_HPG, axis=0)
    qp = jnp.concatenate([qp_ref[0, :, r * HEAD_DIM:(r + 1) * HEAD_DIM] for r in range(NSA_HPG)], axis=0)
    qr = jnp.concatenate([qr_ref[0, :, r * HEAD_DIM:(r + 1) * HEAD_DIM] for r in range(NSA_HPG)], axis=0)

    t_row = q0 + lax.broadcasted_iota(jnp.int32, (tq, ncmp), 0)
    cend = lax.broadcasted_iota(jnp.int32, (tq, ncmp), 1) * CMP_STRIDE + (CMP_BLOCK - 1)
    cbias = stack(jnp.where(cend <= t_row, 0.0, NEG_INF))
    s = jnp.where(cbias == 0.0, _nt(qp, kc_ref[0, 0]), NEG_INF)
    e = jnp.exp2(s - jnp.max(s, axis=-1, keepdims=True))
    p = jnp.where(cbias == 0.0, e / jnp.sum(e, axis=-1, keepdims=True), 0.0)
    o_c = jnp.dot(p.astype(BF16), vc_ref[0, 0], preferred_element_type=F32)

    psum = p[0:tq] + p[tq:2 * tq] + p[2 * tq:3 * tq] + p[3 * tq:4 * tq]
    p_hi = psum.astype(BF16)
    p_lo = (psum - p_hi.astype(F32)).astype(BF16)
    ovt = ovt_ref[...]
    imp = _nt(ovt, p_hi) + _nt(ovt, p_lo)
    jblk = lax.broadcasted_iota(jnp.int32, (nsel, tq), 0)
    tt = q0 + lax.broadcasted_iota(jnp.int32, (nsel, tq), 1)
    blk_t = jnp.right_shift(tt, 6)
    forced = (jblk == 0) | (jblk == blk_t) | (jblk == blk_t - 1)
    xs = jnp.where(jblk * SEL_BLOCK <= tt, imp + jnp.where(forced, FORCE_BONUS, 0.0), NEG_INF)
    drop_t = jnp.full((nsel, tq), NEG_INF, F32)
    jblk_f = jblk.astype(F32)
    for _ in range(SEL_TOPN):
        mx = jnp.max(xs, axis=0, keepdims=True)
        idx = jnp.min(jnp.where(xs == mx, jblk_f, float(nsel)), axis=0, keepdims=True)
        hit = jblk_f == idx
        drop_t = jnp.where(hit, 0.0, drop_t)
        xs = jnp.where(hit, -jnp.inf, xs)
    drop = jnp.transpose(drop_t).astype(BF16)

    wk = WINDOW + tq
    wstart = pl.multiple_of(jnp.maximum(q0 - WINDOW, 0), tq)
    tq_w = q0 + lax.broadcasted_iota(jnp.int32, (tq, wk), 0)
    kpos_w = wstart + lax.broadcasted_iota(jnp.int32, (tq, wk), 1)
    bias_w = jnp.where((kpos_w <= tq_w) & (tq_w - kpos_w < WINDOW), 0.0, NEG_INF)
    sw = _nt(qr, kw_ref[0, 0, pl.ds(wstart, wk), :]) + stack(bias_w)
    ew = jnp.exp2(sw - jnp.max(sw, axis=-1, keepdims=True))
    pv_w = jnp.dot(ew.astype(BF16), vw_ref[0, 0, pl.ds(wstart, wk), :], preferred_element_type=F32)
    o_w = pv_w[:, :HEAD_DIM] / pv_w[:, HEAD_DIM:HEAD_DIM + 1]

    if nsel < 128:
        drop = jnp.concatenate([drop, jnp.zeros((tq, 128 - nsel), BF16)], axis=1)
    q_aug = jnp.concatenate([qr, stack(drop), jnp.zeros((rows, HEAD_DIM), BF16)], axis=1)

    def sel_tile(kt, carry, diagonal):
        m_i, acc = carry
        kbase = pl.multiple_of(kt * tk, tk)
        sc = _nt(q_aug, ks_ref[0, 0, pl.ds(kbase, tk), :])
        if diagonal:
            kpos = kbase + lax.broadcasted_iota(jnp.int32, (tq, tk), 1)
            tpos = q0 + lax.broadcasted_iota(jnp.int32, (tq, tk), 0)
            sc = sc + stack(jnp.where(kpos <= tpos, 0.0, NEG_INF))
        m_new = jnp.maximum(m_i, jnp.max(sc, axis=-1, keepdims=True))
        pe = jnp.exp2(sc - m_new)
        acc_new = jnp.exp2(m_i - m_new) * acc + jnp.dot(pe.astype(BF16), vs_ref[0, 0, pl.ds(kbase, tk), :],
                                                        preferred_element_type=F32)
        return m_new, acc_new

    init = (jnp.full((rows, 1), NEG_INF, F32), jnp.zeros((rows, 2 * HEAD_DIM), F32))
    last_kt = (q0 + tq - 1) // tk
    _, acc_s = sel_tile(last_kt, lax.fori_loop(0, last_kt, lambda kt, c: sel_tile(kt, c, False), init), True)
    o_s = acc_s[:, :HEAD_DIM] / acc_s[:, HEAD_DIM:HEAD_DIM + 1]

    gates = gate_ref[0, 0]
    for r in range(NSA_HPG):
        sl = slice(r * tq, (r + 1) * tq)
        o_r = (gates[:, 3 * r:3 * r + 1] * o_c[sl] + gates[:, 3 * r + 1:3 * r + 2] * o_s[sl]
               + gates[:, 3 * r + 2:3 * r + 3] * o_w[sl])
        o_ref[0, :, r * HEAD_DIM:(r + 1) * HEAD_DIM] = o_r


def _nsa_call(qp, qr, kc, vc, ks, vs, kw, vw, gates, ovt):
    bsz, s, _ = qp.shape
    tq = NSA_TQ
    ncmp = kc.shape[2]
    gw = NSA_HPG * HEAD_DIM
    qspec = pl.BlockSpec((1, tq, gw), lambda b, g, i: (b, i, g))
    cspec = pl.BlockSpec((1, 1, ncmp, HEAD_DIM), lambda b, g, i: (b, g, 0, 0))
    kspec = pl.BlockSpec((1, 1, s, HEAD_DIM), lambda b, g, i: (b, g, 0, 0))
    kaspec = pl.BlockSpec((1, 1, s, 4 * HEAD_DIM), lambda b, g, i: (b, g, 0, 0))
    vspec = pl.BlockSpec((1, 1, s, 2 * HEAD_DIM), lambda b, g, i: (b, g, 0, 0))
    return pl.pallas_call(
        _nsa_kernel,
        grid=(bsz, NSA_GROUPS, s // tq),
        in_specs=[qspec, qspec, cspec, cspec, kaspec, vspec, kspec, vspec,
                  pl.BlockSpec((1, 1, tq, 128), lambda b, g, i: (b, g, i, 0)),
                  pl.BlockSpec(ovt.shape, lambda b, g, i: (0, 0))],
        out_specs=pl.BlockSpec((1, tq, gw), lambda b, g, i: (b, i, g)),
        out_shape=jax.ShapeDtypeStruct((bsz, s, NSA_WIDTH), F32),
        compiler_params=_cparams(("parallel", "parallel", "arbitrary")),
        name="nsa",
    )(qp, qr, kc, vc, ks, vs, kw, vw, gates, ovt)


def _softplus(x):
    return jnp.maximum(x, 0.0) + jnp.log1p(jnp.exp(-jnp.abs(x)))


def _ssd_kernel(xbc_ref, z_ref, dt_ref, dtt_ref, cw_ref, cb_ref, dtb_ref, dtbt_ref, al_ref, alt_ref,
                dsk_ref, nw_ref, o_ref, tail_ref, xp_ref, st_ref, y_ref):
    L = SSD_CHUNK
    P = HEAD_DIM
    N = SSD_STATE
    c = pl.program_id(1)

    @pl.when(c == 0)
    def _():
        tail_ref[...] = jnp.zeros(tail_ref.shape, F32)
        st_ref[...] = jnp.zeros(st_ref.shape, F32)

    xin = xbc_ref[0]
    xp_ref[0:8, :] = tail_ref[...]
    xp_ref[8:8 + L, :] = xin
    tail_ref[...] = xin[L - 8:L, :]
    conv = cb_ref[...] + jnp.zeros_like(xin)
    for k in range(SSD_CONV):
        conv = conv + cw_ref[k:k + 1, :] * xp_ref[pl.ds(8 - (SSD_CONV - 1) + k, L), :]
    u = conv * (1.0 / (1.0 + jnp.exp(-conv)))
    xs = u[:, :SSD_WIDTH]

    dt_c = _softplus(dt_ref[0] + dtb_ref[...])
    dt_r = _softplus(dtt_ref[0] + dtbt_ref[...])
    a_c = -jnp.exp(al_ref[...])
    a_r = -jnp.exp(alt_ref[...])
    li = lax.broadcasted_iota(jnp.int32, (L, L), 0)
    si = lax.broadcasted_iota(jnp.int32, (L, L), 1)
    causal = li >= si
    tri = jnp.where(causal, 1.0, 0.0)
    tri_t = jnp.where(li <= si, 1.0, 0.0)
    acs_c = jnp.dot(tri, dt_c * a_c, preferred_element_type=F32, precision=HIGHEST)
    acs_r = jnp.dot(dt_r * a_r, tri_t, preferred_element_type=F32, precision=HIGHEST)

    for g in range(SSD_GROUPS):
        bm = u[:, SSD_WIDTH + g * N:SSD_WIDTH + (g + 1) * N]
        cm = u[:, SSD_WIDTH + SSD_GROUPS * N + g * N:SSD_WIDTH + SSD_GROUPS * N + (g + 1) * N]
        bm_b = bm.astype(BF16)
        cm_b = cm.astype(BF16)
        cb = _nt(cm_b, bm_b)
        bm_t = jnp.transpose(bm)
        for r in range(SSD_HEADS // SSD_GROUPS):
            hh = g * (SSD_HEADS // SSD_GROUPS) + r
            col = acs_c[:, hh:hh + 1]
            row = acs_r[hh:hh + 1, :]
            last = acs_r[hh:hh + 1, L - 1:L]
            lm = jnp.exp(jnp.where(causal, col - row, NEG_INF))
            x_h = xs[:, hh * P:(hh + 1) * P]
            xd = x_h * dt_c[:, hh:hh + 1]
            y_d = jnp.dot((cb * lm).astype(BF16), xd.astype(BF16), preferred_element_type=F32)
            prev = st_ref[hh]
            y_o = jnp.dot(cm_b, prev.astype(BF16), preferred_element_type=F32) * jnp.exp(col)
            dec = jnp.exp(last - row)
            st_new = jnp.dot((bm_t * dec).astype(BF16), xd.astype(BF16), preferred_element_type=F32)
            st_ref[hh] = prev * jnp.exp(last) + st_new
            y_ref[:, hh * P:(hh + 1) * P] = y_d + y_o + x_h * dsk_ref[0:1, hh:hh + 1]

    zz = z_ref[0]
    y = y_ref[...] * (zz * (1.0 / (1.0 + jnp.exp(-zz))))
    ms = jnp.mean(y * y, axis=-1, keepdims=True)
    o_ref[0] = (y * lax.rsqrt(ms + NORM_EPS) * nw_ref[...]).astype(o_ref.dtype)


def _ssd_call(xbc, z, dt, dtt, conv_w, conv_b, dt_bias, a_log, d_skip, norm_w):
    bsz, s, cch = xbc.shape
    L = SSD_CHUNK
    pad = lambda v: jnp.pad(v.reshape(1, -1), ((0, 0), (0, 128 - v.size)))
    full = lambda a: pl.BlockSpec(a.shape, lambda b, i: (0,) * a.ndim)
    args = (xbc, z, dt, dtt, conv_w, conv_b.reshape(1, -1), pad(dt_bias), dt_bias.reshape(-1, 1),
            pad(a_log), a_log.reshape(-1, 1), pad(d_skip), norm_w.reshape(1, -1))
    in_specs = [pl.BlockSpec((1, L, cch), lambda b, i: (b, i, 0)),
                pl.BlockSpec((1, L, SSD_WIDTH), lambda b, i: (b, i, 0)),
                pl.BlockSpec((1, L, 128), lambda b, i: (b, i, 0)),
                pl.BlockSpec((1, SSD_HEADS, L), lambda b, i: (b, 0, i))] + [full(a) for a in args[4:]]
    return pl.pallas_call(
        _ssd_kernel,
        grid=(bsz, s // L),
        in_specs=in_specs,
        out_specs=pl.BlockSpec((1, L, SSD_WIDTH), lambda b, i: (b, i, 0)),
        out_shape=jax.ShapeDtypeStruct((bsz, s, SSD_WIDTH), BF16),
        scratch_shapes=[pltpu.VMEM((8, cch), F32), pltpu.VMEM((8 + L, cch), F32),
                        pltpu.VMEM((SSD_HEADS, SSD_STATE, HEAD_DIM), F32), pltpu.VMEM((L, SSD_WIDTH), F32)],
        compiler_params=_cparams(("parallel", "arbitrary")),
        name="ssd",
    )(*args)


def _outproj_kernel(on_ref, os_ref, x_ref, mod_ref, nnw_ref, n2w_ref, wo_ref, wq_ref, qnw_ref, sk_ref,
                    x1_ref, h2_ref, sc_ref):
    o = on_ref[0]
    ms = jnp.mean(o * o, axis=-1, keepdims=True)
    on = (o * lax.rsqrt(ms + NORM_EPS) * nnw_ref[...]).astype(BF16)
    mix = jnp.dot(jnp.concatenate([on, os_ref[0]], axis=1), wo_ref[...], preferred_element_type=F32)
    x1 = x_ref[0] + mod_ref[0, 2:3, :] * mix
    x1_ref[0] = x1
    ms2 = jnp.mean(x1 * x1, axis=-1, keepdims=True)
    h2 = ((x1 * lax.rsqrt(ms2 + NORM_EPS) * n2w_ref[...]) * (1.0 + mod_ref[0, 4:5, :]) + mod_ref[0, 3:4, :]).astype(BF16)
    h2_ref[0] = h2
    qall = jnp.dot(h2, wq_ref[...], preferred_element_type=F32)
    for hh in range(PEER_HEADS):
        qh = qall[:, hh * PEER_QDIM:(hh + 1) * PEER_QDIM]
        qn = (qh * lax.rsqrt(jnp.mean(qh * qh, axis=-1, keepdims=True) + NORM_EPS) * qnw_ref[...]).astype(BF16)
        both = _nt(sk_ref[hh], qn)
        sc_ref[2 * hh] = both[:PEER_NKEYS]
        sc_ref[2 * hh + 1] = both[PEER_NKEYS:]


def _outproj_call(o_nsa, o_ssd, x, mod, nsa_nw, n2w, w_out, wq, qnw, sub_keys, ts):
    bsz, s, d = x.shape
    nblk = s // ts
    tok = lambda w: pl.BlockSpec((1, ts, w), lambda b, i: (b, i, 0))
    full = lambda a: pl.BlockSpec(a.shape, lambda b, i: (0,) * a.ndim)
    return pl.pallas_call(
        _outproj_kernel,
        grid=(bsz, nblk),
        in_specs=[tok(NSA_WIDTH), tok(SSD_WIDTH), tok(d), pl.BlockSpec((1, 6, d), lambda b, i: (b, 0, 0)),
                  full(nsa_nw), full(n2w), full(w_out), full(wq), full(qnw), full(sub_keys)],
        out_specs=[tok(d), tok(d),
                   pl.BlockSpec((2 * PEER_HEADS, PEER_NKEYS, ts), lambda b, i: (0, 0, b * nblk + i))],
        out_shape=[jax.ShapeDtypeStruct((bsz, s, d), F32), jax.ShapeDtypeStruct((bsz, s, d), BF16),
                   jax.ShapeDtypeStruct((2 * PEER_HEADS, PEER_NKEYS, bsz * s), F32)],
        compiler_params=_cparams(("parallel", "parallel")),
        name="outproj",
    )(o_nsa, o_ssd, x, mod, nsa_nw, n2w, w_out, wq, qnw, sub_keys)


PEER_TT = 128
PEER_HG = 2


def _peersel_kernel(sc_ref, c_ref, e1_ref, r2_ref, e2_ref):
    nk = PEER_NKEYS
    tt = PEER_TT
    K = PEER_TOPK
    kidx = lax.broadcasted_iota(jnp.int32, (nk, tt), 0).astype(F32)
    i16 = lax.broadcasted_iota(jnp.int32, (K, tt), 0).astype(F32)

    def topk_sorted(x, exact_ties):
        rank = jnp.full((nk, tt), float(K), F32)
        vals = jnp.zeros((K, tt), F32)
        for j in range(K):
            mx = jnp.max(x, axis=0, keepdims=True)
            hit = x == mx
            if exact_ties:
                hit = kidx == jnp.min(jnp.where(hit, kidx, float(nk)), axis=0, keepdims=True)
            rank = jnp.where(hit, float(j), rank)
            x = jnp.where(hit, -jnp.inf, x)
            vals = jnp.where(i16 == float(j), mx, vals)
        taken = jnp.sum(jnp.where(x == -jnp.inf, 1.0, 0.0), axis=0, keepdims=True)
        return rank, vals, taken

    def first_stage(hh0, nh):
        ss = [sc_ref[2 * hh0 + i] for i in range(2 * nh)]
        fast = [topk_sorted(s, False) for s in ss]
        most = fast[0][2]
        for f in fast[1:]:
            most = jnp.maximum(most, f[2])
        res = lax.cond(
            jnp.max(most) > float(K),
            lambda: tuple(a for s in ss for a in topk_sorted(s, True)[:2]),
            lambda: tuple(a for f in fast for a in f[:2]))
        return [(ss[2 * i], ss[2 * i + 1]) + tuple(res[4 * i:4 * i + 4]) for i in range(nh)]

    def second_stage(hh, s1, s2, r1, v1a, r2, v2a):
        v1 = [v1a[i:i + 1, :] for i in range(K)]
        v2 = [v2a[i:i + 1, :] for i in range(K)]
        cmax = v1[0] + v2[0]
        n = jnp.zeros((K, tt), F32)
        f = v1a + v2[0]
        zsum = jnp.zeros((1, tt), F32)
        for _ in range(K):
            mx = jnp.max(f, axis=0, keepdims=True)
            iw = jnp.min(jnp.where(f == mx, i16, float(K)), axis=0, keepdims=True)
            hit = i16 == iw
            n = n + jnp.where(hit, 1.0, 0.0)
            zsum = zsum + jnp.exp(mx - cmax)
            nstar = jnp.sum(jnp.where(hit, n, 0.0), axis=0, keepdims=True)
            v2n = jnp.sum(jnp.where(i16 == nstar, v2a, 0.0), axis=0, keepdims=True)
            f = jnp.where(hit, v1a + v2n, f)
        cnt = jnp.zeros((nk, tt), F32)
        for i in range(K):
            cnt = jnp.where(r1 == float(i), n[i:i + 1, :], cnt)
        c_ref[hh] = cnt
        e1_ref[hh] = jnp.exp(s1 - v1[0]) / zsum
        r2_ref[hh] = r2.astype(BF16)
        e2_ref[hh] = jnp.exp(s2 - v2[0]).astype(BF16)

    for hh0 in range(0, PEER_HEADS, PEER_HG):
        for i, args in enumerate(first_stage(hh0, PEER_HG)):
            second_stage(hh0 + i, *args)


def _peersel_call(sc):
    _, nk, t = sc.shape
    tt = PEER_TT
    ospec = pl.BlockSpec((PEER_HEADS, nk, tt), lambda i: (0, 0, i))
    sd = lambda dt: jax.ShapeDtypeStruct((PEER_HEADS, nk, t), dt)
    return pl.pallas_call(
        _peersel_kernel,
        grid=(t // tt,),
        in_specs=[pl.BlockSpec((2 * PEER_HEADS, nk, tt), lambda i: (0, 0, i))],
        out_specs=[ospec] * 4,
        out_shape=[sd(F32), sd(F32), sd(BF16), sd(BF16)],
        compiler_params=_cparams(("parallel",)),
        name="peersel",
    )(sc)


PEER_TB = 512
PEER_EC = 1024


def _peer_kernel(h2_ref, c_ref, e1_ref, r2_ref, e2_ref, down0_ref, downn_ref, upt_ref, x1_ref, mod_ref, fw_ref,
                 o_ref, acc_ref, act_ref, w_ref):
    j = pl.program_id(1)
    nj = pl.num_programs(1)
    nk = PEER_NKEYS
    na = PEER_EC // nk

    def build_gates(chunk):
        a0 = pl.multiple_of(chunk * na, na)

        def rows(ref, hh, ai):
            grp = ref[hh, pl.ds(a0, na), :]
            r16 = jnp.broadcast_to(grp[ai:ai + 1, :], (16, PEER_TB)).astype(BF16)
            return jnp.concatenate([r16] * (nk // 16), axis=0)

        for ai in range(na):
            w = None
            for hh in range(PEER_HEADS):
                term = jnp.where(r2_ref[hh] < rows(c_ref, hh, ai), e2_ref[hh], 0.0) * rows(e1_ref, hh, ai)
                w = term if w is None else w + term
            w_ref[ai * nk:(ai + 1) * nk, :] = w

    @pl.when(j == 0)
    def _():
        acc_ref[...] = jnp.zeros(acc_ref.shape, F32)
        act_ref[...] = _gelu_tanh(_nt(down0_ref[...], h2_ref[...]).astype(BF16))
        build_gates(0)

    wa = w_ref[...] * act_ref[...]
    acc_ref[...] += jnp.dot(upt_ref[...], wa, preferred_element_type=F32)
    act_ref[...] = _gelu_tanh(_nt(downn_ref[...], h2_ref[...]).astype(BF16))
    build_gates(jnp.minimum(j + 1, nj - 1))

    @pl.when(j == nj - 1)
    def _():
        y = jnp.transpose(acc_ref[...])
        x2 = x1_ref[...] + mod_ref[0, 5:6, :] * y
        ms = jnp.mean(x2 * x2, axis=-1, keepdims=True)
        o_ref[...] = x2 * lax.rsqrt(ms + NORM_EPS) * fw_ref[...]


def _peer_call(h2, cnt, e1, r2, e2, down, upt, x1, mod, fw, s):
    t, d = h2.shape
    ne = down.shape[0]
    tb, ec = PEER_TB, PEER_EC
    per_b = s // tb
    last = ne // ec - 1
    dspec = pl.BlockSpec((PEER_HEADS, PEER_NKEYS, tb), lambda i, j: (0, 0, i))
    return pl.pallas_call(
        _peer_kernel,
        grid=(t // tb, ne // ec),
        in_specs=[pl.BlockSpec((tb, d), lambda i, j: (i, 0)), dspec, dspec, dspec, dspec,
                  pl.BlockSpec((ec, d), lambda i, j: (0, 0)),
                  pl.BlockSpec((ec, d), lambda i, j: (jnp.minimum(j + 1, last), 0)),
                  pl.BlockSpec((d, ec), lambda i, j: (0, j)),
                  pl.BlockSpec((tb, d), lambda i, j: (i, 0)),
                  pl.BlockSpec((1, 6, d), lambda i, j: (i // per_b, 0, 0)),
                  pl.BlockSpec((1, d), lambda i, j: (0, 0))],
        out_specs=pl.BlockSpec((tb, d), lambda i, j: (i, 0)),
        out_shape=jax.ShapeDtypeStruct((t, d), F32),
        scratch_shapes=[pltpu.VMEM((d, tb), F32), pltpu.VMEM((ec, tb), BF16), pltpu.VMEM((ec, tb), BF16)],
        compiler_params=_cparams(("parallel", "arbitrary")),
        name="peer",
    )(h2, cnt, e1, r2, e2, down, down, upt, x1, mod, fw)


def _rope_tables(s):
    half = ROPE_DIM // 2
    inv_freq = ROPE_THETA ** (-jnp.arange(half, dtype=F32) / half)
    ang = jnp.arange(s).astype(F32)[:, None] * inv_freq[None, :]
    cos, sin = jnp.cos(ang), jnp.sin(ang)
    one = jnp.ones((s, HEAD_DIM - ROPE_DIM), F32)
    cos64 = jnp.concatenate([cos, cos, one], axis=1)
    sin64 = jnp.concatenate([-sin, sin, 0.0 * one], axis=1)
    return jnp.concatenate([cos64, cos64], axis=1), jnp.concatenate([sin64, sin64], axis=1)


def _swap_cols(w):
    d, n = w.shape
    wh = w.reshape(d, n // HEAD_DIM, HEAD_DIM)
    half = ROPE_DIM // 2
    sw = jnp.concatenate([wh[..., half:ROPE_DIM], wh[..., :half], jnp.zeros_like(wh[..., ROPE_DIM:])], axis=-1)
    return sw.reshape(d, n)


def _pack_w_in(w):
    o = np.cumsum((0, 512, 128, 128, 128, 128, 128, 128, 24, 512, 1024, 8))
    q, kc, vc, ksel, vsel, kwin, vwin, gl, z, xbc, dtr = (w[:, o[i]:o[i + 1]] for i in range(11))
    gl = gl.reshape(-1, NSA_GROUPS, NSA_HPG * N_BRANCH)
    gl = jnp.pad(gl, ((0, 0), (0, 0), (0, 128 - NSA_HPG * N_BRANCH))).reshape(-1, 256)
    dtr = jnp.pad(dtr, ((0, 0), (0, 128 - SSD_HEADS)))
    cols = [q, _swap_cols(q), ksel, _swap_cols(ksel), kwin, _swap_cols(kwin), vsel, vwin, kc, vc, gl, z, xbc, dtr]
    return jnp.concatenate(cols, axis=1).astype(BF16)


def _overlap_t(s):
    n_cmp_pad = s // CMP_STRIDE
    n_sel = s // SEL_BLOCK
    cs = np.arange(n_cmp_pad) * CMP_STRIDE
    ss = np.arange(n_sel) * SEL_BLOCK
    ov = np.maximum(np.minimum(cs[None, :] + CMP_BLOCK, ss[:, None] + SEL_BLOCK)
                    - np.maximum(cs[None, :], ss[:, None]), 0).astype(np.float32) / CMP_BLOCK
    ov[:, n_cmp_pad - 1] = 0.0
    return jnp.asarray(ov, BF16)


def _blockdiag_keys(sub_keys):
    h, two, n, half = sub_keys.shape
    z = jnp.zeros((h, n, half), sub_keys.dtype)
    top = jnp.concatenate([sub_keys[:, 0], z], axis=2)
    bot = jnp.concatenate([z, sub_keys[:, 1]], axis=2)
    return jnp.concatenate([top, bot], axis=1).astype(BF16)


def kernel(x, c, w_ada, b_ada, norm1_w, w_in, cmp_pe_k, cmp_pe_v, cmp_w1_k, cmp_w2_k, cmp_w1_v, cmp_w2_v,
           nsa_norm_w, conv_w, conv_b, dt_bias, a_log, d_skip, ssd_norm_w, w_out, norm2_w,
           peer_wq, peer_qnorm_w, peer_sub_keys, peer_down, peer_up, final_norm_w):
    bsz, s, d = x.shape
    assert d == D_MODEL and s % NSA_TK == 0 and s // SEL_BLOCK <= 128 and w_ada.shape[0] == 1
    lyr = 0
    ts = 512

    mod = _mod_call(c, w_ada[lyr], b_ada[lyr]).reshape(bsz, 6, d)
    cos_t, sin_t = _rope_tables(s)
    (qp, qr, ksel, kwin, vsel, vwin, kcmp, vcmp, gates, z, xbc, dtr) = _inproj_call(
        x, mod, norm1_w[lyr].reshape(1, d), _pack_w_in(w_in[lyr]), cos_t, sin_t, ts)

    def w1_groups(w1):
        w = w1.reshape(2, CMP_STRIDE, HEAD_DIM, CMP_HIDDEN)
        z = jnp.zeros_like(w)
        per_g = [jnp.concatenate([w, z], axis=2), jnp.concatenate([z, w], axis=2)]
        return jnp.stack(per_g, axis=1).reshape(2, NSA_GROUPS, CMP_STRIDE * 2 * HEAD_DIM, CMP_HIDDEN).astype(BF16)

    kc, vc = _compress_call(
        kcmp, vcmp, jnp.tile(cmp_pe_k[lyr], (1, NSA_GROUPS)), jnp.tile(cmp_pe_v[lyr], (1, NSA_GROUPS)),
        w1_groups(cmp_w1_k[lyr]), cmp_w2_k[lyr].astype(BF16), w1_groups(cmp_w1_v[lyr]), cmp_w2_v[lyr].astype(BF16))

    o_nsa = _nsa_call(qp, qr, kc, vc, ksel, vsel, kwin, vwin, gates, _overlap_t(s))

    dtt = jnp.transpose(dtr[:, :, :SSD_HEADS], (0, 2, 1))
    o_ssd = _ssd_call(xbc, z, dtr, dtt, conv_w[lyr], conv_b[lyr], dt_bias[lyr], a_log[lyr], d_skip[lyr],
                      ssd_norm_w[lyr])

    x1, h2, sc = _outproj_call(
        o_nsa, o_ssd, x, mod, nsa_norm_w[lyr].reshape(1, -1), norm2_w[lyr].reshape(1, d),
        w_out[lyr].astype(BF16), peer_wq[lyr].astype(BF16), peer_qnorm_w[lyr].reshape(1, -1),
        _blockdiag_keys(peer_sub_keys[lyr]), ts)

    cnt, e1, r2, e2 = _peersel_call(sc)

    out = _peer_call(h2.reshape(bsz * s, d), cnt, e1, r2, e2, peer_down[lyr].astype(BF16),
                     jnp.transpose(peer_up[lyr]).astype(BF16), x1.reshape(bsz * s, d), mod,
                     final_norm_w.reshape(1, d), s)
    return out.reshape(bsz, s, d)
```

```python
import functools
import math

import numpy as np
import jax
import jax.numpy as jnp
from jax import lax
from jax.experimental import pallas as pl
from jax.experimental.pallas import tpu as pltpu

F32 = jnp.float32
BF16 = jnp.bfloat16
HIGHEST = lax.Precision.HIGHEST

D_MODEL = 1024
NSA_WIDTH = 512
SSD_WIDTH = 512
HEAD_DIM = 64
NSA_HEADS = 8
NSA_GROUPS = 2
NSA_HPG = 4
N_BRANCH = 3
CMP_BLOCK = 32
CMP_STRIDE = 16
CMP_HIDDEN = 256
SEL_BLOCK = 64
SEL_TOPN = 16
WINDOW = 512
ROPE_THETA = 500000.0
ROPE_DIM = 16
SSD_HEADS = 8
SSD_GROUPS = 2
SSD_STATE = 128
SSD_CONV = 4
SSD_CHUNK = 128
PEER_HEADS = 8
PEER_NKEYS = 128
PEER_QDIM = 256
PEER_TOPK = 16
NORM_EPS = 1e-6
NEG_INF = -1e30
LOG2E = math.log2(math.e)
FORCE_BONUS = 1e4

LANES = 128
VMEM_LIMIT = 56 * 1024 * 1024

C_Q, C_QSW, C_KSEL, C_KSELSW, C_KWIN, C_KWINSW = 0, 512, 1024, 1152, 1280, 1408
C_VSEL, C_VWIN, C_KCMP, C_VCMP, C_GATE, C_Z, C_XBC, C_DT, C_END = 1536, 1664, 1792, 1920, 2048, 2304, 2816, 3840, 3968


def _gelu_tanh(x):
    c = math.sqrt(2.0 / math.pi)
    return 0.5 * x * (1.0 + jnp.tanh(c * (x + 0.044715 * (x * x * x))))


def _nt(a, b):
    return lax.dot_general(a, b, (((1,), (1,)), ((), ())), preferred_element_type=F32)


def _cparams(sem):
    return pltpu.CompilerParams(dimension_semantics=sem, vmem_limit_bytes=VMEM_LIMIT)


def _mod_kernel(c_ref, w_ref, b_ref, o_ref):
    o_ref[...] = jnp.dot(c_ref[...], w_ref[...], preferred_element_type=F32, precision=HIGHEST) + b_ref[...]


def _mod_call(c, w_ada, b_ada):
    bsz = c.shape[0]
    n = w_ada.shape[1]
    return pl.pallas_call(
        _mod_kernel,
        grid=(n // D_MODEL,),
        in_specs=[pl.BlockSpec((bsz, D_MODEL), lambda j: (0, 0)),
                  pl.BlockSpec((D_MODEL, D_MODEL), lambda j: (0, j)),
                  pl.BlockSpec((1, D_MODEL), lambda j: (0, j))],
        out_specs=pl.BlockSpec((bsz, D_MODEL), lambda j: (0, j)),
        out_shape=jax.ShapeDtypeStruct((bsz, n), F32),
        compiler_params=_cparams(("arbitrary",)),
        name="mod",
    )(c, w_ada, b_ada.reshape(1, n))


def _inproj_kernel(x_ref, mod_ref, nw_ref, w_ref, cos_ref, sin_ref,
                   qp_ref, qr_ref, ksel_ref, kwin_ref, vsel_ref, vwin_ref,
                   kcmp_ref, vcmp_ref, gate_ref, z_ref, xbc_ref, dt_ref):
    x = x_ref[0]
    ms = jnp.mean(x * x, axis=-1, keepdims=True)
    y = x * lax.rsqrt(ms + NORM_EPS) * nw_ref[...]
    h = (y * (1.0 + mod_ref[0, 1:2, :]) + mod_ref[0, 0:1, :]).astype(BF16)

    def proj(lo, hi):
        return jnp.dot(h, w_ref[:, lo:hi], preferred_element_type=F32)

    cos = cos_ref[...]
    sin = sin_ref[...]
    scale = HEAD_DIM ** -0.5 * LOG2E
    qa = proj(C_Q, C_KSEL)
    kv = proj(C_KSEL, C_GATE)
    rest = proj(C_GATE, C_END)
    q = qa[:, :512]
    qsw = qa[:, 512:]
    qp_ref[0] = (q * scale).astype(BF16)
    cos4 = jnp.concatenate([cos] * 4, axis=1)
    sin4 = jnp.concatenate([sin] * 4, axis=1)
    qr_ref[0] = ((q * cos4 + qsw * sin4) * scale).astype(BF16)

    part = lambda c: kv[:, c - C_KSEL:c - C_KSEL + 128]
    ks = part(C_KSEL) * cos + part(C_KSELSW) * sin
    kw = part(C_KWIN) * cos + part(C_KWINSW) * sin
    vs = part(C_VSEL)
    vw = part(C_VWIN)
    kwin_ref[0, 0] = kw[:, :HEAD_DIM].astype(BF16)
    kwin_ref[0, 1] = kw[:, HEAD_DIM:].astype(BF16)
    lane = lax.broadcasted_iota(jnp.int32, vs.shape, 1)
    blk = jnp.right_shift(pl.program_id(1) * ks.shape[0] + lax.broadcasted_iota(jnp.int32, ks.shape, 0), 6)
    hot_lo = jnp.where(lane - HEAD_DIM == blk, 1.0, 0.0)
    hot_hi = jnp.where(lane + HEAD_DIM == blk, 1.0, 0.0).astype(BF16)
    ksel_ref[0, 0, :, 0:128] = jnp.where(lane < HEAD_DIM, ks, hot_lo).astype(BF16)
    ksel_ref[0, 1, :, 0:128] = jnp.where(lane < HEAD_DIM, pltpu.roll(ks, HEAD_DIM, axis=1), hot_lo).astype(BF16)
    ksel_ref[0, 0, :, 128:256] = hot_hi
    ksel_ref[0, 1, :, 128:256] = hot_hi
    ones_col = jnp.where(lane == HEAD_DIM, 1.0, 0.0)
    for ref, val in ((vsel_ref, vs), (vwin_ref, vw)):
        ref[0, 0] = jnp.where(lane < HEAD_DIM, val, ones_col).astype(BF16)
        ref[0, 1] = jnp.where(lane < HEAD_DIM, pltpu.roll(val, HEAD_DIM, axis=1), ones_col).astype(BF16)
    kcmp_ref[0] = part(C_KCMP)
    vcmp_ref[0] = part(C_VCMP)
    sg = 1.0 / (1.0 + jnp.exp(-rest[:, :C_Z - C_GATE]))
    gate_ref[0, 0] = sg[:, :128]
    gate_ref[0, 1] = sg[:, 128:]
    z_ref[0] = rest[:, C_Z - C_GATE:C_XBC - C_GATE]
    xbc_ref[0] = rest[:, C_XBC - C_GATE:C_DT - C_GATE]
    dt_ref[0] = rest[:, C_DT - C_GATE:]


def _inproj_call(x, mod, norm_w, w_ext, cos_t, sin_t, ts):
    bsz, s, d = x.shape
    grid = (bsz, s // ts)
    tok = lambda w: pl.BlockSpec((1, ts, w), lambda b, i: (b, i, 0))
    hm = pl.BlockSpec((1, NSA_GROUPS, ts, HEAD_DIM), lambda b, i: (b, 0, i, 0))
    hv = pl.BlockSpec((1, NSA_GROUPS, ts, 128), lambda b, i: (b, 0, i, 0))
    hk = pl.BlockSpec((1, NSA_GROUPS, ts, 256), lambda b, i: (b, 0, i, 0))
    sd = jax.ShapeDtypeStruct
    out_shape = [sd((bsz, s, 512), BF16), sd((bsz, s, 512), BF16)] + \
                [sd((bsz, NSA_GROUPS, s, 256), BF16), sd((bsz, NSA_GROUPS, s, HEAD_DIM), BF16)] + \
                [sd((bsz, NSA_GROUPS, s, 128), BF16)] * 2 + \
                [sd((bsz, s, 128), F32), sd((bsz, s, 128), F32),
                 sd((bsz, NSA_GROUPS, s, 128), F32),
                 sd((bsz, s, 512), F32), sd((bsz, s, 1024), F32), sd((bsz, s, 128), F32)]
    out_specs = [tok(512), tok(512), hk, hm, hv, hv, tok(128), tok(128),
                 pl.BlockSpec((1, NSA_GROUPS, ts, 128), lambda b, i: (b, 0, i, 0)),
                 tok(512), tok(1024), tok(128)]
    return pl.pallas_call(
        _inproj_kernel,
        grid=grid,
        in_specs=[tok(d),
                  pl.BlockSpec((1, 6, d), lambda b, i: (b, 0, 0)),
                  pl.BlockSpec((1, d), lambda b, i: (0, 0)),
                  pl.BlockSpec((d, C_END), lambda b, i: (0, 0)),
                  pl.BlockSpec((ts, 128), lambda b, i: (i, 0)),
                  pl.BlockSpec((ts, 128), lambda b, i: (i, 0))],
        out_specs=out_specs,
        out_shape=out_shape,
        compiler_params=_cparams(("parallel", "parallel")),
        name="inproj",
    )(x, mod, norm_w, w_ext, cos_t, sin_t)


def _compress_kernel(k_ref, v_ref, pek_ref, pev_ref, w1k_ref, w2k_ref, w1v_ref, w2v_ref,
                     kc_ref, vc_ref, hb_ref):
    nch = k_ref.shape[1] // CMP_STRIDE

    def one(x_ref, pe_ref, w1_ref, w2_ref, o_ref):
        xs = [x_ref[0, pl.ds(l, nch, stride=CMP_STRIDE), :] for l in range(CMP_STRIDE)]
        top = jnp.concatenate([(xs[l] + pe_ref[l:l + 1, :]).astype(BF16) for l in range(CMP_STRIDE)], axis=1)
        bot = jnp.concatenate([(xs[l] + pe_ref[CMP_STRIDE + l:CMP_STRIDE + l + 1, :]).astype(BF16)
                               for l in range(CMP_STRIDE)], axis=1)
        htop = jnp.dot(top, w1_ref[0, 0], preferred_element_type=F32)
        hb_ref[0:nch, :] = jnp.dot(bot, w1_ref[1, 0], preferred_element_type=F32)
        hb_ref[nch:nch + 8, :] = jnp.zeros((8, CMP_HIDDEN), F32)
        pre = htop + hb_ref[pl.ds(1, nch), :]
        out = jnp.dot(_gelu_tanh(pre).astype(BF16), w2_ref[...], preferred_element_type=F32)
        row = lax.broadcasted_iota(jnp.int32, out.shape, 0)
        o_ref[0, 0] = jnp.where(row < nch - 1, out, 0.0).astype(BF16)

    one(k_ref, pek_ref, w1k_ref, w2k_ref, kc_ref)
    one(v_ref, pev_ref, w1v_ref, w2v_ref, vc_ref)


def _compress_call(kcmp, vcmp, pek, pev, w1k, w2k, w1v, w2v):
    bsz, s, cw = kcmp.shape
    nch = s // CMP_STRIDE
    xs = pl.BlockSpec((1, s, cw), lambda b, gi: (b, 0, 0))
    full = lambda a: pl.BlockSpec(a.shape, lambda b, gi: (0,) * a.ndim)
    w1s = pl.BlockSpec((2, 1) + w1k.shape[2:], lambda b, gi: (0, gi, 0, 0))
    os_ = pl.BlockSpec((1, 1, nch, HEAD_DIM), lambda b, gi: (b, gi, 0, 0))
    sd = jax.ShapeDtypeStruct((bsz, NSA_GROUPS, nch, HEAD_DIM), BF16)
    return pl.pallas_call(
        _compress_kernel,
        grid=(bsz, NSA_GROUPS),
        in_specs=[xs, xs, full(pek), full(pev), w1s, full(w2k), w1s, full(w2v)],
        out_specs=[os_, os_],
        out_shape=[sd, sd],
        scratch_shapes=[pltpu.VMEM((nch + 8, CMP_HIDDEN), F32)],
        compiler_params=_cparams(("parallel", "parallel")),
        name="compress",
    )(kcmp, vcmp, pek, pev, w1k, w2k, w1v, w2v)


NSA_TQ = 256
NSA_TK = 1024


def _nsa_kernel(qp_ref, qr_ref, kc_ref, vc_ref, ks_ref, vs_ref, kw_ref, vw_ref, gate_ref, ovt_ref, o_ref):
    tq = NSA_TQ
    tk = NSA_TK
    rows = NSA_HPG * tq
    qt = pl.program_id(2)
    q0 = qt * tq
    ncmp = kc_ref.shape[2]
    nsel = ovt_ref.shape[0]
    stack = lambda a: jnp.concatenate([a] * NSA_HPG, axis=0)
    qp = jnp.concatenate([qp_ref[0, :, r * HEAD_DIM:(r + 1) * HEAD_DIM] for r in range(NSA_HPG)], axis=0)
    qr = jnp.concatenate([qr_ref[0, :, r * HEAD_DIM:(r + 1) * HEAD_DIM] for r in range(NSA_HPG)], axis=0)

    t_row = q0 + lax.broadcasted_iota(jnp.int32, (tq, ncmp), 0)
    cend = lax.broadcasted_iota(jnp.int32, (tq, ncmp), 1) * CMP_STRIDE + (CMP_BLOCK - 1)
    cbias = stack(jnp.where(cend <= t_row, 0.0, NEG_INF))
    s = jnp.where(cbias == 0.0, _nt(qp, kc_ref[0, 0]), NEG_INF)
    e = jnp.exp2(s - jnp.max(s, axis=-1, keepdims=True))
    p = jnp.where(cbias == 0.0, e / jnp.sum(e, axis=-1, keepdims=True), 0.0)
    o_c = jnp.dot(p.astype(BF16), vc_ref[0, 0], preferred_element_type=F32)

    psum = p[0:tq] + p[tq:2 * tq] + p[2 * tq:3 * tq] + p[3 * tq:4 * tq]
    p_hi = psum.astype(BF16)
    p_lo = (psum - p_hi.astype(F32)).astype(BF16)
    ovt = ovt_ref[...]
    imp = _nt(ovt, p_hi) + _nt(ovt, p_lo)
    jblk = lax.broadcasted_iota(jnp.int32, (nsel, tq), 0)
    tt = q0 + lax.broadcasted_iota(jnp.int32, (nsel, tq), 1)
    blk_t = jnp.right_shift(tt, 6)
    forced = (jblk == 0) | (jblk == blk_t) | (jblk == blk_t - 1)
    xs = jnp.where(jblk * SEL_BLOCK <= tt, imp + jnp.where(forced, FORCE_BONUS, 0.0), NEG_INF)
    drop_t = jnp.full((nsel, tq), NEG_INF, F32)
    jblk_f = jblk.astype(F32)
    for _ in range(SEL_TOPN):
        mx = jnp.max(xs, axis=0, keepdims=True)
        idx = jnp.min(jnp.where(xs == mx, jblk_f, float(nsel)), axis=0, keepdims=True)
        hit = jblk_f == idx
        drop_t = jnp.where(hit, 0.0, drop_t)
        xs = jnp.where(hit, -jnp.inf, xs)
    drop = jnp.transpose(drop_t).astype(BF16)

    wk = WINDOW + tq
    wstart = pl.multiple_of(jnp.maximum(q0 - WINDOW, 0), tq)
    tq_w = q0 + lax.broadcasted_iota(jnp.int32, (tq, wk), 0)
    kpos_w = wstart + lax.broadcasted_iota(jnp.int32, (tq, wk), 1)
    bias_w = jnp.where((kpos_w <= tq_w) & (tq_w - kpos_w < WINDOW), 0.0, NEG_INF)
    sw = _nt(qr, kw_ref[0, 0, pl.ds(wstart, wk), :]) + stack(bias_w)
    ew = jnp.exp2(sw - jnp.max(sw, axis=-1, keepdims=True))
    pv_w = jnp.dot(ew.astype(BF16), vw_ref[0, 0, pl.ds(wstart, wk), :], preferred_element_type=F32)
    o_w = pv_w[:, :HEAD_DIM] / pv_w[:, HEAD_DIM:HEAD_DIM + 1]

    if nsel < 128:
        drop = jnp.concatenate([drop, jnp.zeros((tq, 128 - nsel), BF16)], axis=1)
    q_aug = jnp.concatenate([qr, stack(drop), jnp.zeros((rows, HEAD_DIM), BF16)], axis=1)

    def sel_tile(kt, carry, diagonal):
        m_i, acc = carry
        kbase = pl.multiple_of(kt * tk, tk)
        sc = _nt(q_aug, ks_ref[0, 0, pl.ds(kbase, tk), :])
        if diagonal:
            kpos = kbase + lax.broadcasted_iota(jnp.int32, (tq, tk), 1)
            tpos = q0 + lax.broadcasted_iota(jnp.int32, (tq, tk), 0)
            sc = sc + stack(jnp.where(kpos <= tpos, 0.0, NEG_INF))
        m_new = jnp.maximum(m_i, jnp.max(sc, axis=-1, keepdims=True))
        pe = jnp.exp2(sc - m_new)
        acc_new = jnp.exp2(m_i - m_new) * acc + jnp.dot(pe.astype(BF16), vs_ref[0, 0, pl.ds(kbase, tk), :],
                                                        preferred_element_type=F32)
        return m_new, acc_new

    init = (jnp.full((rows, 1), NEG_INF, F32), jnp.zeros((rows, 2 * HEAD_DIM), F32))
    last_kt = (q0 + tq - 1) // tk
    _, acc_s = sel_tile(last_kt, lax.fori_loop(0, last_kt, lambda kt, c: sel_tile(kt, c, False), init), True)
    o_s = acc_s[:, :HEAD_DIM] / acc_s[:, HEAD_DIM:HEAD_DIM + 1]

    gates = gate_ref[0, 0]
    for r in range(NSA_HPG):
        sl = slice(r * tq, (r + 1) * tq)
        o_r = (gates[:, 3 * r:3 * r + 1] * o_c[sl] + gates[:, 3 * r + 1:3 * r + 2] * o_s[sl]
               + gates[:, 3 * r + 2:3 * r + 3] * o_w[sl])
        o_ref[0, :, r * HEAD_DIM:(r + 1) * HEAD_DIM] = o_r


def _nsa_call(qp, qr, kc, vc, ks, vs, kw, vw, gates, ovt):
    bsz, s, _ = qp.shape
    tq = NSA_TQ
    ncmp = kc.shape[2]
    gw = NSA_HPG * HEAD_DIM
    qspec = pl.BlockSpec((1, tq, gw), lambda b, g, i: (b, i, g))
    cspec = pl.BlockSpec((1, 1, ncmp, HEAD_DIM), lambda b, g, i: (b, g, 0, 0))
    kspec = pl.BlockSpec((1, 1, s, HEAD_DIM), lambda b, g, i: (b, g, 0, 0))
    kaspec = pl.BlockSpec((1, 1, s, 4 * HEAD_DIM), lambda b, g, i: (b, g, 0, 0))
    vspec = pl.BlockSpec((1, 1, s, 2 * HEAD_DIM), lambda b, g, i: (b, g, 0, 0))
    return pl.pallas_call(
        _nsa_kernel,
        grid=(bsz, NSA_GROUPS, s // tq),
        in_specs=[qspec, qspec, cspec, cspec, kaspec, vspec, kspec, vspec,
                  pl.BlockSpec((1, 1, tq, 128), lambda b, g, i: (b, g, i, 0)),
                  pl.BlockSpec(ovt.shape, lambda b, g, i: (0, 0))],
        out_specs=pl.BlockSpec((1, tq, gw), lambda b, g, i: (b, i, g)),
        out_shape=jax.ShapeDtypeStruct((bsz, s, NSA_WIDTH), F32),
        compiler_params=_cparams(("parallel", "parallel", "arbitrary")),
        name="nsa",
    )(qp, qr, kc, vc, ks, vs, kw, vw, gates, ovt)


def _softplus(x):
    return jnp.maximum(x, 0.0) + jnp.log1p(jnp.exp(-jnp.abs(x)))


def _ssd_kernel(xbc_ref, z_ref, dt_ref, dtt_ref, cw_ref, cb_ref, dtb_ref, dtbt_ref, al_ref, alt_ref,
                dsk_ref, nw_ref, o_ref, tail_ref, xp_ref, st_ref, y_ref):
    L = SSD_CHUNK
    P = HEAD_DIM
    N = SSD_STATE
    c = pl.program_id(1)

    @pl.when(c == 0)
    def _():
        tail_ref[...] = jnp.zeros(tail_ref.shape, F32)
        st_ref[...] = jnp.zeros(st_ref.shape, F32)

    xin = xbc_ref[0]
    xp_ref[0:8, :] = tail_ref[...]
    xp_ref[8:8 + L, :] = xin
    tail_ref[...] = xin[L - 8:L, :]
    conv = cb_ref[...] + jnp.zeros_like(xin)
    for k in range(SSD_CONV):
        conv = conv + cw_ref[k:k + 1, :] * xp_ref[pl.ds(8 - (SSD_CONV - 1) + k, L), :]
    u = conv * (1.0 / (1.0 + jnp.exp(-conv)))
    xs = u[:, :SSD_WIDTH]

    dt_c = _softplus(dt_ref[0] + dtb_ref[...])
    dt_r = _softplus(dtt_ref[0] + dtbt_ref[...])
    a_c = -jnp.exp(al_ref[...])
    a_r = -jnp.exp(alt_ref[...])
    li = lax.broadcasted_iota(jnp.int32, (L, L), 0)
    si = lax.broadcasted_iota(jnp.int32, (L, L), 1)
    causal = li >= si
    tri = jnp.where(causal, 1.0, 0.0)
    tri_t = jnp.where(li <= si, 1.0, 0.0)
    acs_c = jnp.dot(tri, dt_c * a_c, preferred_element_type=F32, precision=HIGHEST)
    acs_r = jnp.dot(dt_r * a_r, tri_t, preferred_element_type=F32, precision=HIGHEST)

    for g in range(SSD_GROUPS):
        bm = u[:, SSD_WIDTH + g * N:SSD_WIDTH + (g + 1) * N]
        cm = u[:, SSD_WIDTH + SSD_GROUPS * N + g * N:SSD_WIDTH + SSD_GROUPS * N + (g + 1) * N]
        bm_b = bm.astype(BF16)
        cm_b = cm.astype(BF16)
        cb = _nt(cm_b, bm_b)
        bm_t = jnp.transpose(bm)
        for r in range(SSD_HEADS // SSD_GROUPS):
            hh = g * (SSD_HEADS // SSD_GROUPS) + r
            col = acs_c[:, hh:hh + 1]
            row = acs_r[hh:hh + 1, :]
            last = acs_r[hh:hh + 1, L - 1:L]
            lm = jnp.exp(jnp.where(causal, col - row, NEG_INF))
            x_h = xs[:, hh * P:(hh + 1) * P]
            xd = x_h * dt_c[:, hh:hh + 1]
            y_d = jnp.dot((cb * lm).astype(BF16), xd.astype(BF16), preferred_element_type=F32)
            prev = st_ref[hh]
            y_o = jnp.dot(cm_b, prev.astype(BF16), preferred_element_type=F32) * jnp.exp(col)
            dec = jnp.exp(last - row)
            st_new = jnp.dot((bm_t * dec).astype(BF16), xd.astype(BF16), preferred_element_type=F32)
            st_ref[hh] = prev * jnp.exp(last) + st_new
            y_ref[:, hh * P:(hh + 1) * P] = y_d + y_o + x_h * dsk_ref[0:1, hh:hh + 1]

    zz = z_ref[0]
    y = y_ref[...] * (zz * (1.0 / (1.0 + jnp.exp(-zz))))
    ms = jnp.mean(y * y, axis=-1, keepdims=True)
    o_ref[0] = (y * lax.rsqrt(ms + NORM_EPS) * nw_ref[...]).astype(o_ref.dtype)


def _ssd_call(xbc, z, dt, dtt, conv_w, conv_b, dt_bias, a_log, d_skip, norm_w):
    bsz, s, cch = xbc.shape
    L = SSD_CHUNK
    pad = lambda v: jnp.pad(v.reshape(1, -1), ((0, 0), (0, 128 - v.size)))
    full = lambda a: pl.BlockSpec(a.shape, lambda b, i: (0,) * a.ndim)
    args = (xbc, z, dt, dtt, conv_w, conv_b.reshape(1, -1), pad(dt_bias), dt_bias.reshape(-1, 1),
            pad(a_log), a_log.reshape(-1, 1), pad(d_skip), norm_w.reshape(1, -1))
    in_specs = [pl.BlockSpec((1, L, cch), lambda b, i: (b, i, 0)),
                pl.BlockSpec((1, L, SSD_WIDTH), lambda b, i: (b, i, 0)),
                pl.BlockSpec((1, L, 128), lambda b, i: (b, i, 0)),
                pl.BlockSpec((1, SSD_HEADS, L), lambda b, i: (b, 0, i))] + [full(a) for a in args[4:]]
    return pl.pallas_call(
        _ssd_kernel,
        grid=(bsz, s // L),
        in_specs=in_specs,
        out_specs=pl.BlockSpec((1, L, SSD_WIDTH), lambda b, i: (b, i, 0)),
        out_shape=jax.ShapeDtypeStruct((bsz, s, SSD_WIDTH), BF16),
        scratch_shapes=[pltpu.VMEM((8, cch), F32), pltpu.VMEM((8 + L, cch), F32),
                        pltpu.VMEM((SSD_HEADS, SSD_STATE, HEAD_DIM), F32), pltpu.VMEM((L, SSD_WIDTH), F32)],
        compiler_params=_cparams(("parallel", "arbitrary")),
        name="ssd",
    )(*args)


def _outproj_kernel(on_ref, os_ref, x_ref, mod_ref, nnw_ref, n2w_ref, wo_ref, wq_ref, qnw_ref, sk_ref,
                    x1_ref, h2_ref, sc_ref):
    o = on_ref[0]
    ms = jnp.mean(o * o, axis=-1, keepdims=True)
    on = (o * lax.rsqrt(ms + NORM_EPS) * nnw_ref[...]).astype(BF16)
    mix = jnp.dot(jnp.concatenate([on, os_ref[0]], axis=1), wo_ref[...], preferred_element_type=F32)
    x1 = x_ref[0] + mod_ref[0, 2:3, :] * mix
    x1_ref[0] = x1
    ms2 = jnp.mean(x1 * x1, axis=-1, keepdims=True)
    h2 = ((x1 * lax.rsqrt(ms2 + NORM_EPS) * n2w_ref[...]) * (1.0 + mod_ref[0, 4:5, :]) + mod_ref[0, 3:4, :]).astype(BF16)
    h2_ref[0] = h2
    qall = jnp.dot(h2, wq_ref[...], preferred_element_type=F32)
    for hh in range(PEER_HEADS):
        qh = qall[:, hh * PEER_QDIM:(hh + 1) * PEER_QDIM]
        qn = (qh * lax.rsqrt(jnp.mean(qh * qh, axis=-1, keepdims=True) + NORM_EPS) * qnw_ref[...]).astype(BF16)
        both = _nt(sk_ref[hh], qn)
        sc_ref[2 * hh] = both[:PEER_NKEYS]
        sc_ref[2 * hh + 1] = both[PEER_NKEYS:]


def _outproj_call(o_nsa, o_ssd, x, mod, nsa_nw, n2w, w_out, wq, qnw, sub_keys, ts):
    bsz, s, d = x.shape
    nblk = s // ts
    tok = lambda w: pl.BlockSpec((1, ts, w), lambda b, i: (b, i, 0))
    full = lambda a: pl.BlockSpec(a.shape, lambda b, i: (0,) * a.ndim)
    return pl.pallas_call(
        _outproj_kernel,
        grid=(bsz, nblk),
        in_specs=[tok(NSA_WIDTH), tok(SSD_WIDTH), tok(d), pl.BlockSpec((1, 6, d), lambda b, i: (b, 0, 0)),
                  full(nsa_nw), full(n2w), full(w_out), full(wq), full(qnw), full(sub_keys)],
        out_specs=[tok(d), tok(d),
                   pl.BlockSpec((2 * PEER_HEADS, PEER_NKEYS, ts), lambda b, i: (0, 0, b * nblk + i))],
        out_shape=[jax.ShapeDtypeStruct((bsz, s, d), F32), jax.ShapeDtypeStruct((bsz, s, d), BF16),
                   jax.ShapeDtypeStruct((2 * PEER_HEADS, PEER_NKEYS, bsz * s), F32)],
        compiler_params=_cparams(("parallel", "parallel")),
        name="outproj",
    )(o_nsa, o_ssd, x, mod, nsa_nw, n2w, w_out, wq, qnw, sub_keys)


PEER_TT = 128
PEER_HG = 2


def _peersel_kernel(sc_ref, c_ref, e1_ref, r2_ref, e2_ref):
    nk = PEER_NKEYS
    tt = PEER_TT
    K = PEER_TOPK
    kidx = lax.broadcasted_iota(jnp.int32, (nk, tt), 0).astype(F32)
    i16 = lax.broadcasted_iota(jnp.int32, (K, tt), 0).astype(F32)

    def topk_sorted(x, exact_ties):
        rank = jnp.full((nk, tt), float(K), F32)
        vals = jnp.zeros((K, tt), F32)
        for j in range(K):
            mx = jnp.max(x, axis=0, keepdims=True)
            hit = x == mx
            if exact_ties:
                hit = kidx == jnp.min(jnp.where(hit, kidx, float(nk)), axis=0, keepdims=True)
            rank = jnp.where(hit, float(j), rank)
            x = jnp.where(hit, -jnp.inf, x)
            vals = jnp.where(i16 == float(j), mx, vals)
        taken = jnp.sum(jnp.where(x == -jnp.inf, 1.0, 0.0), axis=0, keepdims=True)
        return rank, vals, taken

    def first_stage(hh0, nh):
        ss = [sc_ref[2 * hh0 + i] for i in range(2 * nh)]
        fast = [topk_sorted(s, False) for s in ss]
        most = fast[0][2]
        for f in fast[1:]:
            most = jnp.maximum(most, f[2])
        res = lax.cond(
            jnp.max(most) > float(K),
            lambda: tuple(a for s in ss for a in topk_sorted(s, True)[:2]),
            lambda: tuple(a for f in fast for a in f[:2]))
        return [(ss[2 * i], ss[2 * i + 1]) + tuple(res[4 * i:4 * i + 4]) for i in range(nh)]

    def second_stage(hh, s1, s2, r1, v1a, r2, v2a):
        v1 = [v1a[i:i + 1, :] for i in range(K)]
        v2 = [v2a[i:i + 1, :] for i in range(K)]
        cmax = v1[0] + v2[0]
        n = jnp.zeros((K, tt), F32)
        f = v1a + v2[0]
        zsum = jnp.zeros((1, tt), F32)
        for _ in range(K):
            mx = jnp.max(f, axis=0, keepdims=True)
            iw = jnp.min(jnp.where(f == mx, i16, float(K)), axis=0, keepdims=True)
            hit = i16 == iw
            n = n + jnp.where(hit, 1.0, 0.0)
            zsum = zsum + jnp.exp(mx - cmax)
            nstar = jnp.sum(jnp.where(hit, n, 0.0), axis=0, keepdims=True)
            v2n = jnp.sum(jnp.where(i16 == nstar, v2a, 0.0), axis=0, keepdims=True)
            f = jnp.where(hit, v1a + v2n, f)
        cnt = jnp.zeros((nk, tt), F32)
        for i in range(K):
            cnt = jnp.where(r1 == float(i), n[i:i + 1, :], cnt)
        c_ref[hh] = cnt
        e1_ref[hh] = jnp.exp(s1 - v1[0]) / zsum
        r2_ref[hh] = r2.astype(BF16)
        e2_ref[hh] = jnp.exp(s2 - v2[0]).astype(BF16)

    for hh0 in range(0, PEER_HEADS, PEER_HG):
        for i, args in enumerate(first_stage(hh0, PEER_HG)):
            second_stage(hh0 + i, *args)


def _peersel_call(sc):
    _, nk, t = sc.shape
    tt = PEER_TT
    ospec = pl.BlockSpec((PEER_HEADS, nk, tt), lambda i: (0, 0, i))
    sd = lambda dt: jax.ShapeDtypeStruct((PEER_HEADS, nk, t), dt)
    return pl.pallas_call(
        _peersel_kernel,
        grid=(t // tt,),
        in_specs=[pl.BlockSpec((2 * PEER_HEADS, nk, tt), lambda i: (0, 0, i))],
        out_specs=[ospec] * 4,
        out_shape=[sd(F32), sd(F32), sd(BF16), sd(BF16)],
        compiler_params=_cparams(("parallel",)),
        name="peersel",
    )(sc)


PEER_TB = 512
PEER_EC = 1024


def _peer_kernel(h2_ref, c_ref, e1_ref, r2_ref, e2_ref, down0_ref, downn_ref, upt_ref, x1_ref, mod_ref, fw_ref,
                 o_ref, acc_ref, act_ref, w_ref):
    j = pl.program_id(1)
    nj = pl.num_programs(1)
    nk = PEER_NKEYS
    na = PEER_EC // nk

    def build_gates(chunk):
        a0 = pl.multiple_of(chunk * na, na)

        def rows(ref, hh, ai):
            grp = ref[hh, pl.ds(a0, na), :]
            r16 = jnp.broadcast_to(grp[ai:ai + 1, :], (16, PEER_TB)).astype(BF16)
            return jnp.concatenate([r16] * (nk // 16), axis=0)

        for ai in range(na):
            w = None
            for hh in range(PEER_HEADS):
                term = jnp.where(r2_ref[hh] < rows(c_ref, hh, ai), e2_ref[hh], 0.0) * rows(e1_ref, hh, ai)
                w = term if w is None else w + term
            w_ref[ai * nk:(ai + 1) * nk, :] = w

    @pl.when(j == 0)
    def _():
        acc_ref[...] = jnp.zeros(acc_ref.shape, F32)
        act_ref[...] = _gelu_tanh(_nt(down0_ref[...], h2_ref[...]).astype(BF16))
        build_gates(0)

    wa = w_ref[...] * act_ref[...]
    acc_ref[...] += jnp.dot(upt_ref[...], wa, preferred_element_type=F32)
    act_ref[...] = _gelu_tanh(_nt(downn_ref[...], h2_ref[...]).astype(BF16))
    build_gates(jnp.minimum(j + 1, nj - 1))

    @pl.when(j == nj - 1)
    def _():
        y = jnp.transpose(acc_ref[...])
        x2 = x1_ref[...] + mod_ref[0, 5:6, :] * y
        ms = jnp.mean(x2 * x2, axis=-1, keepdims=True)
        o_ref[...] = x2 * lax.rsqrt(ms + NORM_EPS) * fw_ref[...]


def _peer_call(h2, cnt, e1, r2, e2, down, upt, x1, mod, fw, s):
    t, d = h2.shape
    ne = down.shape[0]
    tb, ec = PEER_TB, PEER_EC
    per_b = s // tb
    last = ne // ec - 1
    dspec = pl.BlockSpec((PEER_HEADS, PEER_NKEYS, tb), lambda i, j: (0, 0, i))
    return pl.pallas_call(
        _peer_kernel,
        grid=(t // tb, ne // ec),
        in_specs=[pl.BlockSpec((tb, d), lambda i, j: (i, 0)), dspec, dspec, dspec, dspec,
                  pl.BlockSpec((ec, d), lambda i, j: (0, 0)),
                  pl.BlockSpec((ec, d), lambda i, j: (jnp.minimum(j + 1, last), 0)),
                  pl.BlockSpec((d, ec), lambda i, j: (0, j)),
                  pl.BlockSpec((tb, d), lambda i, j: (i, 0)),
                  pl.BlockSpec((1, 6, d), lambda i, j: (i // per_b, 0, 0)),
                  pl.BlockSpec((1, d), lambda i, j: (0, 0))],
        out_specs=pl.BlockSpec((tb, d), lambda i, j: (i, 0)),
        out_shape=jax.ShapeDtypeStruct((t, d), F32),
        scratch_shapes=[pltpu.VMEM((d, tb), F32), pltpu.VMEM((ec, tb), BF16), pltpu.VMEM((ec, tb), BF16)],
        compiler_params=_cparams(("parallel", "arbitrary")),
        name="peer",
    )(h2, cnt, e1, r2, e2, down, down, upt, x1, mod, fw)


def _rope_tables(s):
    half = ROPE_DIM // 2
    inv_freq = ROPE_THETA ** (-jnp.arange(half, dtype=F32) / half)
    ang = jnp.arange(s).astype(F32)[:, None] * inv_freq[None, :]
    cos, sin = jnp.cos(ang), jnp.sin(ang)
    one = jnp.ones((s, HEAD_DIM - ROPE_DIM), F32)
    cos64 = jnp.concatenate([cos, cos, one], axis=1)
    sin64 = jnp.concatenate([-sin, sin, 0.0 * one], axis=1)
    return jnp.concatenate([cos64, cos64], axis=1), jnp.concatenate([sin64, sin64], axis=1)


def _swap_cols(w):
    d, n = w.shape
    wh = w.reshape(d, n // HEAD_DIM, HEAD_DIM)
    half = ROPE_DIM // 2
    sw = jnp.concatenate([wh[..., half:ROPE_DIM], wh[..., :half], jnp.zeros_like(wh[..., ROPE_DIM:])], axis=-1)
    return sw.reshape(d, n)


def _pack_w_in(w):
    o = np.cumsum((0, 512, 128, 128, 128, 128, 128, 128, 24, 512, 1024, 8))
    q, kc, vc, ksel, vsel, kwin, vwin, gl, z, xbc, dtr = (w[:, o[i]:o[i + 1]] for i in range(11))
    gl = gl.reshape(-1, NSA_GROUPS, NSA_HPG * N_BRANCH)
    gl = jnp.pad(gl, ((0, 0), (0, 0), (0, 128 - NSA_HPG * N_BRANCH))).reshape(-1, 256)
    dtr = jnp.pad(dtr, ((0, 0), (0, 128 - SSD_HEADS)))
    cols = [q, _swap_cols(q), ksel, _swap_cols(ksel), kwin, _swap_cols(kwin), vsel, vwin, kc, vc, gl, z, xbc, dtr]
    return jnp.concatenate(cols, axis=1).astype(BF16)


def _overlap_t(s):
    n_cmp_pad = s // CMP_STRIDE
    n_sel = s // SEL_BLOCK
    cs = np.arange(n_cmp_pad) * CMP_STRIDE
    ss = np.arange(n_sel) * SEL_BLOCK
    ov = np.maximum(np.minimum(cs[None, :] + CMP_BLOCK, ss[:, None] + SEL_BLOCK)
                    - np.maximum(cs[None, :], ss[:, None]), 0).astype(np.float32) / CMP_BLOCK
    ov[:, n_cmp_pad - 1] = 0.0
    return jnp.asarray(ov, BF16)


def _blockdiag_keys(sub_keys):
    h, two, n, half = sub_keys.shape
    z = jnp.zeros((h, n, half), sub_keys.dtype)
    top = jnp.concatenate([sub_keys[:, 0], z], axis=2)
    bot = jnp.concatenate([z, sub_keys[:, 1]], axis=2)
    return jnp.concatenate([top, bot], axis=1).astype(BF16)


def kernel(x, c, w_ada, b_ada, norm1_w, w_in, cmp_pe_k, cmp_pe_v, cmp_w1_k, cmp_w2_k, cmp_w1_v, cmp_w2_v,
           nsa_norm_w, conv_w, conv_b, dt_bias, a_log, d_skip, ssd_norm_w, w_out, norm2_w,
           peer_wq, peer_qnorm_w, peer_sub_keys, peer_down, peer_up, final_norm_w):
    bsz, s, d = x.shape
    assert d == D_MODEL and s % NSA_TK == 0 and s // SEL_BLOCK <= 128 and w_ada.shape[0] == 1
    lyr = 0
    ts = 512

    mod = _mod_call(c, w_ada[lyr], b_ada[lyr]).reshape(bsz, 6, d)
    cos_t, sin_t = _rope_tables(s)
    (qp, qr, ksel, kwin, vsel, vwin, kcmp, vcmp, gates, z, xbc, dtr) = _inproj_call(
        x, mod, norm1_w[lyr].reshape(1, d), _pack_w_in(w_in[lyr]), cos_t, sin_t, ts)

    def w1_groups(w1):
        w = w1.reshape(2, CMP_STRIDE, HEAD_DIM, CMP_HIDDEN)
        z = jnp.zeros_like(w)
        per_g = [jnp.concatenate([w, z], axis=2), jnp.concatenate([z, w], axis=2)]
        return jnp.stack(per_g, axis=1).reshape(2, NSA_GROUPS, CMP_STRIDE * 2 * HEAD_DIM, CMP_HIDDEN).astype(BF16)

    kc, vc = _compress_call(
        kcmp, vcmp, jnp.tile(cmp_pe_k[lyr], (1, NSA_GROUPS)), jnp.tile(cmp_pe_v[lyr], (1, NSA_GROUPS)),
        w1_groups(cmp_w1_k[lyr]), cmp_w2_k[lyr].astype(BF16), w1_groups(cmp_w1_v[lyr]), cmp_w2_v[lyr].astype(BF16))

    o_nsa = _nsa_call(qp, qr, kc, vc, ksel, vsel, kwin, vwin, gates, _overlap_t(s))

    dtt = jnp.transpose(dtr[:, :, :SSD_HEADS], (0, 2, 1))
    o_ssd = _ssd_call(xbc, z, dtr, dtt, conv_w[lyr], conv_b[lyr], dt_bias[lyr], a_log[lyr], d_skip[lyr],
                      ssd_norm_w[lyr])

    x1, h2, sc = _outproj_call(
        o_nsa, o_ssd, x, mod, nsa_norm_w[lyr].reshape(1, -1), norm2_w[lyr].reshape(1, d),
        w_out[lyr].astype(BF16), peer_wq[lyr].astype(BF16), peer_qnorm_w[lyr].reshape(1, -1),
        _blockdiag_keys(peer_sub_keys[lyr]), ts)

    cnt, e1, r2, e2 = _peersel_call(sc)

    out = _peer_call(h2.reshape(bsz * s, d), cnt, e1, r2, e2, peer_down[lyr].astype(BF16),
                     jnp.transpose(peer_up[lyr]).astype(BF16), x1.reshape(bsz * s, d), mod,
                     final_norm_w.reshape(1, d), s)
    return out.reshape(bsz, s, d)
```

```python
import functools
import math

import numpy as np
import jax
import jax.numpy as jnp
from jax import lax
from jax.experimental import pallas as pl
from jax.experimental.pallas import tpu as pltpu

F32 = jnp.float32
BF16 = jnp.bfloat16
HIGHEST = lax.Precision.HIGHEST

D_MODEL = 1024
NSA_WIDTH = 512
SSD_WIDTH = 512
HEAD_DIM = 64
NSA_HEADS = 8
NSA_GROUPS = 2
NSA_HPG = 4
N_BRANCH = 3
CMP_BLOCK = 32
CMP_STRIDE = 16
CMP_HIDDEN = 256
SEL_BLOCK = 64
SEL_TOPN = 16
WINDOW = 512
ROPE_THETA = 500000.0
ROPE_DIM = 16
SSD_HEADS = 8
SSD_GROUPS = 2
SSD_STATE = 128
SSD_CONV = 4
SSD_CHUNK = 128
PEER_HEADS = 8
PEER_NKEYS = 128
PEER_QDIM = 256
PEER_TOPK = 16
NORM_EPS = 1e-6
NEG_INF = -1e30
LOG2E = math.log2(math.e)
FORCE_BONUS = 1e4

LANES = 128
VMEM_LIMIT = 56 * 1024 * 1024

C_Q, C_QSW, C_KSEL, C_KSELSW, C_KWIN, C_KWINSW = 0, 512, 1024, 1152, 1280, 1408
C_VSEL, C_VWIN, C_KCMP, C_VCMP, C_GATE, C_Z, C_XBC, C_DT, C_END = 1536, 1664, 1792, 1920, 2048, 2304, 2816, 3840, 3968


def _gelu_tanh(x):
    c = math.sqrt(2.0 / math.pi)
    return 0.5 * x * (1.0 + jnp.tanh(c * (x + 0.044715 * (x * x * x))))


def _nt(a, b):
    return lax.dot_general(a, b, (((1,), (1,)), ((), ())), preferred_element_type=F32)


def _cparams(sem):
    return pltpu.CompilerParams(dimension_semantics=sem, vmem_limit_bytes=VMEM_LIMIT)


def _mod_kernel(c_ref, w_ref, b_ref, o_ref):
    o_ref[...] = jnp.dot(c_ref[...], w_ref[...], preferred_element_type=F32, precision=HIGHEST) + b_ref[...]


def _mod_call(c, w_ada, b_ada):
    bsz = c.shape[0]
    n = w_ada.shape[1]
    return pl.pallas_call(
        _mod_kernel,
        grid=(n // D_MODEL,),
        in_specs=[pl.BlockSpec((bsz, D_MODEL), lambda j: (0, 0)),
                  pl.BlockSpec((D_MODEL, D_MODEL), lambda j: (0, j)),
                  pl.BlockSpec((1, D_MODEL), lambda j: (0, j))],
        out_specs=pl.BlockSpec((bsz, D_MODEL), lambda j: (0, j)),
        out_shape=jax.ShapeDtypeStruct((bsz, n), F32),
        compiler_params=_cparams(("arbitrary",)),
        name="mod",
    )(c, w_ada, b_ada.reshape(1, n))


def _inproj_kernel(x_ref, mod_ref, nw_ref, w_ref, cos_ref, sin_ref,
                   qp_ref, qr_ref, ksel_ref, kwin_ref, vsel_ref, vwin_ref,
                   kcmp_ref, vcmp_ref, gate_ref, z_ref, xbc_ref, dt_ref):
    x = x_ref[0]
    ms = jnp.mean(x * x, axis=-1, keepdims=True)
    y = x * lax.rsqrt(ms + NORM_EPS) * nw_ref[...]
    h = (y * (1.0 + mod_ref[0, 1:2, :]) + mod_ref[0, 0:1, :]).astype(BF16)

    def proj(lo, hi):
        return jnp.dot(h, w_ref[:, lo:hi], preferred_element_type=F32)

    cos = cos_ref[...]
    sin = sin_ref[...]
    scale = HEAD_DIM ** -0.5 * LOG2E
    qa = proj(C_Q, C_KSEL)
    kv = proj(C_KSEL, C_GATE)
    rest = proj(C_GATE, C_END)
    q = qa[:, :512]
    qsw = qa[:, 512:]
    qp_ref[0] = (q * scale).astype(BF16)
    cos4 = jnp.concatenate([cos] * 4, axis=1)
    sin4 = jnp.concatenate([sin] * 4, axis=1)
    qr_ref[0] = ((q * cos4 + qsw * sin4) * scale).astype(BF16)

    part = lambda c: kv[:, c - C_KSEL:c - C_KSEL + 128]
    ks = part(C_KSEL) * cos + part(C_KSELSW) * sin
    kw = part(C_KWIN) * cos + part(C_KWINSW) * sin
    vs = part(C_VSEL)
    vw = part(C_VWIN)
    kwin_ref[0, 0] = kw[:, :HEAD_DIM].astype(BF16)
    kwin_ref[0, 1] = kw[:, HEAD_DIM:].astype(BF16)
    lane = lax.broadcasted_iota(jnp.int32, vs.shape, 1)
    blk = jnp.right_shift(pl.program_id(1) * ks.shape[0] + lax.broadcasted_iota(jnp.int32, ks.shape, 0), 6)
    hot_lo = jnp.where(lane - HEAD_DIM == blk, 1.0, 0.0)
    hot_hi = jnp.where(lane + HEAD_DIM == blk, 1.0, 0.0).astype(BF16)
    ksel_ref[0, 0, :, 0:128] = jnp.where(lane < HEAD_DIM, ks, hot_lo).astype(BF16)
    ksel_ref[0, 1, :, 0:128] = jnp.where(lane < HEAD_DIM, pltpu.roll(ks, HEAD_DIM, axis=1), hot_lo).astype(BF16)
    ksel_ref[0, 0, :, 128:256] = hot_hi
    ksel_ref[0, 1, :, 128:256] = hot_hi
    ones_col = jnp.where(lane == HEAD_DIM, 1.0, 0.0)
    for ref, val in ((vsel_ref, vs), (vwin_ref, vw)):
        ref[0, 0] = jnp.where(lane < HEAD_DIM, val, ones_col).astype(BF16)
        ref[0, 1] = jnp.where(lane < HEAD_DIM, pltpu.roll(val, HEAD_DIM, axis=1), ones_col).astype(BF16)
    kcmp_ref[0] = part(C_KCMP)
    vcmp_ref[0] = part(C_VCMP)
    sg = 1.0 / (1.0 + jnp.exp(-rest[:, :C_Z - C_GATE]))
    gate_ref[0, 0] = sg[:, :128]
    gate_ref[0, 1] = sg[:, 128:]
    z_ref[0] = rest[:, C_Z - C_GATE:C_XBC - C_GATE]
    xbc_ref[0] = rest[:, C_XBC - C_GATE:C_DT - C_GATE]
    dt_ref[0] = rest[:, C_DT - C_GATE:]


def _inproj_call(x, mod, norm_w, w_ext, cos_t, sin_t, ts):
    bsz, s, d = x.shape
    grid = (bsz, s // ts)
    tok = lambda w: pl.BlockSpec((1, ts, w), lambda b, i: (b, i, 0))
    hm = pl.BlockSpec((1, NSA_GROUPS, ts, HEAD_DIM), lambda b, i: (b, 0, i, 0))
    hv = pl.BlockSpec((1, NSA_GROUPS, ts, 128), lambda b, i: (b, 0, i, 0))
    hk = pl.BlockSpec((1, NSA_GROUPS, ts, 256), lambda b, i: (b, 0, i, 0))
    sd = jax.ShapeDtypeStruct
    out_shape = [sd((bsz, s, 512), BF16), sd((bsz, s, 512), BF16)] + \
                [sd((bsz, NSA_GROUPS, s, 256), BF16), sd((bsz, NSA_GROUPS, s, HEAD_DIM), BF16)] + \
                [sd((bsz, NSA_GROUPS, s, 128), BF16)] * 2 + \
                [sd((bsz, s, 128), F32), sd((bsz, s, 128), F32),
                 sd((bsz, NSA_GROUPS, s, 128), F32),
                 sd((bsz, s, 512), F32), sd((bsz, s, 1024), F32), sd((bsz, s, 128), F32)]
    out_specs = [tok(512), tok(512), hk, hm, hv, hv, tok(128), tok(128),
                 pl.BlockSpec((1, NSA_GROUPS, ts, 128), lambda b, i: (b, 0, i, 0)),
                 tok(512), tok(1024), tok(128)]
    return pl.pallas_call(
        _inproj_kernel,
        grid=grid,
        in_specs=[tok(d),
                  pl.BlockSpec((1, 6, d), lambda b, i: (b, 0, 0)),
                  pl.BlockSpec((1, d), lambda b, i: (0, 0)),
                  pl.BlockSpec((d, C_END), lambda b, i: (0, 0)),
                  pl.BlockSpec((ts, 128), lambda b, i: (i, 0)),
                  pl.BlockSpec((ts, 128), lambda b, i: (i, 0))],
        out_specs=out_specs,
        out_shape=out_shape,
        compiler_params=_cparams(("parallel", "parallel")),
        name="inproj",
    )(x, mod, norm_w, w_ext, cos_t, sin_t)


def _compress_kernel(k_ref, v_ref, pek_ref, pev_ref, w1k_ref, w2k_ref, w1v_ref, w2v_ref,
                     kc_ref, vc_ref, hb_ref):
    nch = k_ref.shape[1] // CMP_STRIDE

    def one(x_ref, pe_ref, w1_ref, w2_ref, o_ref):
        xs = [x_ref[0, pl.ds(l, nch, stride=CMP_STRIDE), :] for l in range(CMP_STRIDE)]
        top = jnp.concatenate([(xs[l] + pe_ref[l:l + 1, :]).astype(BF16) for l in range(CMP_STRIDE)], axis=1)
        bot = jnp.concatenate([(xs[l] + pe_ref[CMP_STRIDE + l:CMP_STRIDE + l + 1, :]).astype(BF16)
                               for l in range(CMP_STRIDE)], axis=1)
        htop = jnp.dot(top, w1_ref[0, 0], preferred_element_type=F32)
        hb_ref[0:nch, :] = jnp.dot(bot, w1_ref[1, 0], preferred_element_type=F32)
        hb_ref[nch:nch + 8, :] = jnp.zeros((8, CMP_HIDDEN), F32)
        pre = htop + hb_ref[pl.ds(1, nch), :]
        out = jnp.dot(_gelu_tanh(pre).astype(BF16), w2_ref[...], preferred_element_type=F32)
        row = lax.broadcasted_iota(jnp.int32, out.shape, 0)
        o_ref[0, 0] = jnp.where(row < nch - 1, out, 0.0).astype(BF16)

    one(k_ref, pek_ref, w1k_ref, w2k_ref, kc_ref)
    one(v_ref, pev_ref, w1v_ref, w2v_ref, vc_ref)


def _compress_call(kcmp, vcmp, pek, pev, w1k, w2k, w1v, w2v):
    bsz, s, cw = kcmp.shape
    nch = s // CMP_STRIDE
    xs = pl.BlockSpec((1, s, cw), lambda b, gi: (b, 0, 0))
    full = lambda a: pl.BlockSpec(a.shape, lambda b, gi: (0,) * a.ndim)
    w1s = pl.BlockSpec((2, 1) + w1k.shape[2:], lambda b, gi: (0, gi, 0, 0))
    os_ = pl.BlockSpec((1, 1, nch, HEAD_DIM), lambda b, gi: (b, gi, 0, 0))
    sd = jax.ShapeDtypeStruct((bsz, NSA_GROUPS, nch, HEAD_DIM), BF16)
    return pl.pallas_call(
        _compress_kernel,
        grid=(bsz, NSA_GROUPS),
        in_specs=[xs, xs, full(pek), full(pev), w1s, full(w2k), w1s, full(w2v)],
        out_specs=[os_, os_],
        out_shape=[sd, sd],
        scratch_shapes=[pltpu.VMEM((nch + 8, CMP_HIDDEN), F32)],
        compiler_params=_cparams(("parallel", "parallel")),
        name="compress",
    )(kcmp, vcmp, pek, pev, w1k, w2k, w1v, w2v)


NSA_TQ = 256
NSA_TK = 1024


def _nsa_kernel(qp_ref, qr_ref, kc_ref, vc_ref, ks_ref, vs_ref, kw_ref, vw_ref, gate_ref, ovt_ref, o_ref):
    tq = NSA_TQ
    tk = NSA_TK
    rows = NSA_HPG * tq
    qt = pl.program_id(2)
    q0 = qt * tq
    ncmp = kc_ref.shape[2]
    nsel = ovt_ref.shape[0]
    stack = lambda a: jnp.concatenate([a] * NSA_HPG, axis=0)
    qp = jnp.concatenate([qp_ref[0, :, r * HEAD_DIM:(r + 1) * HEAD_DIM] for r in range(NSA_HPG)], axis=0)
    qr = jnp.concatenate([qr_ref[0, :, r * HEAD_DIM:(r + 1) * HEAD_DIM] for r in range(NSA_HPG)], axis=0)

    t_row = q0 + lax.broadcasted_iota(jnp.int32, (tq, ncmp), 0)
    cend = lax.broadcasted_iota(jnp.int32, (tq, ncmp), 1) * CMP_STRIDE + (CMP_BLOCK - 1)
    cbias = stack(jnp.where(cend <= t_row, 0.0, NEG_INF))
    s = jnp.where(cbias == 0.0, _nt(qp, kc_ref[0, 0]), NEG_INF)
    e = jnp.exp2(s - jnp.max(s, axis=-1, keepdims=True))
    p = jnp.where(cbias == 0.0, e / jnp.sum(e, axis=-1, keepdims=True), 0.0)
    o_c = jnp.dot(p.astype(BF16), vc_ref[0, 0], preferred_element_type=F32)

    psum = p[0:tq] + p[tq:2 * tq] + p[2 * tq:3 * tq] + p[3 * tq:4 * tq]
    p_hi = psum.astype(BF16)
    p_lo = (psum - p_hi.astype(F32)).astype(BF16)
    ovt = ovt_ref[...]
    imp = _nt(ovt, p_hi) + _nt(ovt, p_lo)
    jblk = lax.broadcasted_iota(jnp.int32, (nsel, tq), 0)
    tt = q0 + lax.broadcasted_iota(jnp.int32, (nsel, tq), 1)
    blk_t = jnp.right_shift(tt, 6)
    forced = (jblk == 0) | (jblk == blk_t) | (jblk == blk_t - 1)
    xs = jnp.where(jblk * SEL_BLOCK <= tt, imp + jnp.where(forced, FORCE_BONUS, 0.0), NEG_INF)
    drop_t = jnp.full((nsel, tq), NEG_INF, F32)
    jblk_f = jblk.astype(F32)
    for _ in range(SEL_TOPN):
        mx = jnp.max(xs, axis=0, keepdims=True)
        idx = jnp.min(jnp.where(xs == mx, jblk_f, float(nsel)), axis=0, keepdims=True)
        hit = jblk_f == idx
        drop_t = jnp.where(hit, 0.0, drop_t)
        xs = jnp.where(hit, -jnp.inf, xs)
    drop = jnp.transpose(drop_t).astype(BF16)

    wk = WINDOW + tq
    wstart = pl.multiple_of(jnp.maximum(q0 - WINDOW, 0), tq)
    tq_w = q0 + lax.broadcasted_iota(jnp.int32, (tq, wk), 0)
    kpos_w = wstart + lax.broadcasted_iota(jnp.int32, (tq, wk), 1)
    bias_w = jnp.where((kpos_w <= tq_w) & (tq_w - kpos_w < WINDOW), 0.0, NEG_INF)
    sw = _nt(qr, kw_ref[0, 0, pl.ds(wstart, wk), :]) + stack(bias_w)
    ew = jnp.exp2(sw - jnp.max(sw, axis=-1, keepdims=True))
    pv_w = jnp.dot(ew.astype(BF16), vw_ref[0, 0, pl.ds(wstart, wk), :], preferred_element_type=F32)
    o_w = pv_w[:, :HEAD_DIM] / pv_w[:, HEAD_DIM:HEAD_DIM + 1]

    if nsel < 128:
        drop = jnp.concatenate([drop, jnp.zeros((tq, 128 - nsel), BF16)], axis=1)
    q_aug = jnp.concatenate([qr, stack(drop), jnp.zeros((rows, HEAD_DIM), BF16)], axis=1)

    def sel_tile(kt, carry, diagonal):
        m_i, acc = carry
        kbase = pl.multiple_of(kt * tk, tk)
        sc = _nt(q_aug, ks_ref[0, 0, pl.ds(kbase, tk), :])
        if diagonal:
            kpos = kbase + lax.broadcasted_iota(jnp.int32, (tq, tk), 1)
            tpos = q0 + lax.broadcasted_iota(jnp.int32, (tq, tk), 0)
            sc = sc + stack(jnp.where(kpos <= tpos, 0.0, NEG_INF))
        m_new = jnp.maximum(m_i, jnp.max(sc, axis=-1, keepdims=True))
        pe = jnp.exp2(sc - m_new)
        acc_new = jnp.exp2(m_i - m_new) * acc + jnp.dot(pe.astype(BF16), vs_ref[0, 0, pl.ds(kbase, tk), :],
                                                        preferred_element_type=F32)
        return m_new, acc_new

    init = (jnp.full((rows, 1), NEG_INF, F32), jnp.zeros((rows, 2 * HEAD_DIM), F32))
    last_kt = (q0 + tq - 1) // tk
    _, acc_s = sel_tile(last_kt, lax.fori_loop(0, last_kt, lambda kt, c: sel_tile(kt, c, False), init), True)
    o_s = acc_s[:, :HEAD_DIM] / acc_s[:, HEAD_DIM:HEAD_DIM + 1]

    gates = gate_ref[0, 0]
    for r in range(NSA_HPG):
        sl = slice(r * tq, (r + 1) * tq)
        o_r = (gates[:, 3 * r:3 * r + 1] * o_c[sl] + gates[:, 3 * r + 1:3 * r + 2] * o_s[sl]
               + gates[:, 3 * r + 2:3 * r + 3] * o_w[sl])
        o_ref[0, :, r * HEAD_DIM:(r + 1) * HEAD_DIM] = o_r


def _nsa_call(qp, qr, kc, vc, ks, vs, kw, vw, gates, ovt):
    bsz, s, _ = qp.shape
    tq = NSA_TQ
    ncmp = kc.shape[2]
    gw = NSA_HPG * HEAD_DIM
    qspec = pl.BlockSpec((1, tq, gw), lambda b, g, i: (b, i, g))
    cspec = pl.BlockSpec((1, 1, ncmp, HEAD_DIM), lambda b, g, i: (b, g, 0, 0))
    kspec = pl.BlockSpec((1, 1, s, HEAD_DIM), lambda b, g, i: (b, g, 0, 0))
    kaspec = pl.BlockSpec((1, 1, s, 4 * HEAD_DIM), lambda b, g, i: (b, g, 0, 0))
    vspec = pl.BlockSpec((1, 1, s, 2 * HEAD_DIM), lambda b, g, i: (b, g, 0, 0))
    return pl.pallas_call(
        _nsa_kernel,
        grid=(bsz, NSA_GROUPS, s // tq),
        in_specs=[qspec, qspec, cspec, cspec, kaspec, vspec, kspec, vspec,
                  pl.BlockSpec((1, 1, tq, 128), lambda b, g, i: (b, g, i, 0)),
                  pl.BlockSpec(ovt.shape, lambda b, g, i: (0, 0))],
        out_specs=pl.BlockSpec((1, tq, gw), lambda b, g, i: (b, i, g)),
        out_shape=jax.ShapeDtypeStruct((bsz, s, NSA_WIDTH), F32),
        compiler_params=_cparams(("parallel", "parallel", "arbitrary")),
        name="nsa",
    )(qp, qr, kc, vc, ks, vs, kw, vw, gates, ovt)


def _softplus(x):
    return jnp.maximum(x, 0.0) + jnp.log1p(jnp.exp(-jnp.abs(x)))


def _ssd_kernel(xbc_ref, z_ref, dt_ref, dtt_ref, cw_ref, cb_ref, dtb_ref, dtbt_ref, al_ref, alt_ref,
                dsk_ref, nw_ref, o_ref, tail_ref, xp_ref, st_ref, y_ref):
    L = SSD_CHUNK
    P = HEAD_DIM
    N = SSD_STATE
    c = pl.program_id(1)

    @pl.when(c == 0)
    def _():
        tail_ref[...] = jnp.zeros(tail_ref.shape, F32)
        st_ref[...] = jnp.zeros(st_ref.shape, F32)

    xin = xbc_ref[0]
    xp_ref[0:8, :] = tail_ref[...]
    xp_ref[8:8 + L, :] = xin
    tail_ref[...] = xin[L - 8:L, :]
    conv = cb_ref[...] + jnp.zeros_like(xin)
    for k in range(SSD_CONV):
        conv = conv + cw_ref[k:k + 1, :] * xp_ref[pl.ds(8 - (SSD_CONV - 1) + k, L), :]
    u = conv * (1.0 / (1.0 + jnp.exp(-conv)))
    xs = u[:, :SSD_WIDTH]

    dt_c = _softplus(dt_ref[0] + dtb_ref[...])
    dt_r = _softplus(dtt_ref[0] + dtbt_ref[...])
    a_c = -jnp.exp(al_ref[...])
    a_r = -jnp.exp(alt_ref[...])
    li = lax.broadcasted_iota(jnp.int32, (L, L), 0)
    si = lax.broadcasted_iota(jnp.int32, (L, L), 1)
    causal = li >= si
    tri = jnp.where(causal, 1.0, 0.0)
    tri_t = jnp.where(li <= si, 1.0, 0.0)
    acs_c = jnp.dot(tri, dt_c * a_c, preferred_element_type=F32, precision=HIGHEST)
    acs_r = jnp.dot(dt_r * a_r, tri_t, preferred_element_type=F32, precision=HIGHEST)

    for g in range(SSD_GROUPS):
        bm = u[:, SSD_WIDTH + g * N:SSD_WIDTH + (g + 1) * N]
        cm = u[:, SSD_WIDTH + SSD_GROUPS * N + g * N:SSD_WIDTH + SSD_GROUPS * N + (g + 1) * N]
        bm_b = bm.astype(BF16)
        cm_b = cm.astype(BF16)
        cb = _nt(cm_b, bm_b)
        bm_t = jnp.transpose(bm)
        for r in range(SSD_HEADS // SSD_GROUPS):
            hh = g * (SSD_HEADS // SSD_GROUPS) + r
            col = acs_c[:, hh:hh + 1]
            row = acs_r[hh:hh + 1, :]
            last = acs_r[hh:hh + 1, L - 1:L]
            lm = jnp.exp(jnp.where(causal, col - row, NEG_INF))
            x_h = xs[:, hh * P:(hh + 1) * P]
            xd = x_h * dt_c[:, hh:hh + 1]
            y_d = jnp.dot((cb * lm).astype(BF16), xd.astype(BF16), preferred_element_type=F32)
            prev = st_ref[hh]
            y_o = jnp.dot(cm_b, prev.astype(BF16), preferred_element_type=F32) * jnp.exp(col)
            dec = jnp.exp(last - row)
            st_new = jnp.dot((bm_t * dec).astype(BF16), xd.astype(BF16), preferred_element_type=F32)
            st_ref[hh] = prev * jnp.exp(last) + st_new
            y_ref[:, hh * P:(hh + 1) * P] = y_d + y_o + x_h * dsk_ref[0:1, hh:hh + 1]

    zz = z_ref[0]
    y = y_ref[...] * (zz * (1.0 / (1.0 + jnp.exp(-zz))))
    ms = jnp.mean(y * y, axis=-1, keepdims=True)
    o_ref[0] = (y * lax.rsqrt(ms + NORM_EPS) * nw_ref[...]).astype(o_ref.dtype)


def _ssd_call(xbc, z, dt, dtt, conv_w, conv_b, dt_bias, a_log, d_skip, norm_w):
    bsz, s, cch = xbc.shape
    L = SSD_CHUNK
    pad = lambda v: jnp.pad(v.reshape(1, -1), ((0, 0), (0, 128 - v.size)))
    full = lambda a: pl.BlockSpec(a.shape, lambda b, i: (0,) * a.ndim)
    args = (xbc, z, dt, dtt, conv_w, conv_b.reshape(1, -1), pad(dt_bias), dt_bias.reshape(-1, 1),
            pad(a_log), a_log.reshape(-1, 1), pad(d_skip), norm_w.reshape(1, -1))
    in_specs = [pl.BlockSpec((1, L, cch), lambda b, i: (b, i, 0)),
                pl.BlockSpec((1, L, SSD_WIDTH), lambda b, i: (b, i, 0)),
                pl.BlockSpec((1, L, 128), lambda b, i: (b, i, 0)),
                pl.BlockSpec((1, SSD_HEADS, L), lambda b, i: (b, 0, i))] + [full(a) for a in args[4:]]
    return pl.pallas_call(
        _ssd_kernel,
        grid=(bsz, s // L),
        in_specs=in_specs,
        out_specs=pl.BlockSpec((1, L, SSD_WIDTH), lambda b, i: (b, i, 0)),
        out_shape=jax.ShapeDtypeStruct((bsz, s, SSD_WIDTH), BF16),
        scratch_shapes=[pltpu.VMEM((8, cch), F32), pltpu.VMEM((8 + L, cch), F32),
                        pltpu.VMEM((SSD_HEADS, SSD_STATE, HEAD_DIM), F32), pltpu.VMEM((L, SSD_WIDTH), F32)],
        compiler_params=_cparams(("parallel", "arbitrary")),
        name="ssd",
    )(*args)


def _outproj_kernel(on_ref, os_ref, x_ref, mod_ref, nnw_ref, n2w_ref, wo_ref, wq_ref, qnw_ref, sk_ref,
                    x1_ref, h2_ref, sc_ref):
    o = on_ref[0]
    ms = jnp.mean(o * o, axis=-1, keepdims=True)
    on = (o * lax.rsqrt(ms + NORM_EPS) * nnw_ref[...]).astype(BF16)
    mix = jnp.dot(jnp.concatenate([on, os_ref[0]], axis=1), wo_ref[...], preferred_element_type=F32)
    x1 = x_ref[0] + mod_ref[0, 2:3, :] * mix
    x1_ref[0] = x1
    ms2 = jnp.mean(x1 * x1, axis=-1, keepdims=True)
    h2 = ((x1 * lax.rsqrt(ms2 + NORM_EPS) * n2w_ref[...]) * (1.0 + mod_ref[0, 4:5, :]) + mod_ref[0, 3:4, :]).astype(BF16)
    h2_ref[0] = h2
    qall = jnp.dot(h2, wq_ref[...], preferred_element_type=F32)
    for hh in range(PEER_HEADS):
        qh = qall[:, hh * PEER_QDIM:(hh + 1) * PEER_QDIM]
        qn = (qh * lax.rsqrt(jnp.mean(qh * qh, axis=-1, keepdims=True) + NORM_EPS) * qnw_ref[...]).astype(BF16)
        both = _nt(sk_ref[hh], qn)
        sc_ref[2 * hh] = both[:PEER_NKEYS]
        sc_ref[2 * hh + 1] = both[PEER_NKEYS:]


def _outproj_call(o_nsa, o_ssd, x, mod, nsa_nw, n2w, w_out, wq, qnw, sub_keys, ts):
    bsz, s, d = x.shape
    nblk = s // ts
    tok = lambda w: pl.BlockSpec((1, ts, w), lambda b, i: (b, i, 0))
    full = lambda a: pl.BlockSpec(a.shape, lambda b, i: (0,) * a.ndim)
    return pl.pallas_call(
        _outproj_kernel,
        grid=(bsz, nblk),
        in_specs=[tok(NSA_WIDTH), tok(SSD_WIDTH), tok(d), pl.BlockSpec((1, 6, d), lambda b, i: (b, 0, 0)),
                  full(nsa_nw), full(n2w), full(w_out), full(wq), full(qnw), full(sub_keys)],
        out_specs=[tok(d), tok(d),
                   pl.BlockSpec((2 * PEER_HEADS, PEER_NKEYS, ts), lambda b, i: (0, 0, b * nblk + i))],
        out_shape=[jax.ShapeDtypeStruct((bsz, s, d), F32), jax.ShapeDtypeStruct((bsz, s, d), BF16),
                   jax.ShapeDtypeStruct((2 * PEER_HEADS, PEER_NKEYS, bsz * s), F32)],
        compiler_params=_cparams(("parallel", "parallel")),
        name="outproj",
    )(o_nsa, o_ssd, x, mod, nsa_nw, n2w, w_out, wq, qnw, sub_keys)


PEER_TT = 128
PEER_HG = 4


def _peersel_kernel(sc_ref, c_ref, e1_ref, r2_ref, e2_ref):
    nk = PEER_NKEYS
    tt = PEER_TT
    K = PEER_TOPK
    kidx = lax.broadcasted_iota(jnp.int32, (nk, tt), 0).astype(F32)
    i16 = lax.broadcasted_iota(jnp.int32, (K, tt), 0).astype(F32)

    def topk_sorted(x, exact_ties, want_rank=True):
        rank = jnp.full((nk, tt), float(K), F32)
        vals = jnp.zeros((K, tt), F32)
        for j in range(K):
            mx = jnp.max(x, axis=0, keepdims=True)
            hit = x == mx
            if exact_ties:
                hit = kidx == jnp.min(jnp.where(hit, kidx, float(nk)), axis=0, keepdims=True)
            if want_rank:
                rank = jnp.where(hit, float(j), rank)
            x = jnp.where(hit, -jnp.inf, x)
            vals = jnp.where(i16 == float(j), mx, vals)
        taken = jnp.sum(jnp.where(x == -jnp.inf, 1.0, 0.0), axis=0, keepdims=True)
        return rank, vals, taken

    def first_stage(hh0, nh):
        ss = [sc_ref[2 * hh0 + i] for i in range(2 * nh)]
        fast = [topk_sorted(s, False, want_rank=i % 2 == 1) for i, s in enumerate(ss)]
        most = fast[0][2]
        for f in fast[1:]:
            most = jnp.maximum(most, f[2])

        def exact():
            out = ()
            for i in range(nh):
                r1, v1a, _ = topk_sorted(ss[2 * i], True)
                r2, v2a, _ = topk_sorted(ss[2 * i + 1], True)
                out += (r1, i16, v1a, r2, v2a)
            return out

        def tie_free():
            out = ()
            for i in range(nh):
                out += (ss[2 * i], fast[2 * i][1], fast[2 * i][1], fast[2 * i + 1][0], fast[2 * i + 1][1])
            return out

        res = lax.cond(jnp.max(most) > float(K), exact, tie_free)
        return [(ss[2 * i], ss[2 * i + 1]) + tuple(res[5 * i:5 * i + 5]) for i in range(nh)]

    def second_stage(hh, s1, s2, key1, match1, v1a, r2, v2a):
        v1 = [v1a[i:i + 1, :] for i in range(K)]
        v2 = [v2a[i:i + 1, :] for i in range(K)]
        cmax = v1[0] + v2[0]
        n = jnp.zeros((K, tt), F32)
        f = v1a + v2[0]
        zsum = jnp.zeros((1, tt), F32)
        for _ in range(K):
            mx = jnp.max(f, axis=0, keepdims=True)
            iw = jnp.min(jnp.where(f == mx, i16, float(K)), axis=0, keepdims=True)
            hit = i16 == iw
            n = n + jnp.where(hit, 1.0, 0.0)
            zsum = zsum + jnp.exp(mx - cmax)
            nstar = jnp.sum(jnp.where(hit, n, 0.0), axis=0, keepdims=True)
            v2n = jnp.sum(jnp.where(i16 == nstar, v2a, 0.0), axis=0, keepdims=True)
            f = jnp.where(hit, v1a + v2n, f)
        cnt = jnp.zeros((nk, tt), F32)
        for i in range(K):
            cnt = jnp.where(key1 == match1[i:i + 1, :], n[i:i + 1, :], cnt)
        c_ref[hh] = cnt
        e1_ref[hh] = jnp.exp(s1 - v1[0]) / zsum
        r2_ref[hh] = r2.astype(BF16)
        e2_ref[hh] = jnp.exp(s2 - v2[0]).astype(BF16)

    for hh0 in range(0, PEER_HEADS, PEER_HG):
        for i, args in enumerate(first_stage(hh0, PEER_HG)):
            second_stage(hh0 + i, *args)


def _peersel_call(sc):
    _, nk, t = sc.shape
    tt = PEER_TT
    ospec = pl.BlockSpec((PEER_HEADS, nk, tt), lambda i: (0, 0, i))
    sd = lambda dt: jax.ShapeDtypeStruct((PEER_HEADS, nk, t), dt)
    return pl.pallas_call(
        _peersel_kernel,
        grid=(t // tt,),
        in_specs=[pl.BlockSpec((2 * PEER_HEADS, nk, tt), lambda i: (0, 0, i))],
        out_specs=[ospec] * 4,
        out_shape=[sd(F32), sd(F32), sd(BF16), sd(BF16)],
        compiler_params=_cparams(("parallel",)),
        name="peersel",
    )(sc)


PEER_TB = 512
PEER_EC = 1024


def _peer_kernel(h2_ref, c_ref, e1_ref, r2_ref, e2_ref, down0_ref, downn_ref, upt_ref, x1_ref, mod_ref, fw_ref,
                 o_ref, acc_ref, act_ref, w_ref):
    j = pl.program_id(1)
    nj = pl.num_programs(1)
    nk = PEER_NKEYS
    na = PEER_EC // nk

    def build_gates(chunk):
        a0 = pl.multiple_of(chunk * na, na)

        def rows(ref, hh, ai):
            grp = ref[hh, pl.ds(a0, na), :]
            r16 = jnp.broadcast_to(grp[ai:ai + 1, :], (16, PEER_TB)).astype(BF16)
            return jnp.concatenate([r16] * (nk // 16), axis=0)

        for ai in range(na):
            w = None
            for hh in range(PEER_HEADS):
                term = jnp.where(r2_ref[hh] < rows(c_ref, hh, ai), e2_ref[hh], 0.0) * rows(e1_ref, hh, ai)
                w = term if w is None else w + term
            w_ref[ai * nk:(ai + 1) * nk, :] = w

    @pl.when(j == 0)
    def _():
        acc_ref[...] = jnp.zeros(acc_ref.shape, F32)
        act_ref[...] = _gelu_tanh(_nt(down0_ref[...], h2_ref[...]).astype(BF16))
        build_gates(0)

    wa = w_ref[...] * act_ref[...]
    acc_ref[...] += jnp.dot(upt_ref[...], wa, preferred_element_type=F32)
    act_ref[...] = _gelu_tanh(_nt(downn_ref[...], h2_ref[...]).astype(BF16))
    build_gates(jnp.minimum(j + 1, nj - 1))

    @pl.when(j == nj - 1)
    def _():
        y = jnp.transpose(acc_ref[...])
        x2 = x1_ref[...] + mod_ref[0, 5:6, :] * y
        ms = jnp.mean(x2 * x2, axis=-1, keepdims=True)
        o_ref[...] = x2 * lax.rsqrt(ms + NORM_EPS) * fw_ref[...]


def _peer_call(h2, cnt, e1, r2, e2, down, upt, x1, mod, fw, s):
    t, d = h2.shape
    ne = down.shape[0]
    tb, ec = PEER_TB, PEER_EC
    per_b = s // tb
    last = ne // ec - 1
    dspec = pl.BlockSpec((PEER_HEADS, PEER_NKEYS, tb), lambda i, j: (0, 0, i))
    return pl.pallas_call(
        _peer_kernel,
        grid=(t // tb, ne // ec),
        in_specs=[pl.BlockSpec((tb, d), lambda i, j: (i, 0)), dspec, dspec, dspec, dspec,
                  pl.BlockSpec((ec, d), lambda i, j: (0, 0)),
                  pl.BlockSpec((ec, d), lambda i, j: (jnp.minimum(j + 1, last), 0)),
                  pl.BlockSpec((d, ec), lambda i, j: (0, j)),
                  pl.BlockSpec((tb, d), lambda i, j: (i, 0)),
                  pl.BlockSpec((1, 6, d), lambda i, j: (i // per_b, 0, 0)),
                  pl.BlockSpec((1, d), lambda i, j: (0, 0))],
        out_specs=pl.BlockSpec((tb, d), lambda i, j: (i, 0)),
        out_shape=jax.ShapeDtypeStruct((t, d), F32),
        scratch_shapes=[pltpu.VMEM((d, tb), F32), pltpu.VMEM((ec, tb), BF16), pltpu.VMEM((ec, tb), BF16)],
        compiler_params=_cparams(("parallel", "arbitrary")),
        name="peer",
    )(h2, cnt, e1, r2, e2, down, down, upt, x1, mod, fw)


def _rope_tables(s):
    half = ROPE_DIM // 2
    inv_freq = ROPE_THETA ** (-jnp.arange(half, dtype=F32) / half)
    ang = jnp.arange(s).astype(F32)[:, None] * inv_freq[None, :]
    cos, sin = jnp.cos(ang), jnp.sin(ang)
    one = jnp.ones((s, HEAD_DIM - ROPE_DIM), F32)
    cos64 = jnp.concatenate([cos, cos, one], axis=1)
    sin64 = jnp.concatenate([-sin, sin, 0.0 * one], axis=1)
    return jnp.concatenate([cos64, cos64], axis=1), jnp.concatenate([sin64, sin64], axis=1)


def _swap_cols(w):
    d, n = w.shape
    wh = w.reshape(d, n // HEAD_DIM, HEAD_DIM)
    half = ROPE_DIM // 2
    sw = jnp.concatenate([wh[..., half:ROPE_DIM], wh[..., :half], jnp.zeros_like(wh[..., ROPE_DIM:])], axis=-1)
    return sw.reshape(d, n)


def _pack_w_in(w):
    o = np.cumsum((0, 512, 128, 128, 128, 128, 128, 128, 24, 512, 1024, 8))
    q, kc, vc, ksel, vsel, kwin, vwin, gl, z, xbc, dtr = (w[:, o[i]:o[i + 1]] for i in range(11))
    gl = gl.reshape(-1, NSA_GROUPS, NSA_HPG * N_BRANCH)
    gl = jnp.pad(gl, ((0, 0), (0, 0), (0, 128 - NSA_HPG * N_BRANCH))).reshape(-1, 256)
    dtr = jnp.pad(dtr, ((0, 0), (0, 128 - SSD_HEADS)))
    cols = [q, _swap_cols(q), ksel, _swap_cols(ksel), kwin, _swap_cols(kwin), vsel, vwin, kc, vc, gl, z, xbc, dtr]
    return jnp.concatenate(cols, axis=1).astype(BF16)


def _overlap_t(s):
    n_cmp_pad = s // CMP_STRIDE
    n_sel = s // SEL_BLOCK
    cs = np.arange(n_cmp_pad) * CMP_STRIDE
    ss = np.arange(n_sel) * SEL_BLOCK
    ov = np.maximum(np.minimum(cs[None, :] + CMP_BLOCK, ss[:, None] + SEL_BLOCK)
                    - np.maximum(cs[None, :], ss[:, None]), 0).astype(np.float32) / CMP_BLOCK
    ov[:, n_cmp_pad - 1] = 0.0
    return jnp.asarray(ov, BF16)


def _blockdiag_keys(sub_keys):
    h, two, n, half = sub_keys.shape
    z = jnp.zeros((h, n, half), sub_keys.dtype)
    top = jnp.concatenate([sub_keys[:, 0], z], axis=2)
    bot = jnp.concatenate([z, sub_keys[:, 1]], axis=2)
    return jnp.concatenate([top, bot], axis=1).astype(BF16)


def kernel(x, c, w_ada, b_ada, norm1_w, w_in, cmp_pe_k, cmp_pe_v, cmp_w1_k, cmp_w2_k, cmp_w1_v, cmp_w2_v,
           nsa_norm_w, conv_w, conv_b, dt_bias, a_log, d_skip, ssd_norm_w, w_out, norm2_w,
           peer_wq, peer_qnorm_w, peer_sub_keys, peer_down, peer_up, final_norm_w):
    bsz, s, d = x.shape
    assert d == D_MODEL and s % NSA_TK == 0 and s // SEL_BLOCK <= 128 and w_ada.shape[0] == 1
    lyr = 0
    ts = 512

    mod = _mod_call(c, w_ada[lyr], b_ada[lyr]).reshape(bsz, 6, d)
    cos_t, sin_t = _rope_tables(s)
    (qp, qr, ksel, kwin, vsel, vwin, kcmp, vcmp, gates, z, xbc, dtr) = _inproj_call(
        x, mod, norm1_w[lyr].reshape(1, d), _pack_w_in(w_in[lyr]), cos_t, sin_t, ts)

    def w1_groups(w1):
        w = w1.reshape(2, CMP_STRIDE, HEAD_DIM, CMP_HIDDEN)
        z = jnp.zeros_like(w)
        per_g = [jnp.concatenate([w, z], axis=2), jnp.concatenate([z, w], axis=2)]
        return jnp.stack(per_g, axis=1).reshape(2, NSA_GROUPS, CMP_STRIDE * 2 * HEAD_DIM, CMP_HIDDEN).astype(BF16)

    kc, vc = _compress_call(
        kcmp, vcmp, jnp.tile(cmp_pe_k[lyr], (1, NSA_GROUPS)), jnp.tile(cmp_pe_v[lyr], (1, NSA_GROUPS)),
        w1_groups(cmp_w1_k[lyr]), cmp_w2_k[lyr].astype(BF16), w1_groups(cmp_w1_v[lyr]), cmp_w2_v[lyr].astype(BF16))

    o_nsa = _nsa_call(qp, qr, kc, vc, ksel, vsel, kwin, vwin, gates, _overlap_t(s))

    dtt = jnp.transpose(dtr[:, :, :SSD_HEADS], (0, 2, 1))
    o_ssd = _ssd_call(xbc, z, dtr, dtt, conv_w[lyr], conv_b[lyr], dt_bias[lyr], a_log[lyr], d_skip[lyr],
                      ssd_norm_w[lyr])

    x1, h2, sc = _outproj_call(
        o_nsa, o_ssd, x, mod, nsa_norm_w[lyr].reshape(1, -1), norm2_w[lyr].reshape(1, d),
        w_out[lyr].astype(BF16), peer_wq[lyr].astype(BF16), peer_qnorm_w[lyr].reshape(1, -1),
        _blockdiag_keys(peer_sub_keys[lyr]), ts)

    cnt, e1, r2, e2 = _peersel_call(sc)

    out = _peer_call(h2.reshape(bsz * s, d), cnt, e1, r2, e2, peer_down[lyr].astype(BF16),
                     jnp.transpose(peer_up[lyr]).astype(BF16), x1.reshape(bsz * s, d), mod,
                     final_norm_w.reshape(1, d), s)
    return out.reshape(bsz, s, d)
```

```python
import functools
import math

import numpy as np
import jax
import jax.numpy as jnp
from jax import lax
from jax.experimental import pallas as pl
from jax.experimental.pallas import tpu as pltpu

F32 = jnp.float32
BF16 = jnp.bfloat16
HIGHEST = lax.Precision.HIGHEST

D_MODEL = 1024
NSA_WIDTH = 512
SSD_WIDTH = 512
HEAD_DIM = 64
NSA_HEADS = 8
NSA_GROUPS = 2
NSA_HPG = 4
N_BRANCH = 3
CMP_BLOCK = 32
CMP_STRIDE = 16
CMP_HIDDEN = 256
SEL_BLOCK = 64
SEL_TOPN = 16
WINDOW = 512
ROPE_THETA = 500000.0
ROPE_DIM = 16
SSD_HEADS = 8
SSD_GROUPS = 2
SSD_STATE = 128
SSD_CONV = 4
SSD_CHUNK = 128
PEER_HEADS = 8
PEER_NKEYS = 128
PEER_QDIM = 256
PEER_TOPK = 16
NORM_EPS = 1e-6
NEG_INF = -1e30
LOG2E = math.log2(math.e)
FORCE_BONUS = 1e4

LANES = 128
VMEM_LIMIT = 56 * 1024 * 1024

C_Q, C_QSW, C_KSEL, C_KSELSW, C_KWIN, C_KWINSW = 0, 512, 1024, 1152, 1280, 1408
C_VSEL, C_VWIN, C_KCMP, C_VCMP, C_GATE, C_Z, C_XBC, C_DT, C_END = 1536, 1664, 1792, 1920, 2048, 2304, 2816, 3840, 3968


def _gelu_tanh(x):
    c = math.sqrt(2.0 / math.pi)
    return 0.5 * x * (1.0 + jnp.tanh(c * (x + 0.044715 * (x * x * x))))


def _nt(a, b):
    return lax.dot_general(a, b, (((1,), (1,)), ((), ())), preferred_element_type=F32)


def _cparams(sem):
    return pltpu.CompilerParams(dimension_semantics=sem, vmem_limit_bytes=VMEM_LIMIT)


def _mod_kernel(c_ref, w_ref, b_ref, o_ref):
    o_ref[...] = jnp.dot(c_ref[...], w_ref[...], preferred_element_type=F32, precision=HIGHEST) + b_ref[...]


def _mod_call(c, w_ada, b_ada):
    bsz = c.shape[0]
    n = w_ada.shape[1]
    return pl.pallas_call(
        _mod_kernel,
        grid=(n // D_MODEL,),
        in_specs=[pl.BlockSpec((bsz, D_MODEL), lambda j: (0, 0)),
                  pl.BlockSpec((D_MODEL, D_MODEL), lambda j: (0, j)),
                  pl.BlockSpec((1, D_MODEL), lambda j: (0, j))],
        out_specs=pl.BlockSpec((bsz, D_MODEL), lambda j: (0, j)),
        out_shape=jax.ShapeDtypeStruct((bsz, n), F32),
        compiler_params=_cparams(("arbitrary",)),
        name="mod",
    )(c, w_ada, b_ada.reshape(1, n))


def _inproj_kernel(x_ref, mod_ref, nw_ref, w_ref, cos_ref, sin_ref,
                   qp_ref, qr_ref, ksel_ref, kwin_ref, vsel_ref, vwin_ref,
                   kcmp_ref, vcmp_ref, gate_ref, z_ref, xbc_ref, dt_ref):
    x = x_ref[0]
    ms = jnp.mean(x * x, axis=-1, keepdims=True)
    y = x * lax.rsqrt(ms + NORM_EPS) * nw_ref[...]
    h = (y * (1.0 + mod_ref[0, 1:2, :]) + mod_ref[0, 0:1, :]).astype(BF16)

    def proj(lo, hi):
        return jnp.dot(h, w_ref[:, lo:hi], preferred_element_type=F32)

    cos = cos_ref[...]
    sin = sin_ref[...]
    scale = HEAD_DIM ** -0.5 * LOG2E
    qa = proj(C_Q, C_KSEL)
    kv = proj(C_KSEL, C_GATE)
    rest = proj(C_GATE, C_END)
    q = qa[:, :512]
    qsw = qa[:, 512:]
    qp_ref[0] = (q * scale).astype(BF16)
    cos4 = jnp.concatenate([cos] * 4, axis=1)
    sin4 = jnp.concatenate([sin] * 4, axis=1)
    qr_ref[0] = ((q * cos4 + qsw * sin4) * scale).astype(BF16)

    part = lambda c: kv[:, c - C_KSEL:c - C_KSEL + 128]
    ks = part(C_KSEL) * cos + part(C_KSELSW) * sin
    kw = part(C_KWIN) * cos + part(C_KWINSW) * sin
    vs = part(C_VSEL)
    vw = part(C_VWIN)
    kwin_ref[0, 0] = kw[:, :HEAD_DIM].astype(BF16)
    kwin_ref[0, 1] = kw[:, HEAD_DIM:].astype(BF16)
    lane = lax.broadcasted_iota(jnp.int32, vs.shape, 1)
    blk = jnp.right_shift(pl.program_id(1) * ks.shape[0] + lax.broadcasted_iota(jnp.int32, ks.shape, 0), 6)
    hot_lo = jnp.where(lane - HEAD_DIM == blk, 1.0, 0.0)
    hot_hi = jnp.where(lane + HEAD_DIM == blk, 1.0, 0.0).astype(BF16)
    ksel_ref[0, 0, :, 0:128] = jnp.where(lane < HEAD_DIM, ks, hot_lo).astype(BF16)
    ksel_ref[0, 1, :, 0:128] = jnp.where(lane < HEAD_DIM, pltpu.roll(ks, HEAD_DIM, axis=1), hot_lo).astype(BF16)
    ksel_ref[0, 0, :, 128:256] = hot_hi
    ksel_ref[0, 1, :, 128:256] = hot_hi
    ones_col = jnp.where(lane == HEAD_DIM, 1.0, 0.0)
    for ref, val in ((vsel_ref, vs), (vwin_ref, vw)):
        ref[0, 0] = jnp.where(lane < HEAD_DIM, val, ones_col).astype(BF16)
        ref[0, 1] = jnp.where(lane < HEAD_DIM, pltpu.roll(val, HEAD_DIM, axis=1), ones_col).astype(BF16)
    kcmp_ref[0] = part(C_KCMP)
    vcmp_ref[0] = part(C_VCMP)
    sg = 1.0 / (1.0 + jnp.exp(-rest[:, :C_Z - C_GATE]))
    gate_ref[0, 0] = sg[:, :128]
    gate_ref[0, 1] = sg[:, 128:]
    z_ref[0] = rest[:, C_Z - C_GATE:C_XBC - C_GATE]
    xbc_ref[0] = rest[:, C_XBC - C_GATE:C_DT - C_GATE]
    dt_ref[0] = rest[:, C_DT - C_GATE:]


def _inproj_call(x, mod, norm_w, w_ext, cos_t, sin_t, ts):
    bsz, s, d = x.shape
    grid = (bsz, s // ts)
    tok = lambda w: pl.BlockSpec((1, ts, w), lambda b, i: (b, i, 0))
    hm = pl.BlockSpec((1, NSA_GROUPS, ts, HEAD_DIM), lambda b, i: (b, 0, i, 0))
    hv = pl.BlockSpec((1, NSA_GROUPS, ts, 128), lambda b, i: (b, 0, i, 0))
    hk = pl.BlockSpec((1, NSA_GROUPS, ts, 256), lambda b, i: (b, 0, i, 0))
    sd = jax.ShapeDtypeStruct
    out_shape = [sd((bsz, s, 512), BF16), sd((bsz, s, 512), BF16)] + \
                [sd((bsz, NSA_GROUPS, s, 256), BF16), sd((bsz, NSA_GROUPS, s, HEAD_DIM), BF16)] + \
                [sd((bsz, NSA_GROUPS, s, 128), BF16)] * 2 + \
                [sd((bsz, s, 128), F32), sd((bsz, s, 128), F32),
                 sd((bsz, NSA_GROUPS, s, 128), F32),
                 sd((bsz, s, 512), F32), sd((bsz, s, 1024), F32), sd((bsz, s, 128), F32)]
    out_specs = [tok(512), tok(512), hk, hm, hv, hv, tok(128), tok(128),
                 pl.BlockSpec((1, NSA_GROUPS, ts, 128), lambda b, i: (b, 0, i, 0)),
                 tok(512), tok(1024), tok(128)]
    return pl.pallas_call(
        _inproj_kernel,
        grid=grid,
        in_specs=[tok(d),
                  pl.BlockSpec((1, 6, d), lambda b, i: (b, 0, 0)),
                  pl.BlockSpec((1, d), lambda b, i: (0, 0)),
                  pl.BlockSpec((d, C_END), lambda b, i: (0, 0)),
                  pl.BlockSpec((ts, 128), lambda b, i: (i, 0)),
                  pl.BlockSpec((ts, 128), lambda b, i: (i, 0))],
        out_specs=out_specs,
        out_shape=out_shape,
        compiler_params=_cparams(("parallel", "parallel")),
        name="inproj",
    )(x, mod, norm_w, w_ext, cos_t, sin_t)


def _compress_kernel(k_ref, v_ref, pek_ref, pev_ref, w1k_ref, w2k_ref, w1v_ref, w2v_ref,
                     kc_ref, vc_ref, hb_ref):
    nch = k_ref.shape[1] // CMP_STRIDE

    def one(x_ref, pe_ref, w1_ref, w2_ref, o_ref):
        xs = [x_ref[0, pl.ds(l, nch, stride=CMP_STRIDE), :] for l in range(CMP_STRIDE)]
        top = jnp.concatenate([(xs[l] + pe_ref[l:l + 1, :]).astype(BF16) for l in range(CMP_STRIDE)], axis=1)
        bot = jnp.concatenate([(xs[l] + pe_ref[CMP_STRIDE + l:CMP_STRIDE + l + 1, :]).astype(BF16)
                               for l in range(CMP_STRIDE)], axis=1)
        htop = jnp.dot(top, w1_ref[0, 0], preferred_element_type=F32)
        hb_ref[0:nch, :] = jnp.dot(bot, w1_ref[1, 0], preferred_element_type=F32)
        hb_ref[nch:nch + 8, :] = jnp.zeros((8, CMP_HIDDEN), F32)
        pre = htop + hb_ref[pl.ds(1, nch), :]
        out = jnp.dot(_gelu_tanh(pre).astype(BF16), w2_ref[...], preferred_element_type=F32)
        row = lax.broadcasted_iota(jnp.int32, out.shape, 0)
        o_ref[0, 0] = jnp.where(row < nch - 1, out, 0.0).astype(BF16)

    one(k_ref, pek_ref, w1k_ref, w2k_ref, kc_ref)
    one(v_ref, pev_ref, w1v_ref, w2v_ref, vc_ref)


def _compress_call(kcmp, vcmp, pek, pev, w1k, w2k, w1v, w2v):
    bsz, s, cw = kcmp.shape
    nch = s // CMP_STRIDE
    xs = pl.BlockSpec((1, s, cw), lambda b, gi: (b, 0, 0))
    full = lambda a: pl.BlockSpec(a.shape, lambda b, gi: (0,) * a.ndim)
    w1s = pl.BlockSpec((2, 1) + w1k.shape[2:], lambda b, gi: (0, gi, 0, 0))
    os_ = pl.BlockSpec((1, 1, nch, HEAD_DIM), lambda b, gi: (b, gi, 0, 0))
    sd = jax.ShapeDtypeStruct((bsz, NSA_GROUPS, nch, HEAD_DIM), BF16)
    return pl.pallas_call(
        _compress_kernel,
        grid=(bsz, NSA_GROUPS),
        in_specs=[xs, xs, full(pek), full(pev), w1s, full(w2k), w1s, full(w2v)],
        out_specs=[os_, os_],
        out_shape=[sd, sd],
        scratch_shapes=[pltpu.VMEM((nch + 8, CMP_HIDDEN), F32)],
        compiler_params=_cparams(("parallel", "parallel")),
        name="compress",
    )(kcmp, vcmp, pek, pev, w1k, w2k, w1v, w2v)


NSA_TQ = 256
NSA_TK = 1024


def _nsa_kernel(qp_ref, qr_ref, kc_ref, vc_ref, ks_ref, vs_ref, kw_ref, vw_ref, gate_ref, ovt_ref, o_ref):
    tq = NSA_TQ
    tk = NSA_TK
    rows = NSA_HPG * tq
    qt = pl.program_id(2)
    q0 = qt * tq
    ncmp = kc_ref.shape[2]
    nsel = ovt_ref.shape[0]
    stack = lambda a: jnp.concatenate([a] * NSA_HPG, axis=0)
    qp = jnp.concatenate([qp_ref[0, :, r * HEAD_DIM:(r + 1) * HEAD_DIM] for r in range(NSA_HPG)], axis=0)
    qr = jnp.concatenate([qr_ref[0, :, r * HEAD_DIM:(r + 1) * HEAD_DIM] for r in range(NSA_HPG)], axis=0)

    t_row = q0 + lax.broadcasted_iota(jnp.int32, (tq, ncmp), 0)
    cend = lax.broadcasted_iota(jnp.int32, (tq, ncmp), 1) * CMP_STRIDE + (CMP_BLOCK - 1)
    cbias = stack(jnp.where(cend <= t_row, 0.0, NEG_INF))
    s = jnp.where(cbias == 0.0, _nt(qp, kc_ref[0, 0]), NEG_INF)
    e = jnp.exp2(s - jnp.max(s, axis=-1, keepdims=True))
    p = jnp.where(cbias == 0.0, e / jnp.sum(e, axis=-1, keepdims=True), 0.0)
    o_c = jnp.dot(p.astype(BF16), vc_ref[0, 0], preferred_element_type=F32)

    psum = p[0:tq] + p[tq:2 * tq] + p[2 * tq:3 * tq] + p[3 * tq:4 * tq]
    p_hi = psum.astype(BF16)
    p_lo = (psum - p_hi.astype(F32)).astype(BF16)
    ovt = ovt_ref[...]
    imp = _nt(ovt, p_hi) + _nt(ovt, p_lo)
    jblk = lax.broadcasted_iota(jnp.int32, (nsel, tq), 0)
    tt = q0 + lax.broadcasted_iota(jnp.int32, (nsel, tq), 1)
    blk_t = jnp.right_shift(tt, 6)
    forced = (jblk == 0) | (jblk == blk_t) | (jblk == blk_t - 1)
    xs = jnp.where(jblk * SEL_BLOCK <= tt, imp + jnp.where(forced, FORCE_BONUS, 0.0), NEG_INF)
    drop_t = jnp.full((nsel, tq), NEG_INF, F32)
    jblk_f = jblk.astype(F32)
    for _ in range(SEL_TOPN):
        mx = jnp.max(xs, axis=0, keepdims=True)
        idx = jnp.min(jnp.where(xs == mx, jblk_f, float(nsel)), axis=0, keepdims=True)
        hit = jblk_f == idx
        drop_t = jnp.where(hit, 0.0, drop_t)
        xs = jnp.where(hit, -jnp.inf, xs)
    drop = jnp.transpose(drop_t).astype(BF16)

    wk = WINDOW + tq
    wstart = pl.multiple_of(jnp.maximum(q0 - WINDOW, 0), tq)
    tq_w = q0 + lax.broadcasted_iota(jnp.int32, (tq, wk), 0)
    kpos_w = wstart + lax.broadcasted_iota(jnp.int32, (tq, wk), 1)
    bias_w = jnp.where((kpos_w <= tq_w) & (tq_w - kpos_w < WINDOW), 0.0, NEG_INF)
    sw = _nt(qr, kw_ref[0, 0, pl.ds(wstart, wk), :]) + stack(bias_w)
    ew = jnp.exp2(sw - jnp.max(sw, axis=-1, keepdims=True))
    pv_w = jnp.dot(ew.astype(BF16), vw_ref[0, 0, pl.ds(wstart, wk), :], preferred_element_type=F32)
    o_w = pv_w[:, :HEAD_DIM] / pv_w[:, HEAD_DIM:HEAD_DIM + 1]

    if nsel < 128:
        drop = jnp.concatenate([drop, jnp.zeros((tq, 128 - nsel), BF16)], axis=1)
    q_aug = jnp.concatenate([qr, stack(drop), jnp.zeros((rows, HEAD_DIM), BF16)], axis=1)

    def sel_tile(kt, carry, diagonal):
        m_i, acc = carry
        kbase = pl.multiple_of(kt * tk, tk)
        sc = _nt(q_aug, ks_ref[0, 0, pl.ds(kbase, tk), :])
        if diagonal:
            kpos = kbase + lax.broadcasted_iota(jnp.int32, (tq, tk), 1)
            tpos = q0 + lax.broadcasted_iota(jnp.int32, (tq, tk), 0)
            sc = sc + stack(jnp.where(kpos <= tpos, 0.0, NEG_INF))
        m_new = jnp.maximum(m_i, jnp.max(sc, axis=-1, keepdims=True))
        pe = jnp.exp2(sc - m_new)
        acc_new = jnp.exp2(m_i - m_new) * acc + jnp.dot(pe.astype(BF16), vs_ref[0, 0, pl.ds(kbase, tk), :],
                                                        preferred_element_type=F32)
        return m_new, acc_new

    init = (jnp.full((rows, 1), NEG_INF, F32), jnp.zeros((rows, 2 * HEAD_DIM), F32))
    last_kt = (q0 + tq - 1) // tk
    _, acc_s = sel_tile(last_kt, lax.fori_loop(0, last_kt, lambda kt, c: sel_tile(kt, c, False), init), True)
    o_s = acc_s[:, :HEAD_DIM] / acc_s[:, HEAD_DIM:HEAD_DIM + 1]

    gates = gate_ref[0, 0]
    for r in range(NSA_HPG):
        sl = slice(r * tq, (r + 1) * tq)
        o_r = (gates[:, 3 * r:3 * r + 1] * o_c[sl] + gates[:, 3 * r + 1:3 * r + 2] * o_s[sl]
               + gates[:, 3 * r + 2:3 * r + 3] * o_w[sl])
        o_ref[0, :, r * HEAD_DIM:(r + 1) * HEAD_DIM] = o_r


def _nsa_call(qp, qr, kc, vc, ks, vs, kw, vw, gates, ovt):
    bsz, s, _ = qp.shape
    tq = NSA_TQ
    ncmp = kc.shape[2]
    gw = NSA_HPG * HEAD_DIM
    qspec = pl.BlockSpec((1, tq, gw), lambda b, g, i: (b, i, g))
    cspec = pl.BlockSpec((1, 1, ncmp, HEAD_DIM), lambda b, g, i: (b, g, 0, 0))
    kspec = pl.BlockSpec((1, 1, s, HEAD_DIM), lambda b, g, i: (b, g, 0, 0))
    kaspec = pl.BlockSpec((1, 1, s, 4 * HEAD_DIM), lambda b, g, i: (b, g, 0, 0))
    vspec = pl.BlockSpec((1, 1, s, 2 * HEAD_DIM), lambda b, g, i: (b, g, 0, 0))
    return pl.pallas_call(
        _nsa_kernel,
        grid=(bsz, NSA_GROUPS, s // tq),
        in_specs=[qspec, qspec, cspec, cspec, kaspec, vspec, kspec, vspec,
                  pl.BlockSpec((1, 1, tq, 128), lambda b, g, i: (b, g, i, 0)),
                  pl.BlockSpec(ovt.shape, lambda b, g, i: (0, 0))],
        out_specs=pl.BlockSpec((1, tq, gw), lambda b, g, i: (b, i, g)),
        out_shape=jax.ShapeDtypeStruct((bsz, s, NSA_WIDTH), F32),
        compiler_params=_cparams(("parallel", "parallel", "arbitrary")),
        name="nsa",
    )(qp, qr, kc, vc, ks, vs, kw, vw, gates, ovt)


def _softplus(x):
    return jnp.maximum(x, 0.0) + jnp.log1p(jnp.exp(-jnp.abs(x)))


def _ssd_kernel(xbc_ref, z_ref, dt_ref, dtt_ref, cw_ref, cb_ref, dtb_ref, dtbt_ref, al_ref, alt_ref,
                dsk_ref, nw_ref, o_ref, tail_ref, xp_ref, st_ref, y_ref):
    L = SSD_CHUNK
    P = HEAD_DIM
    N = SSD_STATE
    c = pl.program_id(1)

    @pl.when(c == 0)
    def _():
        tail_ref[...] = jnp.zeros(tail_ref.shape, F32)
        st_ref[...] = jnp.zeros(st_ref.shape, F32)

    xin = xbc_ref[0]
    xp_ref[0:8, :] = tail_ref[...]
    xp_ref[8:8 + L, :] = xin
    tail_ref[...] = xin[L - 8:L, :]
    conv = cb_ref[...] + jnp.zeros_like(xin)
    for k in range(SSD_CONV):
        conv = conv + cw_ref[k:k + 1, :] * xp_ref[pl.ds(8 - (SSD_CONV - 1) + k, L), :]
    u = conv * (1.0 / (1.0 + jnp.exp(-conv)))
    xs = u[:, :SSD_WIDTH]

    dt_c = _softplus(dt_ref[0] + dtb_ref[...])
    dt_r = _softplus(dtt_ref[0] + dtbt_ref[...])
    a_c = -jnp.exp(al_ref[...])
    a_r = -jnp.exp(alt_ref[...])
    li = lax.broadcasted_iota(jnp.int32, (L, L), 0)
    si = lax.broadcasted_iota(jnp.int32, (L, L), 1)
    causal = li >= si
    tri = jnp.where(causal, 1.0, 0.0)
    tri_t = jnp.where(li <= si, 1.0, 0.0)
    acs_c = jnp.dot(tri, dt_c * a_c, preferred_element_type=F32, precision=HIGHEST)
    acs_r = jnp.dot(dt_r * a_r, tri_t, preferred_element_type=F32, precision=HIGHEST)

    for g in range(SSD_GROUPS):
        bm = u[:, SSD_WIDTH + g * N:SSD_WIDTH + (g + 1) * N]
        cm = u[:, SSD_WIDTH + SSD_GROUPS * N + g * N:SSD_WIDTH + SSD_GROUPS * N + (g + 1) * N]
        bm_b = bm.astype(BF16)
        cm_b = cm.astype(BF16)
        cb = _nt(cm_b, bm_b)
        bm_t = jnp.transpose(bm)
        for r in range(SSD_HEADS // SSD_GROUPS):
            hh = g * (SSD_HEADS // SSD_GROUPS) + r
            col = acs_c[:, hh:hh + 1]
            row = acs_r[hh:hh + 1, :]
            last = acs_r[hh:hh + 1, L - 1:L]
            lm = jnp.exp(jnp.where(causal, col - row, NEG_INF))
            x_h = xs[:, hh * P:(hh + 1) * P]
            xd = x_h * dt_c[:, hh:hh + 1]
            y_d = jnp.dot((cb * lm).astype(BF16), xd.astype(BF16), preferred_element_type=F32)
            prev = st_ref[hh]
            y_o = jnp.dot(cm_b, prev.astype(BF16), preferred_element_type=F32) * jnp.exp(col)
            dec = jnp.exp(last - row)
            st_new = jnp.dot((bm_t * dec).astype(BF16), xd.astype(BF16), preferred_element_type=F32)
            st_ref[hh] = prev * jnp.exp(last) + st_new
            y_ref[:, hh * P:(hh + 1) * P] = y_d + y_o + x_h * dsk_ref[0:1, hh:hh + 1]

    zz = z_ref[0]
    y = y_ref[...] * (zz * (1.0 / (1.0 + jnp.exp(-zz))))
    ms = jnp.mean(y * y, axis=-1, keepdims=True)
    o_ref[0] = (y * lax.rsqrt(ms + NORM_EPS) * nw_ref[...]).astype(o_ref.dtype)


def _ssd_call(xbc, z, dt, dtt, conv_w, conv_b, dt_bias, a_log, d_skip, norm_w):
    bsz, s, cch = xbc.shape
    L = SSD_CHUNK
    pad = lambda v: jnp.pad(v.reshape(1, -1), ((0, 0), (0, 128 - v.size)))
    full = lambda a: pl.BlockSpec(a.shape, lambda b, i: (0,) * a.ndim)
    args = (xbc, z, dt, dtt, conv_w, conv_b.reshape(1, -1), pad(dt_bias), dt_bias.reshape(-1, 1),
            pad(a_log), a_log.reshape(-1, 1), pad(d_skip), norm_w.reshape(1, -1))
    in_specs = [pl.BlockSpec((1, L, cch), lambda b, i: (b, i, 0)),
                pl.BlockSpec((1, L, SSD_WIDTH), lambda b, i: (b, i, 0)),
                pl.BlockSpec((1, L, 128), lambda b, i: (b, i, 0)),
                pl.BlockSpec((1, SSD_HEADS, L), lambda b, i: (b, 0, i))] + [full(a) for a in args[4:]]
    return pl.pallas_call(
        _ssd_kernel,
        grid=(bsz, s // L),
        in_specs=in_specs,
        out_specs=pl.BlockSpec((1, L, SSD_WIDTH), lambda b, i: (b, i, 0)),
        out_shape=jax.ShapeDtypeStruct((bsz, s, SSD_WIDTH), BF16),
        scratch_shapes=[pltpu.VMEM((8, cch), F32), pltpu.VMEM((8 + L, cch), F32),
                        pltpu.VMEM((SSD_HEADS, SSD_STATE, HEAD_DIM), F32), pltpu.VMEM((L, SSD_WIDTH), F32)],
        compiler_params=_cparams(("parallel", "arbitrary")),
        name="ssd",
    )(*args)


def _outproj_kernel(on_ref, os_ref, x_ref, mod_ref, nnw_ref, n2w_ref, wo_ref, wq_ref, qnw_ref, sk_ref,
                    x1_ref, h2_ref, sc_ref):
    o = on_ref[0]
    ms = jnp.mean(o * o, axis=-1, keepdims=True)
    on = (o * lax.rsqrt(ms + NORM_EPS) * nnw_ref[...]).astype(BF16)
    mix = jnp.dot(jnp.concatenate([on, os_ref[0]], axis=1), wo_ref[...], preferred_element_type=F32)
    x1 = x_ref[0] + mod_ref[0, 2:3, :] * mix
    x1_ref[0] = x1
    ms2 = jnp.mean(x1 * x1, axis=-1, keepdims=True)
    h2 = ((x1 * lax.rsqrt(ms2 + NORM_EPS) * n2w_ref[...]) * (1.0 + mod_ref[0, 4:5, :]) + mod_ref[0, 3:4, :]).astype(BF16)
    h2_ref[0] = h2
    qall = jnp.dot(h2, wq_ref[...], preferred_element_type=F32)
    for hh in range(PEER_HEADS):
        qh = qall[:, hh * PEER_QDIM:(hh + 1) * PEER_QDIM]
        qn = (qh * lax.rsqrt(jnp.mean(qh * qh, axis=-1, keepdims=True) + NORM_EPS) * qnw_ref[...]).astype(BF16)
        both = _nt(sk_ref[hh], qn)
        sc_ref[2 * hh] = both[:PEER_NKEYS]
        sc_ref[2 * hh + 1] = both[PEER_NKEYS:]


def _outproj_call(o_nsa, o_ssd, x, mod, nsa_nw, n2w, w_out, wq, qnw, sub_keys, ts):
    bsz, s, d = x.shape
    nblk = s // ts
    tok = lambda w: pl.BlockSpec((1, ts, w), lambda b, i: (b, i, 0))
    full = lambda a: pl.BlockSpec(a.shape, lambda b, i: (0,) * a.ndim)
    return pl.pallas_call(
        _outproj_kernel,
        grid=(bsz, nblk),
        in_specs=[tok(NSA_WIDTH), tok(SSD_WIDTH), tok(d), pl.BlockSpec((1, 6, d), lambda b, i: (b, 0, 0)),
                  full(nsa_nw), full(n2w), full(w_out), full(wq), full(qnw), full(sub_keys)],
        out_specs=[tok(d), tok(d),
                   pl.BlockSpec((2 * PEER_HEADS, PEER_NKEYS, ts), lambda b, i: (0, 0, b * nblk + i))],
        out_shape=[jax.ShapeDtypeStruct((bsz, s, d), F32), jax.ShapeDtypeStruct((bsz, s, d), BF16),
                   jax.ShapeDtypeStruct((2 * PEER_HEADS, PEER_NKEYS, bsz * s), F32)],
        compiler_params=_cparams(("parallel", "parallel")),
        name="outproj",
    )(o_nsa, o_ssd, x, mod, nsa_nw, n2w, w_out, wq, qnw, sub_keys)


PEER_TT = 128
PEER_HG = 4


def _peersel_kernel(sc_ref, c_ref, e1_ref, r2_ref, e2_ref):
    nk = PEER_NKEYS
    tt = PEER_TT
    K = PEER_TOPK
    kidx = lax.broadcasted_iota(jnp.int32, (nk, tt), 0).astype(F32)
    i16 = lax.broadcasted_iota(jnp.int32, (K, tt), 0).astype(F32)

    def topk_sorted(x, exact_ties, want_rank=True):
        rank = jnp.full((nk, tt), float(K), F32)
        vals = jnp.zeros((K, tt), F32)
        for j in range(K):
            mx = jnp.max(x, axis=0, keepdims=True)
            hit = x == mx
            if exact_ties:
                hit = kidx == jnp.min(jnp.where(hit, kidx, float(nk)), axis=0, keepdims=True)
            if want_rank:
                rank = jnp.where(hit, float(j), rank)
            x = jnp.where(hit, -jnp.inf, x)
            vals = jnp.where(i16 == float(j), mx, vals)
        taken = jnp.sum(jnp.where(x == -jnp.inf, 1.0, 0.0), axis=0, keepdims=True)
        return rank, vals, taken

    def first_stage(hh0, nh):
        ss = [sc_ref[2 * hh0 + i] for i in range(2 * nh)]
        fast = [topk_sorted(s, False, want_rank=i % 2 == 1) for i, s in enumerate(ss)]
        most = fast[0][2]
        for f in fast[1:]:
            most = jnp.maximum(most, f[2])

        def exact():
            out = ()
            for i in range(nh):
                r1, v1a, _ = topk_sorted(ss[2 * i], True)
                r2, v2a, _ = topk_sorted(ss[2 * i + 1], True)
                out += (r1, i16, v1a, r2, v2a)
            return out

        def tie_free():
            out = ()
            for i in range(nh):
                out += (ss[2 * i], fast[2 * i][1], fast[2 * i][1], fast[2 * i + 1][0], fast[2 * i + 1][1])
            return out

        res = lax.cond(jnp.max(most) > float(K), exact, tie_free)
        return [(ss[2 * i], ss[2 * i + 1]) + tuple(res[5 * i:5 * i + 5]) for i in range(nh)]

    def second_stage(hh, s1, s2, key1, match1, v1a, r2, v2a):
        v1 = [v1a[i:i + 1, :] for i in range(K)]
        v2 = [v2a[i:i + 1, :] for i in range(K)]
        cmax = v1[0] + v2[0]
        n = jnp.zeros((K, tt), F32)
        f = v1a + v2[0]
        zsum = jnp.zeros((1, tt), F32)
        for _ in range(K):
            mx = jnp.max(f, axis=0, keepdims=True)
            iw = jnp.min(jnp.where(f == mx, i16, float(K)), axis=0, keepdims=True)
            hit = i16 == iw
            n = n + jnp.where(hit, 1.0, 0.0)
            zsum = zsum + jnp.exp(mx - cmax)
            nstar = jnp.sum(jnp.where(hit, n, 0.0), axis=0, keepdims=True)
            v2n = jnp.sum(jnp.where(i16 == nstar, v2a, 0.0), axis=0, keepdims=True)
            f = jnp.where(hit, v1a + v2n, f)
        cnt = jnp.zeros((nk, tt), F32)
        for i in range(K):
            cnt = jnp.where(key1 == match1[i:i + 1, :], n[i:i + 1, :], cnt)
        c_ref[hh] = cnt
        e1_ref[hh] = jnp.exp(s1 - v1[0]) / zsum
        r2_ref[hh] = r2.astype(BF16)
        e2_ref[hh] = jnp.exp(s2 - v2[0]).astype(BF16)

    for hh0 in range(0, PEER_HEADS, PEER_HG):
        for i, args in enumerate(first_stage(hh0, PEER_HG)):
            second_stage(hh0 + i, *args)


def _peersel_call(sc):
    _, nk, t = sc.shape
    tt = PEER_TT
    ospec = pl.BlockSpec((PEER_HEADS, nk, tt), lambda i: (0, 0, i))
    sd = lambda dt: jax.ShapeDtypeStruct((PEER_HEADS, nk, t), dt)
    return pl.pallas_call(
        _peersel_kernel,
        grid=(t // tt,),
        in_specs=[pl.BlockSpec((2 * PEER_HEADS, nk, tt), lambda i: (0, 0, i))],
        out_specs=[ospec] * 4,
        out_shape=[sd(F32), sd(F32), sd(BF16), sd(BF16)],
        compiler_params=_cparams(("parallel",)),
        name="peersel",
    )(sc)


PEER_TB = 512
PEER_EC = 1024


def _peer_kernel(h2_ref, c_ref, e1_ref, r2_ref, e2_ref, down0_ref, downn_ref, upt_ref, x1_ref, mod_ref, fw_ref,
                 o_ref, acc_ref, act_ref, w_ref):
    j = pl.program_id(1)
    nj = pl.num_programs(1)
    nk = PEER_NKEYS
    na = PEER_EC // nk

    def build_gates(chunk):
        a0 = pl.multiple_of(chunk * na, na)

        def rows(ref, hh, ai):
            grp = ref[hh, pl.ds(a0, na), :]
            r16 = jnp.broadcast_to(grp[ai:ai + 1, :], (16, PEER_TB)).astype(BF16)
            return jnp.concatenate([r16] * (nk // 16), axis=0)

        for ai in range(na):
            w = None
            for hh in range(PEER_HEADS):
                term = jnp.where(r2_ref[hh] < rows(c_ref, hh, ai), e2_ref[hh], 0.0) * rows(e1_ref, hh, ai)
                w = term if w is None else w + term
            w_ref[ai * nk:(ai + 1) * nk, :] = w

    @pl.when(j == 0)
    def _():
        acc_ref[...] = jnp.zeros(acc_ref.shape, F32)

    @pl.when((pl.program_id(0) == 0) & (j == 0))
    def _():
        act_ref[...] = _gelu_tanh(_nt(down0_ref[...], h2_ref[...]).astype(BF16))
        build_gates(0)

    wa = w_ref[...] * act_ref[...]
    acc_ref[...] += jnp.dot(upt_ref[...], wa, preferred_element_type=F32)
    act_ref[...] = _gelu_tanh(_nt(downn_ref[...], h2_ref[...]).astype(BF16))
    build_gates(jnp.where(j == nj - 1, 0, j + 1))

    @pl.when(j == nj - 1)
    def _():
        y = jnp.transpose(acc_ref[...])
        x2 = x1_ref[...] + mod_ref[0, 5:6, :] * y
        ms = jnp.mean(x2 * x2, axis=-1, keepdims=True)
        o_ref[...] = x2 * lax.rsqrt(ms + NORM_EPS) * fw_ref[...]


def _peer_call(h2, cnt, e1, r2, e2, down, upt, x1, mod, fw, s):
    t, d = h2.shape
    ne = down.shape[0]
    tb, ec = PEER_TB, PEER_EC
    per_b = s // tb
    nj = ne // ec
    nblk = t // tb

    def prep(i, j):
        return jnp.minimum(i + (j == nj - 1).astype(jnp.int32), nblk - 1)

    dspec = pl.BlockSpec((PEER_HEADS, PEER_NKEYS, tb), lambda i, j: (0, 0, prep(i, j)))
    return pl.pallas_call(
        _peer_kernel,
        grid=(nblk, nj),
        in_specs=[pl.BlockSpec((tb, d), lambda i, j: (prep(i, j), 0)), dspec, dspec, dspec, dspec,
                  pl.BlockSpec((ec, d), lambda i, j: (0, 0)),
                  pl.BlockSpec((ec, d), lambda i, j: ((j + 1) % nj, 0)),
                  pl.BlockSpec((d, ec), lambda i, j: (0, j)),
                  pl.BlockSpec((tb, d), lambda i, j: (i, 0)),
                  pl.BlockSpec((1, 6, d), lambda i, j: (i // per_b, 0, 0)),
                  pl.BlockSpec((1, d), lambda i, j: (0, 0))],
        out_specs=pl.BlockSpec((tb, d), lambda i, j: (i, 0)),
        out_shape=jax.ShapeDtypeStruct((t, d), F32),
        scratch_shapes=[pltpu.VMEM((d, tb), F32), pltpu.VMEM((ec, tb), BF16), pltpu.VMEM((ec, tb), BF16)],
        compiler_params=_cparams(("arbitrary", "arbitrary")),
        name="peer",
    )(h2, cnt, e1, r2, e2, down, down, upt, x1, mod, fw)


def _rope_tables(s):
    half = ROPE_DIM // 2
    inv_freq = ROPE_THETA ** (-jnp.arange(half, dtype=F32) / half)
    ang = jnp.arange(s).astype(F32)[:, None] * inv_freq[None, :]
    cos, sin = jnp.cos(ang), jnp.sin(ang)
    one = jnp.ones((s, HEAD_DIM - ROPE_DIM), F32)
    cos64 = jnp.concatenate([cos, cos, one], axis=1)
    sin64 = jnp.concatenate([-sin, sin, 0.0 * one], axis=1)
    return jnp.concatenate([cos64, cos64], axis=1), jnp.concatenate([sin64, sin64], axis=1)


def _swap_cols(w):
    d, n = w.shape
    wh = w.reshape(d, n // HEAD_DIM, HEAD_DIM)
    half = ROPE_DIM // 2
    sw = jnp.concatenate([wh[..., half:ROPE_DIM], wh[..., :half], jnp.zeros_like(wh[..., ROPE_DIM:])], axis=-1)
    return sw.reshape(d, n)


def _pack_w_in(w):
    o = np.cumsum((0, 512, 128, 128, 128, 128, 128, 128, 24, 512, 1024, 8))
    q, kc, vc, ksel, vsel, kwin, vwin, gl, z, xbc, dtr = (w[:, o[i]:o[i + 1]] for i in range(11))
    gl = gl.reshape(-1, NSA_GROUPS, NSA_HPG * N_BRANCH)
    gl = jnp.pad(gl, ((0, 0), (0, 0), (0, 128 - NSA_HPG * N_BRANCH))).reshape(-1, 256)
    dtr = jnp.pad(dtr, ((0, 0), (0, 128 - SSD_HEADS)))
    cols = [q, _swap_cols(q), ksel, _swap_cols(ksel), kwin, _swap_cols(kwin), vsel, vwin, kc, vc, gl, z, xbc, dtr]
    return jnp.concatenate(cols, axis=1).astype(BF16)


def _overlap_t(s):
    n_cmp_pad = s // CMP_STRIDE
    n_sel = s // SEL_BLOCK
    cs = np.arange(n_cmp_pad) * CMP_STRIDE
    ss = np.arange(n_sel) * SEL_BLOCK
    ov = np.maximum(np.minimum(cs[None, :] + CMP_BLOCK, ss[:, None] + SEL_BLOCK)
                    - np.maximum(cs[None, :], ss[:, None]), 0).astype(np.float32) / CMP_BLOCK
    ov[:, n_cmp_pad - 1] = 0.0
    return jnp.asarray(ov, BF16)


def _blockdiag_keys(sub_keys):
    h, two, n, half = sub_keys.shape
    z = jnp.zeros((h, n, half), sub_keys.dtype)
    top = jnp.concatenate([sub_keys[:, 0], z], axis=2)
    bot = jnp.concatenate([z, sub_keys[:, 1]], axis=2)
    return jnp.concatenate([top, bot], axis=1).astype(BF16)


def kernel(x, c, w_ada, b_ada, norm1_w, w_in, cmp_pe_k, cmp_pe_v, cmp_w1_k, cmp_w2_k, cmp_w1_v, cmp_w2_v,
           nsa_norm_w, conv_w, conv_b, dt_bias, a_log, d_skip, ssd_norm_w, w_out, norm2_w,
           peer_wq, peer_qnorm_w, peer_sub_keys, peer_down, peer_up, final_norm_w):
    bsz, s, d = x.shape
    assert d == D_MODEL and s % NSA_TK == 0 and s // SEL_BLOCK <= 128 and w_ada.shape[0] == 1
    lyr = 0
    ts = 512

    mod = _mod_call(c, w_ada[lyr], b_ada[lyr]).reshape(bsz, 6, d)
    cos_t, sin_t = _rope_tables(s)
    (qp, qr, ksel, kwin, vsel, vwin, kcmp, vcmp, gates, z, xbc, dtr) = _inproj_call(
        x, mod, norm1_w[lyr].reshape(1, d), _pack_w_in(w_in[lyr]), cos_t, sin_t, ts)

    def w1_groups(w1):
        w = w1.reshape(2, CMP_STRIDE, HEAD_DIM, CMP_HIDDEN)
        z = jnp.zeros_like(w)
        per_g = [jnp.concatenate([w, z], axis=2), jnp.concatenate([z, w], axis=2)]
        return jnp.stack(per_g, axis=1).reshape(2, NSA_GROUPS, CMP_STRIDE * 2 * HEAD_DIM, CMP_HIDDEN).astype(BF16)

    kc, vc = _compress_call(
        kcmp, vcmp, jnp.tile(cmp_pe_k[lyr], (1, NSA_GROUPS)), jnp.tile(cmp_pe_v[lyr], (1, NSA_GROUPS)),
        w1_groups(cmp_w1_k[lyr]), cmp_w2_k[lyr].astype(BF16), w1_groups(cmp_w1_v[lyr]), cmp_w2_v[lyr].astype(BF16))

    o_nsa = _nsa_call(qp, qr, kc, vc, ksel, vsel, kwin, vwin, gates, _overlap_t(s))

    dtt = jnp.transpose(dtr[:, :, :SSD_HEADS], (0, 2, 1))
    o_ssd = _ssd_call(xbc, z, dtr, dtt, conv_w[lyr], conv_b[lyr], dt_bias[lyr], a_log[lyr], d_skip[lyr],
                      ssd_norm_w[lyr])

    x1, h2, sc = _outproj_call(
        o_nsa, o_ssd, x, mod, nsa_norm_w[lyr].reshape(1, -1), norm2_w[lyr].reshape(1, d),
        w_out[lyr].astype(BF16), peer_wq[lyr].astype(BF16), peer_qnorm_w[lyr].reshape(1, -1),
        _blockdiag_keys(peer_sub_keys[lyr]), ts)

    cnt, e1, r2, e2 = _peersel_call(sc)

    out = _peer_call(h2.reshape(bsz * s, d), cnt, e1, r2, e2, peer_down[lyr].astype(BF16),
                     jnp.transpose(peer_up[lyr]).astype(BF16), x1.reshape(bsz * s, d), mod,
                     final_norm_w.reshape(1, d), s)
    return out.reshape(bsz, s, d)
```

```python
import functools
import math

import numpy as np
import jax
import jax.numpy as jnp
from jax import lax
from jax.experimental import pallas as pl
from jax.experimental.pallas import tpu as pltpu

F32 = jnp.float32
BF16 = jnp.bfloat16
HIGHEST = lax.Precision.HIGHEST

D_MODEL = 1024
NSA_WIDTH = 512
SSD_WIDTH = 512
HEAD_DIM = 64
NSA_HEADS = 8
NSA_GROUPS = 2
NSA_HPG = 4
N_BRANCH = 3
CMP_BLOCK = 32
CMP_STRIDE = 16
CMP_HIDDEN = 256
SEL_BLOCK = 64
SEL_SHIFT = SEL_BLOCK.bit_length() - 1
SEL_TOPN = 16
WINDOW = 512
ROPE_THETA = 500000.0
ROPE_DIM = 16
SSD_HEADS = 8
SSD_GROUPS = 2
SSD_STATE = 128
SSD_CONV = 4
SSD_CHUNK = 128
PEER_HEADS = 8
PEER_NKEYS = 128
PEER_QDIM = 256
PEER_TOPK = 16
NORM_EPS = 1e-6
NEG_INF = -1e30
LOG2E = math.log2(math.e)
FORCE_BONUS = 1e4

LANES = 128
VMEM_LIMIT = 56 * 1024 * 1024

C_Q, C_QSW, C_KSEL, C_KSELSW, C_KWIN, C_KWINSW = 0, 512, 1024, 1152, 1280, 1408
C_VSEL, C_VWIN, C_KCMP, C_VCMP, C_GATE, C_Z, C_XBC, C_DT, C_END = 1536, 1664, 1792, 1920, 2048, 2304, 2816, 3840, 3968


def _gelu_tanh(x):
    c = math.sqrt(2.0 / math.pi)
    return 0.5 * x * (1.0 + jnp.tanh(c * (x + 0.044715 * (x * x * x))))


def _nt(a, b):
    return lax.dot_general(a, b, (((1,), (1,)), ((), ())), preferred_element_type=F32)


def _cparams(sem):
    return pltpu.CompilerParams(dimension_semantics=sem, vmem_limit_bytes=VMEM_LIMIT)


def _mod_kernel(c_ref, w_ref, b_ref, o_ref):
    o_ref[...] = jnp.dot(c_ref[...], w_ref[...], preferred_element_type=F32, precision=HIGHEST) + b_ref[...]


def _mod_call(c, w_ada, b_ada):
    bsz = c.shape[0]
    n = w_ada.shape[1]
    return pl.pallas_call(
        _mod_kernel,
        grid=(n // D_MODEL,),
        in_specs=[pl.BlockSpec((bsz, D_MODEL), lambda j: (0, 0)),
                  pl.BlockSpec((D_MODEL, D_MODEL), lambda j: (0, j)),
                  pl.BlockSpec((1, D_MODEL), lambda j: (0, j))],
        out_specs=pl.BlockSpec((bsz, D_MODEL), lambda j: (0, j)),
        out_shape=jax.ShapeDtypeStruct((bsz, n), F32),
        compiler_params=_cparams(("arbitrary",)),
        name="mod",
    )(c, w_ada, b_ada.reshape(1, n))


def _inproj_kernel(x_ref, mod_ref, nw_ref, w_ref, cos_ref, sin_ref,
                   qp_ref, qr_ref, ksel_ref, kwin_ref, vsel_ref, vwin_ref,
                   kcmp_ref, vcmp_ref, gate_ref, z_ref, xbc_ref, dt_ref):
    x = x_ref[0]
    ms = jnp.mean(x * x, axis=-1, keepdims=True)
    y = x * lax.rsqrt(ms + NORM_EPS) * nw_ref[...]
    h = (y * (1.0 + mod_ref[0, 1:2, :]) + mod_ref[0, 0:1, :]).astype(BF16)

    def proj(lo, hi):
        return jnp.dot(h, w_ref[:, lo:hi], preferred_element_type=F32)

    cos = cos_ref[...]
    sin = sin_ref[...]
    scale = HEAD_DIM ** -0.5 * LOG2E
    qa = proj(C_Q, C_KSEL)
    kv = proj(C_KSEL, C_GATE)
    rest = proj(C_GATE, C_END)
    q = qa[:, :512]
    qsw = qa[:, 512:]
    qp_ref[0] = (q * scale).astype(BF16)
    cos4 = jnp.concatenate([cos] * 4, axis=1)
    sin4 = jnp.concatenate([sin] * 4, axis=1)
    qr_ref[0] = ((q * cos4 + qsw * sin4) * scale).astype(BF16)

    part = lambda c: kv[:, c - C_KSEL:c - C_KSEL + 128]
    ks = part(C_KSEL) * cos + part(C_KSELSW) * sin
    kw = part(C_KWIN) * cos + part(C_KWINSW) * sin
    vs = part(C_VSEL)
    vw = part(C_VWIN)
    kwin_ref[0, 0] = kw[:, :HEAD_DIM].astype(BF16)
    kwin_ref[0, 1] = kw[:, HEAD_DIM:].astype(BF16)
    lane = lax.broadcasted_iota(jnp.int32, vs.shape, 1)
    blk = jnp.right_shift(pl.program_id(1) * ks.shape[0] + lax.broadcasted_iota(jnp.int32, ks.shape, 0), SEL_SHIFT)
    hot_lo = jnp.where(lane - HEAD_DIM == blk, 1.0, 0.0)
    hot_hi = jnp.where(lane + HEAD_DIM == blk, 1.0, 0.0).astype(BF16)
    ksel_ref[0, 0, :, 0:128] = jnp.where(lane < HEAD_DIM, ks, hot_lo).astype(BF16)
    ksel_ref[0, 1, :, 0:128] = jnp.where(lane < HEAD_DIM, pltpu.roll(ks, HEAD_DIM, axis=1), hot_lo).astype(BF16)
    ksel_ref[0, 0, :, 128:256] = hot_hi
    ksel_ref[0, 1, :, 128:256] = hot_hi
    ones_col = jnp.where(lane == HEAD_DIM, 1.0, 0.0)
    for ref, val in ((vsel_ref, vs), (vwin_ref, vw)):
        ref[0, 0] = jnp.where(lane < HEAD_DIM, val, ones_col).astype(BF16)
        ref[0, 1] = jnp.where(lane < HEAD_DIM, pltpu.roll(val, HEAD_DIM, axis=1), ones_col).astype(BF16)
    kcmp_ref[0] = part(C_KCMP)
    vcmp_ref[0] = part(C_VCMP)
    sg = 1.0 / (1.0 + jnp.exp(-rest[:, :C_Z - C_GATE]))
    gate_ref[0, 0] = sg[:, :128]
    gate_ref[0, 1] = sg[:, 128:]
    z_ref[0] = rest[:, C_Z - C_GATE:C_XBC - C_GATE]
    xbc_ref[0] = rest[:, C_XBC - C_GATE:C_DT - C_GATE]
    dt_ref[0] = rest[:, C_DT - C_GATE:]


def _inproj_call(x, mod, norm_w, w_ext, cos_t, sin_t, ts):
    bsz, s, d = x.shape
    grid = (bsz, s // ts)
    tok = lambda w: pl.BlockSpec((1, ts, w), lambda b, i: (b, i, 0))
    hm = pl.BlockSpec((1, NSA_GROUPS, ts, HEAD_DIM), lambda b, i: (b, 0, i, 0))
    hv = pl.BlockSpec((1, NSA_GROUPS, ts, 128), lambda b, i: (b, 0, i, 0))
    hk = pl.BlockSpec((1, NSA_GROUPS, ts, 256), lambda b, i: (b, 0, i, 0))
    sd = jax.ShapeDtypeStruct
    out_shape = [sd((bsz, s, 512), BF16), sd((bsz, s, 512), BF16)] + \
                [sd((bsz, NSA_GROUPS, s, 256), BF16), sd((bsz, NSA_GROUPS, s, HEAD_DIM), BF16)] + \
                [sd((bsz, NSA_GROUPS, s, 128), BF16)] * 2 + \
                [sd((bsz, s, 128), F32), sd((bsz, s, 128), F32),
                 sd((bsz, NSA_GROUPS, s, 128), F32),
                 sd((bsz, s, 512), F32), sd((bsz, s, 1024), F32), sd((bsz, s, 128), F32)]
    out_specs = [tok(512), tok(512), hk, hm, hv, hv, tok(128), tok(128),
                 pl.BlockSpec((1, NSA_GROUPS, ts, 128), lambda b, i: (b, 0, i, 0)),
                 tok(512), tok(1024), tok(128)]
    return pl.pallas_call(
        _inproj_kernel,
        grid=grid,
        in_specs=[tok(d),
                  pl.BlockSpec((1, 6, d), lambda b, i: (b, 0, 0)),
                  pl.BlockSpec((1, d), lambda b, i: (0, 0)),
                  pl.BlockSpec((d, C_END), lambda b, i: (0, 0)),
                  pl.BlockSpec((ts, 128), lambda b, i: (i, 0)),
                  pl.BlockSpec((ts, 128), lambda b, i: (i, 0))],
        out_specs=out_specs,
        out_shape=out_shape,
        compiler_params=_cparams(("parallel", "parallel")),
        name="inproj",
    )(x, mod, norm_w, w_ext, cos_t, sin_t)


def _compress_kernel(k_ref, v_ref, pek_ref, pev_ref, w1k_ref, w2k_ref, w1v_ref, w2v_ref,
                     kc_ref, vc_ref, hb_ref):
    nch = k_ref.shape[1] // CMP_STRIDE

    def one(x_ref, pe_ref, w1_ref, w2_ref, o_ref):
        xs = [x_ref[0, pl.ds(l, nch, stride=CMP_STRIDE), :] for l in range(CMP_STRIDE)]
        top = jnp.concatenate([(xs[l] + pe_ref[l:l + 1, :]).astype(BF16) for l in range(CMP_STRIDE)], axis=1)
        bot = jnp.concatenate([(xs[l] + pe_ref[CMP_STRIDE + l:CMP_STRIDE + l + 1, :]).astype(BF16)
                               for l in range(CMP_STRIDE)], axis=1)
        htop = jnp.dot(top, w1_ref[0, 0], preferred_element_type=F32)
        hb_ref[0:nch, :] = jnp.dot(bot, w1_ref[1, 0], preferred_element_type=F32)
        hb_ref[nch:nch + 8, :] = jnp.zeros((8, CMP_HIDDEN), F32)
        pre = htop + hb_ref[pl.ds(1, nch), :]
        out = jnp.dot(_gelu_tanh(pre).astype(BF16), w2_ref[...], preferred_element_type=F32)
        row = lax.broadcasted_iota(jnp.int32, out.shape, 0)
        o_ref[0, 0] = jnp.where(row < nch - 1, out, 0.0).astype(BF16)

    one(k_ref, pek_ref, w1k_ref, w2k_ref, kc_ref)
    one(v_ref, pev_ref, w1v_ref, w2v_ref, vc_ref)


def _compress_call(kcmp, vcmp, pek, pev, w1k, w2k, w1v, w2v):
    bsz, s, cw = kcmp.shape
    nch = s // CMP_STRIDE
    xs = pl.BlockSpec((1, s, cw), lambda b, gi: (b, 0, 0))
    full = lambda a: pl.BlockSpec(a.shape, lambda b, gi: (0,) * a.ndim)
    w1s = pl.BlockSpec((2, 1) + w1k.shape[2:], lambda b, gi: (0, gi, 0, 0))
    os_ = pl.BlockSpec((1, 1, nch, HEAD_DIM), lambda b, gi: (b, gi, 0, 0))
    sd = jax.ShapeDtypeStruct((bsz, NSA_GROUPS, nch, HEAD_DIM), BF16)
    return pl.pallas_call(
        _compress_kernel,
        grid=(bsz, NSA_GROUPS),
        in_specs=[xs, xs, full(pek), full(pev), w1s, full(w2k), w1s, full(w2v)],
        out_specs=[os_, os_],
        out_shape=[sd, sd],
        scratch_shapes=[pltpu.VMEM((nch + 8, CMP_HIDDEN), F32)],
        compiler_params=_cparams(("parallel", "parallel")),
        name="compress",
    )(kcmp, vcmp, pek, pev, w1k, w2k, w1v, w2v)


NSA_TQ = 256
NSA_TK = 1024


def _nsa_kernel(qp_ref, qr_ref, kc_ref, vc_ref, ks_ref, vs_ref, kw_ref, vw_ref, gate_ref, ovt_ref, o_ref):
    tq = NSA_TQ
    tk = NSA_TK
    rows = NSA_HPG * tq
    qt = pl.program_id(2)
    q0 = qt * tq
    ncmp = kc_ref.shape[2]
    nsel = ovt_ref.shape[0]
    stack = lambda a: jnp.concatenate([a] * NSA_HPG, axis=0)
    qp = jnp.concatenate([qp_ref[0, :, r * HEAD_DIM:(r + 1) * HEAD_DIM] for r in range(NSA_HPG)], axis=0)
    qr = jnp.concatenate([qr_ref[0, :, r * HEAD_DIM:(r + 1) * HEAD_DIM] for r in range(NSA_HPG)], axis=0)

    t_row = q0 + lax.broadcasted_iota(jnp.int32, (tq, ncmp), 0)
    cend = lax.broadcasted_iota(jnp.int32, (tq, ncmp), 1) * CMP_STRIDE + (CMP_BLOCK - 1)
    cbias = stack(jnp.where(cend <= t_row, 0.0, NEG_INF))
    s = jnp.where(cbias == 0.0, _nt(qp, kc_ref[0, 0]), NEG_INF)
    e = jnp.exp2(s - jnp.max(s, axis=-1, keepdims=True))
    p = jnp.where(cbias == 0.0, e / jnp.sum(e, axis=-1, keepdims=True), 0.0)
    o_c = jnp.dot(p.astype(BF16), vc_ref[0, 0], preferred_element_type=F32)

    psum = p[0:tq] + p[tq:2 * tq] + p[2 * tq:3 * tq] + p[3 * tq:4 * tq]
    p_hi = psum.astype(BF16)
    p_lo = (psum - p_hi.astype(F32)).astype(BF16)
    ovt = ovt_ref[...]
    imp = _nt(ovt, p_hi) + _nt(ovt, p_lo)
    jblk = lax.broadcasted_iota(jnp.int32, (nsel, tq), 0)
    tt = q0 + lax.broadcasted_iota(jnp.int32, (nsel, tq), 1)
    blk_t = jnp.right_shift(tt, SEL_SHIFT)
    forced = (jblk == 0) | (jblk == blk_t) | (jblk == blk_t - 1)
    xs = jnp.where(jblk * SEL_BLOCK <= tt, imp + jnp.where(forced, FORCE_BONUS, 0.0), NEG_INF)
    drop_t = jnp.full((nsel, tq), NEG_INF, F32)
    jblk_f = jblk.astype(F32)
    for _ in range(SEL_TOPN):
        mx = jnp.max(xs, axis=0, keepdims=True)
        idx = jnp.min(jnp.where(xs == mx, jblk_f, float(nsel)), axis=0, keepdims=True)
        hit = jblk_f == idx
        drop_t = jnp.where(hit, 0.0, drop_t)
        xs = jnp.where(hit, -jnp.inf, xs)
    drop = jnp.transpose(drop_t).astype(BF16)

    wk = WINDOW + tq
    wstart = pl.multiple_of(jnp.maximum(q0 - WINDOW, 0), tq)
    tq_w = q0 + lax.broadcasted_iota(jnp.int32, (tq, wk), 0)
    kpos_w = wstart + lax.broadcasted_iota(jnp.int32, (tq, wk), 1)
    bias_w = jnp.where((kpos_w <= tq_w) & (tq_w - kpos_w < WINDOW), 0.0, NEG_INF)
    sw = _nt(qr, kw_ref[0, 0, pl.ds(wstart, wk), :]) + stack(bias_w)
    ew = jnp.exp2(sw - jnp.max(sw, axis=-1, keepdims=True))
    pv_w = jnp.dot(ew.astype(BF16), vw_ref[0, 0, pl.ds(wstart, wk), :], preferred_element_type=F32)
    o_w = pv_w[:, :HEAD_DIM] / pv_w[:, HEAD_DIM:HEAD_DIM + 1]

    if nsel < LANES:
        drop = jnp.concatenate([drop, jnp.zeros((tq, LANES - nsel), BF16)], axis=1)
    q_aug = jnp.concatenate([qr, stack(drop), jnp.zeros((rows, HEAD_DIM), BF16)], axis=1)

    def sel_tile(kt, carry, diagonal):
        m_i, acc = carry
        kbase = pl.multiple_of(kt * tk, tk)
        sc = _nt(q_aug, ks_ref[0, 0, pl.ds(kbase, tk), :])
        if diagonal:
            kpos = kbase + lax.broadcasted_iota(jnp.int32, (tq, tk), 1)
            tpos = q0 + lax.broadcasted_iota(jnp.int32, (tq, tk), 0)
            sc = sc + stack(jnp.where(kpos <= tpos, 0.0, NEG_INF))
        m_new = jnp.maximum(m_i, jnp.max(sc, axis=-1, keepdims=True))
        pe = jnp.exp2(sc - m_new)
        acc_new = jnp.exp2(m_i - m_new) * acc + jnp.dot(pe.astype(BF16), vs_ref[0, 0, pl.ds(kbase, tk), :],
                                                        preferred_element_type=F32)
        return m_new, acc_new

    init = (jnp.full((rows, 1), NEG_INF, F32), jnp.zeros((rows, 2 * HEAD_DIM), F32))
    last_kt = (q0 + tq - 1) // tk
    _, acc_s = sel_tile(last_kt, lax.fori_loop(0, last_kt, lambda kt, c: sel_tile(kt, c, False), init), True)
    o_s = acc_s[:, :HEAD_DIM] / acc_s[:, HEAD_DIM:HEAD_DIM + 1]

    gates = gate_ref[0, 0]
    for r in range(NSA_HPG):
        sl = slice(r * tq, (r + 1) * tq)
        o_r = (gates[:, 3 * r:3 * r + 1] * o_c[sl] + gates[:, 3 * r + 1:3 * r + 2] * o_s[sl]
               + gates[:, 3 * r + 2:3 * r + 3] * o_w[sl])
        o_ref[0, :, r * HEAD_DIM:(r + 1) * HEAD_DIM] = o_r


def _nsa_call(qp, qr, kc, vc, ks, vs, kw, vw, gates, ovt):
    bsz, s, _ = qp.shape
    tq = NSA_TQ
    ncmp = kc.shape[2]
    gw = NSA_HPG * HEAD_DIM
    qspec = pl.BlockSpec((1, tq, gw), lambda b, g, i: (b, i, g))
    cspec = pl.BlockSpec((1, 1, ncmp, HEAD_DIM), lambda b, g, i: (b, g, 0, 0))
    kspec = pl.BlockSpec((1, 1, s, HEAD_DIM), lambda b, g, i: (b, g, 0, 0))
    kaspec = pl.BlockSpec((1, 1, s, 4 * HEAD_DIM), lambda b, g, i: (b, g, 0, 0))
    vspec = pl.BlockSpec((1, 1, s, 2 * HEAD_DIM), lambda b, g, i: (b, g, 0, 0))
    return pl.pallas_call(
        _nsa_kernel,
        grid=(bsz, NSA_GROUPS, s // tq),
        in_specs=[qspec, qspec, cspec, cspec, kaspec, vspec, kspec, vspec,
                  pl.BlockSpec((1, 1, tq, 128), lambda b, g, i: (b, g, i, 0)),
                  pl.BlockSpec(ovt.shape, lambda b, g, i: (0, 0))],
        out_specs=pl.BlockSpec((1, tq, gw), lambda b, g, i: (b, i, g)),
        out_shape=jax.ShapeDtypeStruct((bsz, s, NSA_WIDTH), F32),
        compiler_params=_cparams(("parallel", "parallel", "arbitrary")),
        name="nsa",
    )(qp, qr, kc, vc, ks, vs, kw, vw, gates, ovt)


def _softplus(x):
    return jnp.maximum(x, 0.0) + jnp.log1p(jnp.exp(-jnp.abs(x)))


def _ssd_kernel(xbc_ref, z_ref, dt_ref, dtt_ref, cw_ref, cb_ref, dtb_ref, dtbt_ref, al_ref, alt_ref,
                dsk_ref, nw_ref, o_ref, tail_ref, xp_ref, st_ref, y_ref):
    L = SSD_CHUNK
    P = HEAD_DIM
    N = SSD_STATE
    c = pl.program_id(1)

    @pl.when(c == 0)
    def _():
        tail_ref[...] = jnp.zeros(tail_ref.shape, F32)
        st_ref[...] = jnp.zeros(st_ref.shape, F32)

    xin = xbc_ref[0]
    xp_ref[0:8, :] = tail_ref[...]
    xp_ref[8:8 + L, :] = xin
    tail_ref[...] = xin[L - 8:L, :]
    conv = cb_ref[...] + jnp.zeros_like(xin)
    for k in range(SSD_CONV):
        conv = conv + cw_ref[k:k + 1, :] * xp_ref[pl.ds(8 - (SSD_CONV - 1) + k, L), :]
    u = conv * (1.0 / (1.0 + jnp.exp(-conv)))
    xs = u[:, :SSD_WIDTH]

    dt_c = _softplus(dt_ref[0] + dtb_ref[...])
    dt_r = _softplus(dtt_ref[0] + dtbt_ref[...])
    a_c = -jnp.exp(al_ref[...])
    a_r = -jnp.exp(alt_ref[...])
    li = lax.broadcasted_iota(jnp.int32, (L, L), 0)
    si = lax.broadcasted_iota(jnp.int32, (L, L), 1)
    causal = li >= si
    tri = jnp.where(causal, 1.0, 0.0)
    tri_t = jnp.where(li <= si, 1.0, 0.0)
    acs_c = jnp.dot(tri, dt_c * a_c, preferred_element_type=F32, precision=HIGHEST)
    acs_r = jnp.dot(dt_r * a_r, tri_t, preferred_element_type=F32, precision=HIGHEST)

    for g in range(SSD_GROUPS):
        bm = u[:, SSD_WIDTH + g * N:SSD_WIDTH + (g + 1) * N]
        cm = u[:, SSD_WIDTH + SSD_GROUPS * N + g * N:SSD_WIDTH + SSD_GROUPS * N + (g + 1) * N]
        bm_b = bm.astype(BF16)
        cm_b = cm.astype(BF16)
        cb = _nt(cm_b, bm_b)
        bm_t = jnp.transpose(bm)
        for r in range(SSD_HEADS // SSD_GROUPS):
            hh = g * (SSD_HEADS // SSD_GROUPS) + r
            col = acs_c[:, hh:hh + 1]
            row = acs_r[hh:hh + 1, :]
            last = acs_r[hh:hh + 1, L - 1:L]
            lm = jnp.exp(jnp.where(causal, col - row, NEG_INF))
            x_h = xs[:, hh * P:(hh + 1) * P]
            xd = x_h * dt_c[:, hh:hh + 1]
            y_d = jnp.dot((cb * lm).astype(BF16), xd.astype(BF16), preferred_element_type=F32)
            prev = st_ref[hh]
            y_o = jnp.dot(cm_b, prev.astype(BF16), preferred_element_type=F32) * jnp.exp(col)
            dec = jnp.exp(last - row)
            st_new = jnp.dot((bm_t * dec).astype(BF16), xd.astype(BF16), preferred_element_type=F32)
            st_ref[hh] = prev * jnp.exp(last) + st_new
            y_ref[:, hh * P:(hh + 1) * P] = y_d + y_o + x_h * dsk_ref[0:1, hh:hh + 1]

    zz = z_ref[0]
    y = y_ref[...] * (zz * (1.0 / (1.0 + jnp.exp(-zz))))
    ms = jnp.mean(y * y, axis=-1, keepdims=True)
    o_ref[0] = (y * lax.rsqrt(ms + NORM_EPS) * nw_ref[...]).astype(o_ref.dtype)


def _ssd_call(xbc, z, dt, dtt, conv_w, conv_b, dt_bias, a_log, d_skip, norm_w):
    bsz, s, cch = xbc.shape
    L = SSD_CHUNK
    pad = lambda v: jnp.pad(v.reshape(1, -1), ((0, 0), (0, 128 - v.size)))
    full = lambda a: pl.BlockSpec(a.shape, lambda b, i: (0,) * a.ndim)
    args = (xbc, z, dt, dtt, conv_w, conv_b.reshape(1, -1), pad(dt_bias), dt_bias.reshape(-1, 1),
            pad(a_log), a_log.reshape(-1, 1), pad(d_skip), norm_w.reshape(1, -1))
    in_specs = [pl.BlockSpec((1, L, cch), lambda b, i: (b, i, 0)),
                pl.BlockSpec((1, L, SSD_WIDTH), lambda b, i: (b, i, 0)),
                pl.BlockSpec((1, L, 128), lambda b, i: (b, i, 0)),
                pl.BlockSpec((1, SSD_HEADS, L), lambda b, i: (b, 0, i))] + [full(a) for a in args[4:]]
    return pl.pallas_call(
        _ssd_kernel,
        grid=(bsz, s // L),
        in_specs=in_specs,
        out_specs=pl.BlockSpec((1, L, SSD_WIDTH), lambda b, i: (b, i, 0)),
        out_shape=jax.ShapeDtypeStruct((bsz, s, SSD_WIDTH), BF16),
        scratch_shapes=[pltpu.VMEM((8, cch), F32), pltpu.VMEM((8 + L, cch), F32),
                        pltpu.VMEM((SSD_HEADS, SSD_STATE, HEAD_DIM), F32), pltpu.VMEM((L, SSD_WIDTH), F32)],
        compiler_params=_cparams(("parallel", "arbitrary")),
        name="ssd",
    )(*args)


def _outproj_kernel(on_ref, os_ref, x_ref, mod_ref, nnw_ref, n2w_ref, wo_ref, wq_ref, qnw_ref, sk_ref,
                    x1_ref, h2_ref, sc_ref):
    o = on_ref[0]
    ms = jnp.mean(o * o, axis=-1, keepdims=True)
    on = (o * lax.rsqrt(ms + NORM_EPS) * nnw_ref[...]).astype(BF16)
    mix = jnp.dot(jnp.concatenate([on, os_ref[0]], axis=1), wo_ref[...], preferred_element_type=F32)
    x1 = x_ref[0] + mod_ref[0, 2:3, :] * mix
    x1_ref[0] = x1
    ms2 = jnp.mean(x1 * x1, axis=-1, keepdims=True)
    h2 = ((x1 * lax.rsqrt(ms2 + NORM_EPS) * n2w_ref[...]) * (1.0 + mod_ref[0, 4:5, :]) + mod_ref[0, 3:4, :]).astype(BF16)
    h2_ref[0] = h2
    qall = jnp.dot(h2, wq_ref[...], preferred_element_type=F32)
    for hh in range(PEER_HEADS):
        qh = qall[:, hh * PEER_QDIM:(hh + 1) * PEER_QDIM]
        qn = (qh * lax.rsqrt(jnp.mean(qh * qh, axis=-1, keepdims=True) + NORM_EPS) * qnw_ref[...]).astype(BF16)
        both = _nt(sk_ref[hh], qn)
        sc_ref[2 * hh] = both[:PEER_NKEYS]
        sc_ref[2 * hh + 1] = both[PEER_NKEYS:]


def _outproj_call(o_nsa, o_ssd, x, mod, nsa_nw, n2w, w_out, wq, qnw, sub_keys, ts):
    bsz, s, d = x.shape
    nblk = s // ts
    tok = lambda w: pl.BlockSpec((1, ts, w), lambda b, i: (b, i, 0))
    full = lambda a: pl.BlockSpec(a.shape, lambda b, i: (0,) * a.ndim)
    return pl.pallas_call(
        _outproj_kernel,
        grid=(bsz, nblk),
        in_specs=[tok(NSA_WIDTH), tok(SSD_WIDTH), tok(d), pl.BlockSpec((1, 6, d), lambda b, i: (b, 0, 0)),
                  full(nsa_nw), full(n2w), full(w_out), full(wq), full(qnw), full(sub_keys)],
        out_specs=[tok(d), tok(d),
                   pl.BlockSpec((2 * PEER_HEADS, PEER_NKEYS, ts), lambda b, i: (0, 0, b * nblk + i))],
        out_shape=[jax.ShapeDtypeStruct((bsz, s, d), F32), jax.ShapeDtypeStruct((bsz, s, d), BF16),
                   jax.ShapeDtypeStruct((2 * PEER_HEADS, PEER_NKEYS, bsz * s), F32)],
        compiler_params=_cparams(("parallel", "parallel")),
        name="outproj",
    )(o_nsa, o_ssd, x, mod, nsa_nw, n2w, w_out, wq, qnw, sub_keys)


PEER_TT = 128
PEER_HG = 4


def _peersel_kernel(sc_ref, c_ref, e1_ref, r2_ref, e2_ref):
    nk = PEER_NKEYS
    tt = PEER_TT
    K = PEER_TOPK
    kidx = lax.broadcasted_iota(jnp.int32, (nk, tt), 0).astype(F32)
    i16 = lax.broadcasted_iota(jnp.int32, (K, tt), 0).astype(F32)

    def topk_sorted(x, exact_ties, want_rank=True):
        rank = jnp.full((nk, tt), float(K), F32)
        vals = jnp.zeros((K, tt), F32)
        for j in range(K):
            mx = jnp.max(x, axis=0, keepdims=True)
            hit = x == mx
            if exact_ties:
                hit = kidx == jnp.min(jnp.where(hit, kidx, float(nk)), axis=0, keepdims=True)
            if want_rank:
                rank = jnp.where(hit, float(j), rank)
            x = jnp.where(hit, -jnp.inf, x)
            vals = jnp.where(i16 == float(j), mx, vals)
        taken = jnp.sum(jnp.where(x == -jnp.inf, 1.0, 0.0), axis=0, keepdims=True)
        return rank, vals, taken

    def first_stage(hh0, nh):
        ss = [sc_ref[2 * hh0 + i] for i in range(2 * nh)]
        fast = [topk_sorted(s, False, want_rank=i % 2 == 1) for i, s in enumerate(ss)]
        most = fast[0][2]
        for f in fast[1:]:
            most = jnp.maximum(most, f[2])

        def exact():
            out = ()
            for i in range(nh):
                r1, v1a, _ = topk_sorted(ss[2 * i], True)
                r2, v2a, _ = topk_sorted(ss[2 * i + 1], True)
                out += (r1, i16, v1a, r2, v2a)
            return out

        def tie_free():
            out = ()
            for i in range(nh):
                out += (ss[2 * i], fast[2 * i][1], fast[2 * i][1], fast[2 * i + 1][0], fast[2 * i + 1][1])
            return out

        res = lax.cond(jnp.max(most) > float(K), exact, tie_free)
        return [(ss[2 * i], ss[2 * i + 1]) + tuple(res[5 * i:5 * i + 5]) for i in range(nh)]

    def second_stage(hh, s1, s2, key1, match1, v1a, r2, v2a):
        v1 = [v1a[i:i + 1, :] for i in range(K)]
        v2 = [v2a[i:i + 1, :] for i in range(K)]
        cmax = v1[0] + v2[0]
        n = jnp.zeros((K, tt), F32)
        f = v1a + v2[0]
        zsum = jnp.zeros((1, tt), F32)
        for _ in range(K):
            mx = jnp.max(f, axis=0, keepdims=True)
            iw = jnp.min(jnp.where(f == mx, i16, float(K)), axis=0, keepdims=True)
            hit = i16 == iw
            n = n + jnp.where(hit, 1.0, 0.0)
            zsum = zsum + jnp.exp(mx - cmax)
            nstar = jnp.sum(jnp.where(hit, n, 0.0), axis=0, keepdims=True)
            v2n = jnp.sum(jnp.where(i16 == nstar, v2a, 0.0), axis=0, keepdims=True)
            f = jnp.where(hit, v1a + v2n, f)
        cnt = jnp.zeros((nk, tt), F32)
        for i in range(K):
            cnt = jnp.where(key1 == match1[i:i + 1, :], n[i:i + 1, :], cnt)
        c_ref[hh] = cnt
        e1_ref[hh] = jnp.exp(s1 - v1[0]) / zsum
        r2_ref[hh] = r2.astype(BF16)
        e2_ref[hh] = jnp.exp(s2 - v2[0]).astype(BF16)

    for hh0 in range(0, PEER_HEADS, PEER_HG):
        for i, args in enumerate(first_stage(hh0, PEER_HG)):
            second_stage(hh0 + i, *args)


def _peersel_call(sc):
    _, nk, t = sc.shape
    tt = PEER_TT
    ospec = pl.BlockSpec((PEER_HEADS, nk, tt), lambda i: (0, 0, i))
    sd = lambda dt: jax.ShapeDtypeStruct((PEER_HEADS, nk, t), dt)
    return pl.pallas_call(
        _peersel_kernel,
        grid=(t // tt,),
        in_specs=[pl.BlockSpec((2 * PEER_HEADS, nk, tt), lambda i: (0, 0, i))],
        out_specs=[ospec] * 4,
        out_shape=[sd(F32), sd(F32), sd(BF16), sd(BF16)],
        compiler_params=_cparams(("parallel",)),
        name="peersel",
    )(sc)


PEER_TB = 512
PEER_EC = 1024


def _peer_kernel(h2_ref, c_ref, e1_ref, r2_ref, e2_ref, down0_ref, downn_ref, upt_ref, x1_ref, mod_ref, fw_ref,
                 o_ref, acc_ref, act_ref, w_ref):
    j = pl.program_id(1)
    nj = pl.num_programs(1)
    nk = PEER_NKEYS
    na = PEER_EC // nk

    def build_gates(chunk):
        a0 = pl.multiple_of(chunk * na, na)

        def rows(ref, hh, ai):
            grp = ref[hh, pl.ds(a0, na), :]
            r16 = jnp.broadcast_to(grp[ai:ai + 1, :], (16, PEER_TB)).astype(BF16)
            return jnp.concatenate([r16] * (nk // 16), axis=0)

        for ai in range(na):
            w = None
            for hh in range(PEER_HEADS):
                term = jnp.where(r2_ref[hh] < rows(c_ref, hh, ai), e2_ref[hh], 0.0) * rows(e1_ref, hh, ai)
                w = term if w is None else w + term
            w_ref[ai * nk:(ai + 1) * nk, :] = w

    @pl.when(j == 0)
    def _():
        acc_ref[...] = jnp.zeros(acc_ref.shape, F32)

    @pl.when((pl.program_id(0) == 0) & (j == 0))
    def _():
        act_ref[...] = _gelu_tanh(_nt(down0_ref[...], h2_ref[...]).astype(BF16))
        build_gates(0)

    wa = w_ref[...] * act_ref[...]
    acc_ref[...] += jnp.dot(upt_ref[...], wa, preferred_element_type=F32)
    act_ref[...] = _gelu_tanh(_nt(downn_ref[...], h2_ref[...]).astype(BF16))
    build_gates(jnp.where(j == nj - 1, 0, j + 1))

    @pl.when(j == nj - 1)
    def _():
        y = jnp.transpose(acc_ref[...])
        x2 = x1_ref[...] + mod_ref[0, 5:6, :] * y
        ms = jnp.mean(x2 * x2, axis=-1, keepdims=True)
        o_ref[...] = x2 * lax.rsqrt(ms + NORM_EPS) * fw_ref[...]


def _peer_call(h2, cnt, e1, r2, e2, down, upt, x1, mod, fw, s):
    t, d = h2.shape
    ne = down.shape[0]
    tb, ec = PEER_TB, PEER_EC
    per_b = s // tb
    nj = ne // ec
    nblk = t // tb

    def prep(i, j):
        return jnp.minimum(i + (j == nj - 1).astype(jnp.int32), nblk - 1)

    dspec = pl.BlockSpec((PEER_HEADS, PEER_NKEYS, tb), lambda i, j: (0, 0, prep(i, j)))
    return pl.pallas_call(
        _peer_kernel,
        grid=(nblk, nj),
        in_specs=[pl.BlockSpec((tb, d), lambda i, j: (prep(i, j), 0)), dspec, dspec, dspec, dspec,
                  pl.BlockSpec((ec, d), lambda i, j: (0, 0)),
                  pl.BlockSpec((ec, d), lambda i, j: ((j + 1) % nj, 0)),
                  pl.BlockSpec((d, ec), lambda i, j: (0, j)),
                  pl.BlockSpec((tb, d), lambda i, j: (i, 0)),
                  pl.BlockSpec((1, 6, d), lambda i, j: (i // per_b, 0, 0)),
                  pl.BlockSpec((1, d), lambda i, j: (0, 0))],
        out_specs=pl.BlockSpec((tb, d), lambda i, j: (i, 0)),
        out_shape=jax.ShapeDtypeStruct((t, d), F32),
        scratch_shapes=[pltpu.VMEM((d, tb), F32), pltpu.VMEM((ec, tb), BF16), pltpu.VMEM((ec, tb), BF16)],
        compiler_params=_cparams(("arbitrary", "arbitrary")),
        name="peer",
    )(h2, cnt, e1, r2, e2, down, down, upt, x1, mod, fw)


def _rope_tables(s):
    half = ROPE_DIM // 2
    inv_freq = ROPE_THETA ** (-jnp.arange(half, dtype=F32) / half)
    ang = jnp.arange(s).astype(F32)[:, None] * inv_freq[None, :]
    cos, sin = jnp.cos(ang), jnp.sin(ang)
    one = jnp.ones((s, HEAD_DIM - ROPE_DIM), F32)
    cos64 = jnp.concatenate([cos, cos, one], axis=1)
    sin64 = jnp.concatenate([-sin, sin, 0.0 * one], axis=1)
    return jnp.concatenate([cos64, cos64], axis=1), jnp.concatenate([sin64, sin64], axis=1)


def _swap_cols(w):
    d, n = w.shape
    wh = w.reshape(d, n // HEAD_DIM, HEAD_DIM)
    half = ROPE_DIM // 2
    sw = jnp.concatenate([wh[..., half:ROPE_DIM], wh[..., :half], jnp.zeros_like(wh[..., ROPE_DIM:])], axis=-1)
    return sw.reshape(d, n)


def _pack_w_in(w):
    o = np.cumsum((0, 512, 128, 128, 128, 128, 128, 128, 24, 512, 1024, 8))
    q, kc, vc, ksel, vsel, kwin, vwin, gl, z, xbc, dtr = (w[:, o[i]:o[i + 1]] for i in range(11))
    gl = gl.reshape(-1, NSA_GROUPS, NSA_HPG * N_BRANCH)
    gl = jnp.pad(gl, ((0, 0), (0, 0), (0, 128 - NSA_HPG * N_BRANCH))).reshape(-1, 256)
    dtr = jnp.pad(dtr, ((0, 0), (0, 128 - SSD_HEADS)))
    cols = [q, _swap_cols(q), ksel, _swap_cols(ksel), kwin, _swap_cols(kwin), vsel, vwin, kc, vc, gl, z, xbc, dtr]
    return jnp.concatenate(cols, axis=1).astype(BF16)


def _overlap_t(s):
    n_cmp_pad = s // CMP_STRIDE
    n_sel = s // SEL_BLOCK
    cs = np.arange(n_cmp_pad) * CMP_STRIDE
    ss = np.arange(n_sel) * SEL_BLOCK
    ov = np.maximum(np.minimum(cs[None, :] + CMP_BLOCK, ss[:, None] + SEL_BLOCK)
                    - np.maximum(cs[None, :], ss[:, None]), 0).astype(np.float32) / CMP_BLOCK
    ov[:, n_cmp_pad - 1] = 0.0
    return jnp.asarray(ov, BF16)


def _blockdiag_keys(sub_keys):
    h, two, n, half = sub_keys.shape
    z = jnp.zeros((h, n, half), sub_keys.dtype)
    top = jnp.concatenate([sub_keys[:, 0], z], axis=2)
    bot = jnp.concatenate([z, sub_keys[:, 1]], axis=2)
    return jnp.concatenate([top, bot], axis=1).astype(BF16)


def kernel(x, c, w_ada, b_ada, norm1_w, w_in, cmp_pe_k, cmp_pe_v, cmp_w1_k, cmp_w2_k, cmp_w1_v, cmp_w2_v,
           nsa_norm_w, conv_w, conv_b, dt_bias, a_log, d_skip, ssd_norm_w, w_out, norm2_w,
           peer_wq, peer_qnorm_w, peer_sub_keys, peer_down, peer_up, final_norm_w):
    bsz, s, d = x.shape
    assert d == D_MODEL and s % NSA_TK == 0 and s // SEL_BLOCK <= LANES and w_ada.shape[0] == 1
    lyr = 0
    ts = 512

    mod = _mod_call(c, w_ada[lyr], b_ada[lyr]).reshape(bsz, 6, d)
    cos_t, sin_t = _rope_tables(s)
    (qp, qr, ksel, kwin, vsel, vwin, kcmp, vcmp, gates, z, xbc, dtr) = _inproj_call(
        x, mod, norm1_w[lyr].reshape(1, d), _pack_w_in(w_in[lyr]), cos_t, sin_t, ts)

    def w1_groups(w1):
        w = w1.reshape(2, CMP_STRIDE, HEAD_DIM, CMP_HIDDEN)
        z = jnp.zeros_like(w)
        per_g = [jnp.concatenate([w, z], axis=2), jnp.concatenate([z, w], axis=2)]
        return jnp.stack(per_g, axis=1).reshape(2, NSA_GROUPS, CMP_STRIDE * 2 * HEAD_DIM, CMP_HIDDEN).astype(BF16)

    kc, vc = _compress_call(
        kcmp, vcmp, jnp.tile(cmp_pe_k[lyr], (1, NSA_GROUPS)), jnp.tile(cmp_pe_v[lyr], (1, NSA_GROUPS)),
        w1_groups(cmp_w1_k[lyr]), cmp_w2_k[lyr].astype(BF16), w1_groups(cmp_w1_v[lyr]), cmp_w2_v[lyr].astype(BF16))

    o_nsa = _nsa_call(qp, qr, kc, vc, ksel, vsel, kwin, vwin, gates, _overlap_t(s))

    dtt = jnp.transpose(dtr[:, :, :SSD_HEADS], (0, 2, 1))
    o_ssd = _ssd_call(xbc, z, dtr, dtt, conv_w[lyr], conv_b[lyr], dt_bias[lyr], a_log[lyr], d_skip[lyr],
                      ssd_norm_w[lyr])

    x1, h2, sc = _outproj_call(
        o_nsa, o_ssd, x, mod, nsa_norm_w[lyr].reshape(1, -1), norm2_w[lyr].reshape(1, d),
        w_out[lyr].astype(BF16), peer_wq[lyr].astype(BF16), peer_qnorm_w[lyr].reshape(1, -1),
        _blockdiag_keys(peer_sub_keys[lyr]), ts)

    cnt, e1, r2, e2 = _peersel_call(sc)

    out = _peer_call(h2.reshape(bsz * s, d), cnt, e1, r2, e2, peer_down[lyr].astype(BF16),
                     jnp.transpose(peer_up[lyr]).astype(BF16), x1.reshape(bsz * s, d), mod,
                     final_norm_w.reshape(1, d), s)
    return out.reshape(bsz, s, d)
```

```python
import functools
import math

import numpy as np
import jax
import jax.numpy as jnp
from jax import lax
from jax.experimental import pallas as pl
from jax.experimental.pallas import tpu as pltpu

F32 = jnp.float32
BF16 = jnp.bfloat16
HIGHEST = lax.Precision.HIGHEST

D_MODEL = 1024
NSA_WIDTH = 512
SSD_WIDTH = 512
HEAD_DIM = 64
NSA_HEADS = 8
NSA_GROUPS = 2
NSA_HPG = 4
N_BRANCH = 3
CMP_BLOCK = 32
CMP_STRIDE = 16
CMP_HIDDEN = 256
SEL_BLOCK = 64
SEL_SHIFT = SEL_BLOCK.bit_length() - 1
SEL_TOPN = 16
WINDOW = 512
ROPE_THETA = 500000.0
ROPE_DIM = 16
SSD_HEADS = 8
SSD_GROUPS = 2
SSD_STATE = 128
SSD_CONV = 4
SSD_CHUNK = 128
PEER_HEADS = 8
PEER_NKEYS = 128
PEER_QDIM = 256
PEER_TOPK = 16
NORM_EPS = 1e-6
NEG_INF = -1e30
LOG2E = math.log2(math.e)
FORCE_BONUS = 1e4

LANES = 128
VMEM_LIMIT = 56 * 1024 * 1024

C_Q, C_QSW, C_KSEL, C_KSELSW, C_KWIN, C_KWINSW = 0, 512, 1024, 1152, 1280, 1408
C_VSEL, C_VWIN, C_KCMP, C_VCMP, C_GATE, C_Z, C_XBC, C_DT, C_END = 1536, 1664, 1792, 1920, 2048, 2304, 2816, 3840, 3968


def _gelu_tanh(x):
    c = math.sqrt(2.0 / math.pi)
    return 0.5 * x * (1.0 + jnp.tanh(c * (x + 0.044715 * (x * x * x))))


def _nt(a, b):
    return lax.dot_general(a, b, (((1,), (1,)), ((), ())), preferred_element_type=F32)


def _cparams(sem):
    return pltpu.CompilerParams(dimension_semantics=sem, vmem_limit_bytes=VMEM_LIMIT)


def _mod_kernel(c_ref, w_ref, b_ref, o_ref):
    o_ref[...] = jnp.dot(c_ref[...], w_ref[...], preferred_element_type=F32, precision=HIGHEST) + b_ref[...]


def _mod_call(c, w_ada, b_ada):
    bsz = c.shape[0]
    n = w_ada.shape[1]
    return pl.pallas_call(
        _mod_kernel,
        grid=(n // D_MODEL,),
        in_specs=[pl.BlockSpec((bsz, D_MODEL), lambda j: (0, 0)),
                  pl.BlockSpec((D_MODEL, D_MODEL), lambda j: (0, j)),
                  pl.BlockSpec((1, D_MODEL), lambda j: (0, j))],
        out_specs=pl.BlockSpec((bsz, D_MODEL), lambda j: (0, j)),
        out_shape=jax.ShapeDtypeStruct((bsz, n), F32),
        compiler_params=_cparams(("arbitrary",)),
        name="mod",
    )(c, w_ada, b_ada.reshape(1, n))


def _inproj_kernel(x_ref, mod_ref, nw_ref, w_ref, cos_ref, sin_ref,
                   qp_ref, qr_ref, ksel_ref, kwin_ref, vsel_ref, vwin_ref,
                   kcmp_ref, vcmp_ref, gate_ref, z_ref, xbc_ref, dt_ref):
    x = x_ref[0]
    ms = jnp.mean(x * x, axis=-1, keepdims=True)
    y = x * lax.rsqrt(ms + NORM_EPS) * nw_ref[...]
    h = (y * (1.0 + mod_ref[0, 1:2, :]) + mod_ref[0, 0:1, :]).astype(BF16)

    def proj(lo, hi):
        return jnp.dot(h, w_ref[:, lo:hi], preferred_element_type=F32)

    cos = cos_ref[...]
    sin = sin_ref[...]
    scale = HEAD_DIM ** -0.5 * LOG2E
    qa = proj(C_Q, C_KSEL)
    kv = proj(C_KSEL, C_GATE)
    rest = proj(C_GATE, C_END)
    q = qa[:, :512]
    qsw = qa[:, 512:]
    qp_ref[0] = (q * scale).astype(BF16)
    cos4 = jnp.concatenate([cos] * 4, axis=1)
    sin4 = jnp.concatenate([sin] * 4, axis=1)
    qr_ref[0] = ((q * cos4 + qsw * sin4) * scale).astype(BF16)

    part = lambda c: kv[:, c - C_KSEL:c - C_KSEL + 128]
    ks = part(C_KSEL) * cos + part(C_KSELSW) * sin
    kw = part(C_KWIN) * cos + part(C_KWINSW) * sin
    vs = part(C_VSEL)
    vw = part(C_VWIN)
    kwin_ref[0, 0] = kw[:, :HEAD_DIM].astype(BF16)
    kwin_ref[0, 1] = kw[:, HEAD_DIM:].astype(BF16)
    lane = lax.broadcasted_iota(jnp.int32, vs.shape, 1)
    blk = jnp.right_shift(pl.program_id(1) * ks.shape[0] + lax.broadcasted_iota(jnp.int32, ks.shape, 0), SEL_SHIFT)
    hot_lo = jnp.where(lane - HEAD_DIM == blk, 1.0, 0.0)
    hot_hi = jnp.where(lane + HEAD_DIM == blk, 1.0, 0.0).astype(BF16)
    ksel_ref[0, 0, :, 0:128] = jnp.where(lane < HEAD_DIM, ks, hot_lo).astype(BF16)
    ksel_ref[0, 1, :, 0:128] = jnp.where(lane < HEAD_DIM, pltpu.roll(ks, HEAD_DIM, axis=1), hot_lo).astype(BF16)
    ksel_ref[0, 0, :, 128:256] = hot_hi
    ksel_ref[0, 1, :, 128:256] = hot_hi
    ones_col = jnp.where(lane == HEAD_DIM, 1.0, 0.0)
    for ref, val in ((vsel_ref, vs), (vwin_ref, vw)):
        ref[0, 0] = jnp.where(lane < HEAD_DIM, val, ones_col).astype(BF16)
        ref[0, 1] = jnp.where(lane < HEAD_DIM, pltpu.roll(val, HEAD_DIM, axis=1), ones_col).astype(BF16)
    kcmp_ref[0] = part(C_KCMP)
    vcmp_ref[0] = part(C_VCMP)
    sg = 1.0 / (1.0 + jnp.exp(-rest[:, :C_Z - C_GATE]))
    gate_ref[0, 0] = sg[:, :128]
    gate_ref[0, 1] = sg[:, 128:]
    z_ref[0] = rest[:, C_Z - C_GATE:C_XBC - C_GATE]
    xbc_ref[0] = rest[:, C_XBC - C_GATE:C_DT - C_GATE]
    dt_ref[0] = rest[:, C_DT - C_GATE:]


def _inproj_call(x, mod, norm_w, w_ext, cos_t, sin_t, ts):
    bsz, s, d = x.shape
    grid = (bsz, s // ts)
    tok = lambda w: pl.BlockSpec((1, ts, w), lambda b, i: (b, i, 0))
    hm = pl.BlockSpec((1, NSA_GROUPS, ts, HEAD_DIM), lambda b, i: (b, 0, i, 0))
    hv = pl.BlockSpec((1, NSA_GROUPS, ts, 128), lambda b, i: (b, 0, i, 0))
    hk = pl.BlockSpec((1, NSA_GROUPS, ts, 256), lambda b, i: (b, 0, i, 0))
    sd = jax.ShapeDtypeStruct
    out_shape = [sd((bsz, s, 512), BF16), sd((bsz, s, 512), BF16)] + \
                [sd((bsz, NSA_GROUPS, s, 256), BF16), sd((bsz, NSA_GROUPS, s, HEAD_DIM), BF16)] + \
                [sd((bsz, NSA_GROUPS, s, 128), BF16)] * 2 + \
                [sd((bsz, s, 128), F32), sd((bsz, s, 128), F32),
                 sd((bsz, NSA_GROUPS, s, 128), F32),
                 sd((bsz, s, 512), F32), sd((bsz, s, 1024), F32), sd((bsz, s, 128), F32)]
    out_specs = [tok(512), tok(512), hk, hm, hv, hv, tok(128), tok(128),
                 pl.BlockSpec((1, NSA_GROUPS, ts, 128), lambda b, i: (b, 0, i, 0)),
                 tok(512), tok(1024), tok(128)]
    return pl.pallas_call(
        _inproj_kernel,
        grid=grid,
        in_specs=[tok(d),
                  pl.BlockSpec((1, 6, d), lambda b, i: (b, 0, 0)),
                  pl.BlockSpec((1, d), lambda b, i: (0, 0)),
                  pl.BlockSpec((d, C_END), lambda b, i: (0, 0)),
                  pl.BlockSpec((ts, 128), lambda b, i: (i, 0)),
                  pl.BlockSpec((ts, 128), lambda b, i: (i, 0))],
        out_specs=out_specs,
        out_shape=out_shape,
        compiler_params=_cparams(("parallel", "parallel")),
        name="inproj",
    )(x, mod, norm_w, w_ext, cos_t, sin_t)


def _compress_kernel(k_ref, v_ref, pek_ref, pev_ref, w1k_ref, w2k_ref, w1v_ref, w2v_ref,
                     kc_ref, vc_ref, hb_ref):
    nch = k_ref.shape[1] // CMP_STRIDE

    def one(x_ref, pe_ref, w1_ref, w2_ref, o_ref):
        xs = [x_ref[0, pl.ds(l, nch, stride=CMP_STRIDE), :] for l in range(CMP_STRIDE)]
        top = jnp.concatenate([(xs[l] + pe_ref[l:l + 1, :]).astype(BF16) for l in range(CMP_STRIDE)], axis=1)
        bot = jnp.concatenate([(xs[l] + pe_ref[CMP_STRIDE + l:CMP_STRIDE + l + 1, :]).astype(BF16)
                               for l in range(CMP_STRIDE)], axis=1)
        htop = jnp.dot(top, w1_ref[0, 0], preferred_element_type=F32)
        hb_ref[0:nch, :] = jnp.dot(bot, w1_ref[1, 0], preferred_element_type=F32)
        hb_ref[nch:nch + 8, :] = jnp.zeros((8, CMP_HIDDEN), F32)
        pre = htop + hb_ref[pl.ds(1, nch), :]
        out = jnp.dot(_gelu_tanh(pre).astype(BF16), w2_ref[...], preferred_element_type=F32)
        row = lax.broadcasted_iota(jnp.int32, out.shape, 0)
        o_ref[0, 0] = jnp.where(row < nch - 1, out, 0.0).astype(BF16)

    one(k_ref, pek_ref, w1k_ref, w2k_ref, kc_ref)
    one(v_ref, pev_ref, w1v_ref, w2v_ref, vc_ref)


def _compress_call(kcmp, vcmp, pek, pev, w1k, w2k, w1v, w2v):
    bsz, s, cw = kcmp.shape
    nch = s // CMP_STRIDE
    xs = pl.BlockSpec((1, s, cw), lambda b, gi: (b, 0, 0))
    full = lambda a: pl.BlockSpec(a.shape, lambda b, gi: (0,) * a.ndim)
    w1s = pl.BlockSpec((2, 1) + w1k.shape[2:], lambda b, gi: (0, gi, 0, 0))
    os_ = pl.BlockSpec((1, 1, nch, HEAD_DIM), lambda b, gi: (b, gi, 0, 0))
    sd = jax.ShapeDtypeStruct((bsz, NSA_GROUPS, nch, HEAD_DIM), BF16)
    return pl.pallas_call(
        _compress_kernel,
        grid=(bsz, NSA_GROUPS),
        in_specs=[xs, xs, full(pek), full(pev), w1s, full(w2k), w1s, full(w2v)],
        out_specs=[os_, os_],
        out_shape=[sd, sd],
        scratch_shapes=[pltpu.VMEM((nch + 8, CMP_HIDDEN), F32)],
        compiler_params=_cparams(("parallel", "parallel")),
        name="compress",
    )(kcmp, vcmp, pek, pev, w1k, w2k, w1v, w2v)


NSA_TQ = 256
NSA_TK = 1024


def _nsa_kernel(qp_ref, qr_ref, kc_ref, vc_ref, ks_ref, vs_ref, kw_ref, vw_ref, gate_ref, ovt_ref, o_ref):
    tq = NSA_TQ
    tk = NSA_TK
    rows = NSA_HPG * tq
    qt = pl.program_id(2)
    q0 = qt * tq
    ncmp = kc_ref.shape[2]
    nsel = ovt_ref.shape[0]
    stack = lambda a: jnp.concatenate([a] * NSA_HPG, axis=0)
    qp = jnp.concatenate([qp_ref[0, :, r * HEAD_DIM:(r + 1) * HEAD_DIM] for r in range(NSA_HPG)], axis=0)
    qr = jnp.concatenate([qr_ref[0, :, r * HEAD_DIM:(r + 1) * HEAD_DIM] for r in range(NSA_HPG)], axis=0)

    t_row = q0 + lax.broadcasted_iota(jnp.int32, (tq, ncmp), 0)
    cend = lax.broadcasted_iota(jnp.int32, (tq, ncmp), 1) * CMP_STRIDE + (CMP_BLOCK - 1)
    cbias = stack(jnp.where(cend <= t_row, 0.0, NEG_INF))
    s = jnp.where(cbias == 0.0, _nt(qp, kc_ref[0, 0]), NEG_INF)
    e = jnp.exp2(s - jnp.max(s, axis=-1, keepdims=True))
    p = jnp.where(cbias == 0.0, e / jnp.sum(e, axis=-1, keepdims=True), 0.0)
    o_c = jnp.dot(p.astype(BF16), vc_ref[0, 0], preferred_element_type=F32)

    psum = p[0:tq] + p[tq:2 * tq] + p[2 * tq:3 * tq] + p[3 * tq:4 * tq]
    p_hi = psum.astype(BF16)
    p_lo = (psum - p_hi.astype(F32)).astype(BF16)
    ovt = ovt_ref[...]
    imp = _nt(ovt, p_hi) + _nt(ovt, p_lo)
    jblk = lax.broadcasted_iota(jnp.int32, (nsel, tq), 0)
    tt = q0 + lax.broadcasted_iota(jnp.int32, (nsel, tq), 1)
    blk_t = jnp.right_shift(tt, SEL_SHIFT)
    forced = (jblk == 0) | (jblk == blk_t) | (jblk == blk_t - 1)
    xs = jnp.where(jblk * SEL_BLOCK <= tt, imp + jnp.where(forced, FORCE_BONUS, 0.0), NEG_INF)
    drop_t = jnp.full((nsel, tq), NEG_INF, F32)
    jblk_f = jblk.astype(F32)
    for _ in range(SEL_TOPN):
        mx = jnp.max(xs, axis=0, keepdims=True)
        idx = jnp.min(jnp.where(xs == mx, jblk_f, float(nsel)), axis=0, keepdims=True)
        hit = jblk_f == idx
        drop_t = jnp.where(hit, 0.0, drop_t)
        xs = jnp.where(hit, -jnp.inf, xs)
    drop = jnp.transpose(drop_t).astype(BF16)

    wk = WINDOW + tq
    wstart = pl.multiple_of(jnp.maximum(q0 - WINDOW, 0), tq)
    tq_w = q0 + lax.broadcasted_iota(jnp.int32, (tq, wk), 0)
    kpos_w = wstart + lax.broadcasted_iota(jnp.int32, (tq, wk), 1)
    bias_w = jnp.where((kpos_w <= tq_w) & (tq_w - kpos_w < WINDOW), 0.0, NEG_INF)
    sw = _nt(qr, kw_ref[0, 0, pl.ds(wstart, wk), :]) + stack(bias_w)
    ew = jnp.exp2(sw - jnp.max(sw, axis=-1, keepdims=True))
    pv_w = jnp.dot(ew.astype(BF16), vw_ref[0, 0, pl.ds(wstart, wk), :], preferred_element_type=F32)
    o_w = pv_w[:, :HEAD_DIM] / pv_w[:, HEAD_DIM:HEAD_DIM + 1]

    if nsel < LANES:
        drop = jnp.concatenate([drop, jnp.zeros((tq, LANES - nsel), BF16)], axis=1)
    q_aug = jnp.concatenate([qr, stack(drop), jnp.zeros((rows, HEAD_DIM), BF16)], axis=1)

    def sel_tile(kt, carry, diagonal):
        m_i, acc = carry
        kbase = pl.multiple_of(kt * tk, tk)
        sc = _nt(q_aug, ks_ref[0, 0, pl.ds(kbase, tk), :])
        if diagonal:
            kpos = kbase + lax.broadcasted_iota(jnp.int32, (tq, tk), 1)
            tpos = q0 + lax.broadcasted_iota(jnp.int32, (tq, tk), 0)
            sc = sc + stack(jnp.where(kpos <= tpos, 0.0, NEG_INF))
        m_new = jnp.maximum(m_i, jnp.max(sc, axis=-1, keepdims=True))
        pe = jnp.exp2(sc - m_new)
        acc_new = jnp.exp2(m_i - m_new) * acc + jnp.dot(pe.astype(BF16), vs_ref[0, 0, pl.ds(kbase, tk), :],
                                                        preferred_element_type=F32)
        return m_new, acc_new

    init = (jnp.full((rows, 1), NEG_INF, F32), jnp.zeros((rows, 2 * HEAD_DIM), F32))
    last_kt = (q0 + tq - 1) // tk
    _, acc_s = sel_tile(last_kt, lax.fori_loop(0, last_kt, lambda kt, c: sel_tile(kt, c, False), init), True)
    o_s = acc_s[:, :HEAD_DIM] / acc_s[:, HEAD_DIM:HEAD_DIM + 1]

    gates = gate_ref[0, 0]
    for r in range(NSA_HPG):
        sl = slice(r * tq, (r + 1) * tq)
        o_r = (gates[:, 3 * r:3 * r + 1] * o_c[sl] + gates[:, 3 * r + 1:3 * r + 2] * o_s[sl]
               + gates[:, 3 * r + 2:3 * r + 3] * o_w[sl])
        o_ref[0, :, r * HEAD_DIM:(r + 1) * HEAD_DIM] = o_r


def _nsa_call(qp, qr, kc, vc, ks, vs, kw, vw, gates, ovt):
    bsz, s, _ = qp.shape
    tq = NSA_TQ
    ncmp = kc.shape[2]
    gw = NSA_HPG * HEAD_DIM
    qspec = pl.BlockSpec((1, tq, gw), lambda b, g, i: (b, i, g))
    cspec = pl.BlockSpec((1, 1, ncmp, HEAD_DIM), lambda b, g, i: (b, g, 0, 0))
    kspec = pl.BlockSpec((1, 1, s, HEAD_DIM), lambda b, g, i: (b, g, 0, 0))
    kaspec = pl.BlockSpec((1, 1, s, 4 * HEAD_DIM), lambda b, g, i: (b, g, 0, 0))
    vspec = pl.BlockSpec((1, 1, s, 2 * HEAD_DIM), lambda b, g, i: (b, g, 0, 0))
    return pl.pallas_call(
        _nsa_kernel,
        grid=(bsz, NSA_GROUPS, s // tq),
        in_specs=[qspec, qspec, cspec, cspec, kaspec, vspec, kspec, vspec,
                  pl.BlockSpec((1, 1, tq, 128), lambda b, g, i: (b, g, i, 0)),
                  pl.BlockSpec(ovt.shape, lambda b, g, i: (0, 0))],
        out_specs=pl.BlockSpec((1, tq, gw), lambda b, g, i: (b, i, g)),
        out_shape=jax.ShapeDtypeStruct((bsz, s, NSA_WIDTH), F32),
        compiler_params=_cparams(("parallel", "parallel", "arbitrary")),
        name="nsa",
    )(qp, qr, kc, vc, ks, vs, kw, vw, gates, ovt)


def _softplus(x):
    return jnp.maximum(x, 0.0) + jnp.log1p(jnp.exp(-jnp.abs(x)))


def _ssd_kernel(xbc_ref, z_ref, dt_ref, dtt_ref, cw_ref, cb_ref, dtb_ref, dtbt_ref, al_ref, alt_ref,
                dsk_ref, nw_ref, o_ref, tail_ref, xp_ref, st_ref, y_ref):
    L = SSD_CHUNK
    P = HEAD_DIM
    N = SSD_STATE
    c = pl.program_id(1)

    @pl.when(c == 0)
    def _():
        tail_ref[...] = jnp.zeros(tail_ref.shape, F32)
        st_ref[...] = jnp.zeros(st_ref.shape, F32)

    xin = xbc_ref[0]
    xp_ref[0:8, :] = tail_ref[...]
    xp_ref[8:8 + L, :] = xin
    tail_ref[...] = xin[L - 8:L, :]
    conv = cb_ref[...] + jnp.zeros_like(xin)
    for k in range(SSD_CONV):
        conv = conv + cw_ref[k:k + 1, :] * xp_ref[pl.ds(8 - (SSD_CONV - 1) + k, L), :]
    u = conv * (1.0 / (1.0 + jnp.exp(-conv)))
    xs = u[:, :SSD_WIDTH]

    dt_c = _softplus(dt_ref[0] + dtb_ref[...])
    dt_r = _softplus(dtt_ref[0] + dtbt_ref[...])
    a_c = -jnp.exp(al_ref[...])
    a_r = -jnp.exp(alt_ref[...])
    li = lax.broadcasted_iota(jnp.int32, (L, L), 0)
    si = lax.broadcasted_iota(jnp.int32, (L, L), 1)
    causal = li >= si
    tri = jnp.where(causal, 1.0, 0.0)
    tri_t = jnp.where(li <= si, 1.0, 0.0)
    acs_c = jnp.dot(tri, dt_c * a_c, preferred_element_type=F32, precision=HIGHEST)
    acs_r = jnp.dot(dt_r * a_r, tri_t, preferred_element_type=F32, precision=HIGHEST)

    for g in range(SSD_GROUPS):
        bm = u[:, SSD_WIDTH + g * N:SSD_WIDTH + (g + 1) * N]
        cm = u[:, SSD_WIDTH + SSD_GROUPS * N + g * N:SSD_WIDTH + SSD_GROUPS * N + (g + 1) * N]
        bm_b = bm.astype(BF16)
        cm_b = cm.astype(BF16)
        cb = _nt(cm_b, bm_b)
        bm_t = jnp.transpose(bm)
        for r in range(SSD_HEADS // SSD_GROUPS):
            hh = g * (SSD_HEADS // SSD_GROUPS) + r
            col = acs_c[:, hh:hh + 1]
            row = acs_r[hh:hh + 1, :]
            last = acs_r[hh:hh + 1, L - 1:L]
            lm = jnp.exp(jnp.where(causal, col - row, NEG_INF))
            x_h = xs[:, hh * P:(hh + 1) * P]
            xd = x_h * dt_c[:, hh:hh + 1]
            y_d = jnp.dot((cb * lm).astype(BF16), xd.astype(BF16), preferred_element_type=F32)
            prev = st_ref[hh]
            y_o = jnp.dot(cm_b, prev.astype(BF16), preferred_element_type=F32) * jnp.exp(col)
            dec = jnp.exp(last - row)
            st_new = jnp.dot((bm_t * dec).astype(BF16), xd.astype(BF16), preferred_element_type=F32)
            st_ref[hh] = prev * jnp.exp(last) + st_new
            y_ref[:, hh * P:(hh + 1) * P] = y_d + y_o + x_h * dsk_ref[0:1, hh:hh + 1]

    zz = z_ref[0]
    y = y_ref[...] * (zz * (1.0 / (1.0 + jnp.exp(-zz))))
    ms = jnp.mean(y * y, axis=-1, keepdims=True)
    o_ref[0] = (y * lax.rsqrt(ms + NORM_EPS) * nw_ref[...]).astype(o_ref.dtype)


def _ssd_call(xbc, z, dt, dtt, conv_w, conv_b, dt_bias, a_log, d_skip, norm_w):
    bsz, s, cch = xbc.shape
    L = SSD_CHUNK
    pad = lambda v: jnp.pad(v.reshape(1, -1), ((0, 0), (0, 128 - v.size)))
    full = lambda a: pl.BlockSpec(a.shape, lambda b, i: (0,) * a.ndim)
    args = (xbc, z, dt, dtt, conv_w, conv_b.reshape(1, -1), pad(dt_bias), dt_bias.reshape(-1, 1),
            pad(a_log), a_log.reshape(-1, 1), pad(d_skip), norm_w.reshape(1, -1))
    in_specs = [pl.BlockSpec((1, L, cch), lambda b, i: (b, i, 0)),
                pl.BlockSpec((1, L, SSD_WIDTH), lambda b, i: (b, i, 0)),
                pl.BlockSpec((1, L, 128), lambda b, i: (b, i, 0)),
                pl.BlockSpec((1, SSD_HEADS, L), lambda b, i: (b, 0, i))] + [full(a) for a in args[4:]]
    return pl.pallas_call(
        _ssd_kernel,
        grid=(bsz, s // L),
        in_specs=in_specs,
        out_specs=pl.BlockSpec((1, L, SSD_WIDTH), lambda b, i: (b, i, 0)),
        out_shape=jax.ShapeDtypeStruct((bsz, s, SSD_WIDTH), BF16),
        scratch_shapes=[pltpu.VMEM((8, cch), F32), pltpu.VMEM((8 + L, cch), F32),
                        pltpu.VMEM((SSD_HEADS, SSD_STATE, HEAD_DIM), F32), pltpu.VMEM((L, SSD_WIDTH), F32)],
        compiler_params=_cparams(("parallel", "arbitrary")),
        name="ssd",
    )(*args)


def _outproj_kernel(on_ref, os_ref, x_ref, mod_ref, nnw_ref, n2w_ref, wo_ref, wq_ref, qnw_ref, sk_ref,
                    x1_ref, h2_ref, sc_ref):
    o = on_ref[0]
    ms = jnp.mean(o * o, axis=-1, keepdims=True)
    on = (o * lax.rsqrt(ms + NORM_EPS) * nnw_ref[...]).astype(BF16)
    mix = jnp.dot(jnp.concatenate([on, os_ref[0]], axis=1), wo_ref[...], preferred_element_type=F32)
    x1 = x_ref[0] + mod_ref[0, 2:3, :] * mix
    x1_ref[0] = x1
    ms2 = jnp.mean(x1 * x1, axis=-1, keepdims=True)
    h2 = ((x1 * lax.rsqrt(ms2 + NORM_EPS) * n2w_ref[...]) * (1.0 + mod_ref[0, 4:5, :]) + mod_ref[0, 3:4, :]).astype(BF16)
    h2_ref[0] = h2
    qall = jnp.dot(h2, wq_ref[...], preferred_element_type=F32)
    for hh in range(PEER_HEADS):
        qh = qall[:, hh * PEER_QDIM:(hh + 1) * PEER_QDIM]
        qn = (qh * lax.rsqrt(jnp.mean(qh * qh, axis=-1, keepdims=True) + NORM_EPS) * qnw_ref[...]).astype(BF16)
        both = _nt(sk_ref[hh], qn)
        sc_ref[2 * hh] = both[:PEER_NKEYS]
        sc_ref[2 * hh + 1] = both[PEER_NKEYS:]


def _outproj_call(o_nsa, o_ssd, x, mod, nsa_nw, n2w, w_out, wq, qnw, sub_keys, ts):
    bsz, s, d = x.shape
    nblk = s // ts
    tok = lambda w: pl.BlockSpec((1, ts, w), lambda b, i: (b, i, 0))
    full = lambda a: pl.BlockSpec(a.shape, lambda b, i: (0,) * a.ndim)
    return pl.pallas_call(
        _outproj_kernel,
        grid=(bsz, nblk),
        in_specs=[tok(NSA_WIDTH), tok(SSD_WIDTH), tok(d), pl.BlockSpec((1, 6, d), lambda b, i: (b, 0, 0)),
                  full(nsa_nw), full(n2w), full(w_out), full(wq), full(qnw), full(sub_keys)],
        out_specs=[tok(d), tok(d),
                   pl.BlockSpec((2 * PEER_HEADS, PEER_NKEYS, ts), lambda b, i: (0, 0, b * nblk + i))],
        out_shape=[jax.ShapeDtypeStruct((bsz, s, d), F32), jax.ShapeDtypeStruct((bsz, s, d), BF16),
                   jax.ShapeDtypeStruct((2 * PEER_HEADS, PEER_NKEYS, bsz * s), F32)],
        compiler_params=_cparams(("parallel", "parallel")),
        name="outproj",
    )(o_nsa, o_ssd, x, mod, nsa_nw, n2w, w_out, wq, qnw, sub_keys)


PEER_TT = 256
PEER_HG = 4


def _peersel_kernel(sc_ref, c_ref, e1_ref, r2_ref, e2_ref):
    nk = PEER_NKEYS
    tt = PEER_TT
    K = PEER_TOPK
    kidx = lax.broadcasted_iota(jnp.int32, (nk, tt), 0).astype(F32)
    i16 = lax.broadcasted_iota(jnp.int32, (K, tt), 0).astype(F32)

    def topk_sorted(x, exact_ties, want_rank=True):
        rank = jnp.full((nk, tt), float(K), F32)
        vals = jnp.zeros((K, tt), F32)
        for j in range(K):
            mx = jnp.max(x, axis=0, keepdims=True)
            hit = x == mx
            if exact_ties:
                hit = kidx == jnp.min(jnp.where(hit, kidx, float(nk)), axis=0, keepdims=True)
            if want_rank:
                rank = jnp.where(hit, float(j), rank)
            x = jnp.where(hit, -jnp.inf, x)
            vals = jnp.where(i16 == float(j), mx, vals)
        taken = jnp.sum(jnp.where(x == -jnp.inf, 1.0, 0.0), axis=0, keepdims=True)
        return rank, vals, taken

    def first_stage(hh0, nh):
        ss = [sc_ref[2 * hh0 + i] for i in range(2 * nh)]
        fast = [topk_sorted(s, False, want_rank=i % 2 == 1) for i, s in enumerate(ss)]
        most = fast[0][2]
        for f in fast[1:]:
            most = jnp.maximum(most, f[2])

        def exact():
            out = ()
            for i in range(nh):
                r1, v1a, _ = topk_sorted(ss[2 * i], True)
                r2, v2a, _ = topk_sorted(ss[2 * i + 1], True)
                out += (r1, i16, v1a, r2, v2a)
            return out

        def tie_free():
            out = ()
            for i in range(nh):
                out += (ss[2 * i], fast[2 * i][1], fast[2 * i][1], fast[2 * i + 1][0], fast[2 * i + 1][1])
            return out

        res = lax.cond(jnp.max(most) > float(K), exact, tie_free)
        return [(ss[2 * i], ss[2 * i + 1]) + tuple(res[5 * i:5 * i + 5]) for i in range(nh)]

    def second_stage(hh, s1, s2, key1, match1, v1a, r2, v2a):
        v1 = [v1a[i:i + 1, :] for i in range(K)]
        v2 = [v2a[i:i + 1, :] for i in range(K)]
        cmax = v1[0] + v2[0]
        n = jnp.zeros((K, tt), F32)
        f = v1a + v2[0]
        zsum = jnp.zeros((1, tt), F32)
        for _ in range(K):
            mx = jnp.max(f, axis=0, keepdims=True)
            iw = jnp.min(jnp.where(f == mx, i16, float(K)), axis=0, keepdims=True)
            hit = i16 == iw
            n = n + jnp.where(hit, 1.0, 0.0)
            zsum = zsum + jnp.exp(mx - cmax)
            nstar = jnp.sum(jnp.where(hit, n, 0.0), axis=0, keepdims=True)
            v2n = jnp.sum(jnp.where(i16 == nstar, v2a, 0.0), axis=0, keepdims=True)
            f = jnp.where(hit, v1a + v2n, f)
        cnt = jnp.zeros((nk, tt), F32)
        for i in range(K):
            cnt = jnp.where(key1 == match1[i:i + 1, :], n[i:i + 1, :], cnt)
        c_ref[hh] = cnt
        e1_ref[hh] = jnp.exp(s1 - v1[0]) / zsum
        r2_ref[hh] = r2.astype(BF16)
        e2_ref[hh] = jnp.exp(s2 - v2[0]).astype(BF16)

    for hh0 in range(0, PEER_HEADS, PEER_HG):
        for i, args in enumerate(first_stage(hh0, PEER_HG)):
            second_stage(hh0 + i, *args)


def _peersel_call(sc):
    _, nk, t = sc.shape
    tt = PEER_TT
    ospec = pl.BlockSpec((PEER_HEADS, nk, tt), lambda i: (0, 0, i))
    sd = lambda dt: jax.ShapeDtypeStruct((PEER_HEADS, nk, t), dt)
    return pl.pallas_call(
        _peersel_kernel,
        grid=(t // tt,),
        in_specs=[pl.BlockSpec((2 * PEER_HEADS, nk, tt), lambda i: (0, 0, i))],
        out_specs=[ospec] * 4,
        out_shape=[sd(F32), sd(F32), sd(BF16), sd(BF16)],
        compiler_params=_cparams(("parallel",)),
        name="peersel",
    )(sc)


PEER_TB = 512
PEER_EC = 1024


def _peer_kernel(h2_ref, c_ref, e1_ref, r2_ref, e2_ref, down0_ref, downn_ref, upt_ref, x1_ref, mod_ref, fw_ref,
                 o_ref, acc_ref, act_ref, w_ref):
    j = pl.program_id(1)
    nj = pl.num_programs(1)
    nk = PEER_NKEYS
    na = PEER_EC // nk

    def build_gates(chunk):
        a0 = pl.multiple_of(chunk * na, na)

        def rows(ref, hh, ai):
            grp = ref[hh, pl.ds(a0, na), :]
            r16 = jnp.broadcast_to(grp[ai:ai + 1, :], (16, PEER_TB)).astype(BF16)
            return jnp.concatenate([r16] * (nk // 16), axis=0)

        for ai in range(na):
            w = None
            for hh in range(PEER_HEADS):
                term = jnp.where(r2_ref[hh] < rows(c_ref, hh, ai), e2_ref[hh], 0.0) * rows(e1_ref, hh, ai)
                w = term if w is None else w + term
            w_ref[ai * nk:(ai + 1) * nk, :] = w

    @pl.when(j == 0)
    def _():
        acc_ref[...] = jnp.zeros(acc_ref.shape, F32)

    @pl.when((pl.program_id(0) == 0) & (j == 0))
    def _():
        act_ref[...] = _gelu_tanh(_nt(down0_ref[...], h2_ref[...]).astype(BF16))
        build_gates(0)

    wa = w_ref[...] * act_ref[...]
    acc_ref[...] += jnp.dot(upt_ref[...], wa, preferred_element_type=F32)
    act_ref[...] = _gelu_tanh(_nt(downn_ref[...], h2_ref[...]).astype(BF16))
    build_gates(jnp.where(j == nj - 1, 0, j + 1))

    @pl.when(j == nj - 1)
    def _():
        y = jnp.transpose(acc_ref[...])
        x2 = x1_ref[...] + mod_ref[0, 5:6, :] * y
        ms = jnp.mean(x2 * x2, axis=-1, keepdims=True)
        o_ref[...] = x2 * lax.rsqrt(ms + NORM_EPS) * fw_ref[...]


def _peer_call(h2, cnt, e1, r2, e2, down, upt, x1, mod, fw, s):
    t, d = h2.shape
    ne = down.shape[0]
    tb, ec = PEER_TB, PEER_EC
    per_b = s // tb
    nj = ne // ec
    nblk = t // tb

    def prep(i, j):
        return jnp.minimum(i + (j == nj - 1).astype(jnp.int32), nblk - 1)

    dspec = pl.BlockSpec((PEER_HEADS, PEER_NKEYS, tb), lambda i, j: (0, 0, prep(i, j)))
    return pl.pallas_call(
        _peer_kernel,
        grid=(nblk, nj),
        in_specs=[pl.BlockSpec((tb, d), lambda i, j: (prep(i, j), 0)), dspec, dspec, dspec, dspec,
                  pl.BlockSpec((ec, d), lambda i, j: (0, 0)),
                  pl.BlockSpec((ec, d), lambda i, j: ((j + 1) % nj, 0)),
                  pl.BlockSpec((d, ec), lambda i, j: (0, j)),
                  pl.BlockSpec((tb, d), lambda i, j: (i, 0)),
                  pl.BlockSpec((1, 6, d), lambda i, j: (i // per_b, 0, 0)),
                  pl.BlockSpec((1, d), lambda i, j: (0, 0))],
        out_specs=pl.BlockSpec((tb, d), lambda i, j: (i, 0)),
        out_shape=jax.ShapeDtypeStruct((t, d), F32),
        scratch_shapes=[pltpu.VMEM((d, tb), F32), pltpu.VMEM((ec, tb), BF16), pltpu.VMEM((ec, tb), BF16)],
        compiler_params=_cparams(("arbitrary", "arbitrary")),
        name="peer",
    )(h2, cnt, e1, r2, e2, down, down, upt, x1, mod, fw)


def _rope_tables(s):
    half = ROPE_DIM // 2
    inv_freq = ROPE_THETA ** (-jnp.arange(half, dtype=F32) / half)
    ang = jnp.arange(s).astype(F32)[:, None] * inv_freq[None, :]
    cos, sin = jnp.cos(ang), jnp.sin(ang)
    one = jnp.ones((s, HEAD_DIM - ROPE_DIM), F32)
    cos64 = jnp.concatenate([cos, cos, one], axis=1)
    sin64 = jnp.concatenate([-sin, sin, 0.0 * one], axis=1)
    return jnp.concatenate([cos64, cos64], axis=1), jnp.concatenate([sin64, sin64], axis=1)


def _swap_cols(w):
    d, n = w.shape
    wh = w.reshape(d, n // HEAD_DIM, HEAD_DIM)
    half = ROPE_DIM // 2
    sw = jnp.concatenate([wh[..., half:ROPE_DIM], wh[..., :half], jnp.zeros_like(wh[..., ROPE_DIM:])], axis=-1)
    return sw.reshape(d, n)


def _pack_w_in(w):
    o = np.cumsum((0, 512, 128, 128, 128, 128, 128, 128, 24, 512, 1024, 8))
    q, kc, vc, ksel, vsel, kwin, vwin, gl, z, xbc, dtr = (w[:, o[i]:o[i + 1]] for i in range(11))
    gl = gl.reshape(-1, NSA_GROUPS, NSA_HPG * N_BRANCH)
    gl = jnp.pad(gl, ((0, 0), (0, 0), (0, 128 - NSA_HPG * N_BRANCH))).reshape(-1, 256)
    dtr = jnp.pad(dtr, ((0, 0), (0, 128 - SSD_HEADS)))
    cols = [q, _swap_cols(q), ksel, _swap_cols(ksel), kwin, _swap_cols(kwin), vsel, vwin, kc, vc, gl, z, xbc, dtr]
    return jnp.concatenate(cols, axis=1).astype(BF16)


def _overlap_t(s):
    n_cmp_pad = s // CMP_STRIDE
    n_sel = s // SEL_BLOCK
    cs = np.arange(n_cmp_pad) * CMP_STRIDE
    ss = np.arange(n_sel) * SEL_BLOCK
    ov = np.maximum(np.minimum(cs[None, :] + CMP_BLOCK, ss[:, None] + SEL_BLOCK)
                    - np.maximum(cs[None, :], ss[:, None]), 0).astype(np.float32) / CMP_BLOCK
    ov[:, n_cmp_pad - 1] = 0.0
    return jnp.asarray(ov, BF16)


def _blockdiag_keys(sub_keys):
    h, two, n, half = sub_keys.shape
    z = jnp.zeros((h, n, half), sub_keys.dtype)
    top = jnp.concatenate([sub_keys[:, 0], z], axis=2)
    bot = jnp.concatenate([z, sub_keys[:, 1]], axis=2)
    return jnp.concatenate([top, bot], axis=1).astype(BF16)


def kernel(x, c, w_ada, b_ada, norm1_w, w_in, cmp_pe_k, cmp_pe_v, cmp_w1_k, cmp_w2_k, cmp_w1_v, cmp_w2_v,
           nsa_norm_w, conv_w, conv_b, dt_bias, a_log, d_skip, ssd_norm_w, w_out, norm2_w,
           peer_wq, peer_qnorm_w, peer_sub_keys, peer_down, peer_up, final_norm_w):
    bsz, s, d = x.shape
    assert d == D_MODEL and s % NSA_TK == 0 and s // SEL_BLOCK <= LANES and w_ada.shape[0] == 1
    lyr = 0
    ts = 512

    mod = _mod_call(c, w_ada[lyr], b_ada[lyr]).reshape(bsz, 6, d)
    cos_t, sin_t = _rope_tables(s)
    (qp, qr, ksel, kwin, vsel, vwin, kcmp, vcmp, gates, z, xbc, dtr) = _inproj_call(
        x, mod, norm1_w[lyr].reshape(1, d), _pack_w_in(w_in[lyr]), cos_t, sin_t, ts)

    def w1_groups(w1):
        w = w1.reshape(2, CMP_STRIDE, HEAD_DIM, CMP_HIDDEN)
        z = jnp.zeros_like(w)
        per_g = [jnp.concatenate([w, z], axis=2), jnp.concatenate([z, w], axis=2)]
        return jnp.stack(per_g, axis=1).reshape(2, NSA_GROUPS, CMP_STRIDE * 2 * HEAD_DIM, CMP_HIDDEN).astype(BF16)

    kc, vc = _compress_call(
        kcmp, vcmp, jnp.tile(cmp_pe_k[lyr], (1, NSA_GROUPS)), jnp.tile(cmp_pe_v[lyr], (1, NSA_GROUPS)),
        w1_groups(cmp_w1_k[lyr]), cmp_w2_k[lyr].astype(BF16), w1_groups(cmp_w1_v[lyr]), cmp_w2_v[lyr].astype(BF16))

    o_nsa = _nsa_call(qp, qr, kc, vc, ksel, vsel, kwin, vwin, gates, _overlap_t(s))

    dtt = jnp.transpose(dtr[:, :, :SSD_HEADS], (0, 2, 1))
    o_ssd = _ssd_call(xbc, z, dtr, dtt, conv_w[lyr], conv_b[lyr], dt_bias[lyr], a_log[lyr], d_skip[lyr],
                      ssd_norm_w[lyr])

    x1, h2, sc = _outproj_call(
        o_nsa, o_ssd, x, mod, nsa_norm_w[lyr].reshape(1, -1), norm2_w[lyr].reshape(1, d),
        w_out[lyr].astype(BF16), peer_wq[lyr].astype(BF16), peer_qnorm_w[lyr].reshape(1, -1),
        _blockdiag_keys(peer_sub_keys[lyr]), ts)

    cnt, e1, r2, e2 = _peersel_call(sc)

    out = _peer_call(h2.reshape(bsz * s, d), cnt, e1, r2, e2, peer_down[lyr].astype(BF16),
                     jnp.transpose(peer_up[lyr]).astype(BF16), x1.reshape(bsz * s, d), mod,
                     final_norm_w.reshape(1, d), s)
    return out.reshape(bsz, s, d)
```
